```python
import math
import jax, jax.numpy as jnp
from jax import lax
import numpy as np

D_MODEL = 1024
BATCH = 8
SEQ = 2048
DEPTH = 1

D_MIX = D_MODEL
HEAD_DIM = 64
D_A = D_MIX // 2
D_B = D_MIX - D_A
N_HEADS_A = D_A // HEAD_DIM
N_HEADS_B = D_B // HEAD_DIM
IDX_HEADS = 16
IDX_DIM = 64
TOPK_MAX = 256
NUM_BUCKETS = 32
MAX_DISTANCE = 128
Q_BLOCK = 128
SPARSE_Q_BLOCK = 64
RMS_EPS = 1e-6
IDX_SCALE = (IDX_HEADS * IDX_DIM) ** -0.5

SPLIT_SIZES = (
    D_A, D_A, D_A, D_A,
    IDX_HEADS * IDX_DIM, IDX_DIM, IDX_HEADS,
    D_B, D_B, D_B, D_B,
)
D_IN_PROJ = sum(SPLIT_SIZES)

kernel_name = "hymba_dsa_stickbreaking_hybrid"


def rms_norm(x, gain):
    xf = x.astype(jnp.float32)
    y = xf * lax.rsqrt(jnp.mean(xf * xf, axis=-1, keepdims=True) + RMS_EPS)
    return (y * gain.astype(jnp.float32)).astype(x.dtype)


def split_columns(a):
    offsets = np.cumsum(SPLIT_SIZES)[:-1].tolist()
    return jnp.split(a, offsets, axis=-1)


def t5_bucket(dist):
    max_exact = NUM_BUCKETS // 2
    d = jnp.maximum(dist, 0)
    d_f = jnp.maximum(d, 1).astype(jnp.float32)
    large = max_exact + (jnp.log(d_f / max_exact) / math.log(MAX_DISTANCE / max_exact)
                         * (NUM_BUCKETS - max_exact)).astype(jnp.int32)
    large = jnp.minimum(large, NUM_BUCKETS - 1)
    return jnp.where(d < max_exact, d, large)


def to_blocks(a, block):
    b, l = a.shape[0], a.shape[1]
    return jnp.moveaxis(a.reshape(b, l // block, block, *a.shape[2:]), 1, 0)


def from_blocks(a):
    a = jnp.moveaxis(a, 0, 1)
    return a.reshape(a.shape[0], a.shape[1] * a.shape[2], *a.shape[3:])


def dsa_sparse_attention(q, k, v, q_idx, k_idx, w_idx, rel_bias):
    b, l, h, dh = q.shape
    topk = min(TOPK_MAX, l // 4)
    nb = l // SPARSE_Q_BLOCK
    key_pos = jnp.arange(l, dtype=jnp.int32)
    scale = dh ** -0.5
    gather = jax.vmap(lambda src, idx: src[idx])

    def block(args):
        qb, qib, wb, t0 = args
        t = t0 + jnp.arange(SPARSE_Q_BLOCK, dtype=jnp.int32)
        causal = key_pos[None, :] <= t[:, None]
        dots = jnp.einsum('btid,bsd->btis', qib, k_idx).astype(jnp.float32)
        score = jnp.einsum('bti,btis->bts', wb.astype(jnp.float32) * IDX_SCALE,
                           jax.nn.relu(dots))
        score = jnp.where(causal[None], score, -jnp.inf)
        _, sel = lax.top_k(score, topk)
        valid = sel <= t[None, :, None]
        k_sel = gather(k, sel)
        v_sel = gather(v, sel)
        logits = jnp.einsum('bthd,btkhd->bthk', qb, k_sel).astype(jnp.float32) * scale
        bias = rel_bias[t5_bucket(t[None, :, None] - sel)]
        logits = logits + jnp.moveaxis(bias.astype(jnp.float32), -1, 2)
        logits = jnp.where(valid[:, :, None, :], logits, -jnp.inf)
        p = jax.nn.softmax(logits, axis=-1)
        return jnp.einsum('bthk,btkhd->bthd', p.astype(v.dtype), v_sel)

    starts = jnp.arange(nb, dtype=jnp.int32) * SPARSE_Q_BLOCK
    out = lax.map(block, (to_blocks(q, SPARSE_Q_BLOCK), to_blocks(q_idx, SPARSE_Q_BLOCK),
                          to_blocks(w_idx, SPARSE_Q_BLOCK), starts))
    return from_blocks(out)


def stick_breaking_attention(q, k, v):
    b, l, h, dh = q.shape
    nb = l // Q_BLOCK
    key_pos = jnp.arange(l, dtype=jnp.int32)
    scale = dh ** -0.5

    def block(args):
        qb, t0 = args
        t = t0 + jnp.arange(Q_BLOCK, dtype=jnp.int32)
        strict = key_pos[None, :] < t[:, None]
        z = jnp.einsum('bthd,bshd->bhts', qb, k).astype(jnp.float32) * scale
        log_beta = jax.nn.log_sigmoid(z)
        log_one_minus = jnp.where(strict, jax.nn.log_sigmoid(-z), 0.0)
        suffix = lax.cumsum(log_one_minus, axis=3, reverse=True) - log_one_minus
        a = jnp.where(strict, jnp.exp(log_beta + suffix), 0.0)
        return jnp.einsum('bhts,bshd->bthd', a.astype(v.dtype), v)

    starts = jnp.arange(nb, dtype=jnp.int32) * Q_BLOCK
    out = lax.map(block, (to_blocks(q, Q_BLOCK), starts))
    return from_blocks(out)


def setup_inputs(seed: int = 0) -> dict:
    key = jax.random.key(seed)
    ks = jax.random.split(key, 8)
    x = jax.random.normal(ks[0], (BATCH, SEQ, D_MODEL), jnp.float32)
    norm_gain = 1.0 + 0.05 * jax.random.normal(ks[1], (DEPTH, D_MODEL), jnp.float32)
    w_in = jax.random.normal(ks[2], (DEPTH, D_MODEL, D_IN_PROJ), jnp.float32) * D_MODEL ** -0.5
    q_norm_gain = 1.0 + 0.05 * jax.random.normal(ks[3], (DEPTH, HEAD_DIM), jnp.float32)
    k_norm_gain = 1.0 + 0.05 * jax.random.normal(ks[4], (DEPTH, HEAD_DIM), jnp.float32)
    rel_bias = 0.5 * jax.random.normal(ks[5], (NUM_BUCKETS, N_HEADS_A), jnp.float32)
    w_out = jax.random.normal(ks[6], (DEPTH, D_MIX, D_MODEL), jnp.float32) * D_MIX ** -0.5
    return {"x": x, "norm_gain": norm_gain, "w_in": w_in, "q_norm_gain": q_norm_gain,
            "k_norm_gain": k_norm_gain, "rel_bias": rel_bias, "w_out": w_out}


def reference(x, norm_gain, w_in, q_norm_gain, k_norm_gain, rel_bias, w_out):
    b, l, _ = x.shape
    for layer in range(DEPTH):
        h = rms_norm(x, norm_gain[layer])
        proj = jnp.einsum('bld,dp->blp', h, w_in[layer])
        q_a, k_a, v_a, g_a, q_i, k_i, w_i, q_b, k_b, v_b, g_b = split_columns(proj)
        q_a = rms_norm(q_a.reshape(b, l, N_HEADS_A, HEAD_DIM), q_norm_gain[layer])
        k_a = rms_norm(k_a.reshape(b, l, N_HEADS_A, HEAD_DIM), k_norm_gain[layer])
        v_a = v_a.reshape(b, l, N_HEADS_A, HEAD_DIM)
        q_i = q_i.reshape(b, l, IDX_HEADS, IDX_DIM)
        o_a = dsa_sparse_attention(q_a, k_a, v_a, q_i, k_i, w_i, rel_bias)
        o_a = o_a.reshape(b, l, D_A) * jax.nn.silu(g_a)
        o_b = stick_breaking_attention(q_b.reshape(b, l, N_HEADS_B, HEAD_DIM),
                                       k_b.reshape(b, l, N_HEADS_B, HEAD_DIM),
                                       v_b.reshape(b, l, N_HEADS_B, HEAD_DIM))
        o_b = o_b.reshape(b, l, D_B) * jax.nn.silu(g_b)
        mixed = jnp.concatenate([o_a, o_b], axis=-1)
        x = x + jnp.einsum('blm,md->bld', mixed, w_out[layer])
    return x
```

```python
import functools
import math

import jax
import jax.numpy as jnp
from jax import lax
from jax.experimental import pallas as pl
from jax.experimental.pallas import tpu as pltpu

F32 = jnp.float32
BF16 = jnp.bfloat16
I32 = jnp.int32

HEAD_DIM = 64
IDX_HEADS = 16
IDX_DIM = 64
TOPK_MAX = 256
NUM_BUCKETS = 32
MAX_DISTANCE = 128
RMS_EPS = 1e-6

INT_MIN = -(2 ** 31)
NEG_BIG = -1e30

VMEM_LIMIT_BYTES = 56 * 1024 * 1024
TOKEN_TILE = 512
ATT_TILE = 256

_NT = (((1,), (1,)), ((), ()))


def _params(*sem):
    return pltpu.CompilerParams(dimension_semantics=sem, vmem_limit_bytes=VMEM_LIMIT_BYTES)


def _inproj_kernel(x_ref, gain_ref, w_ref, qg_ref, kg_ref, gsum_ref,
                   qa_ref, ka_ref, va_ref, ga_ref, qi_ref, ki_ref, wi_ref,
                   qb_ref, kb_ref, vb_ref, gb_ref, *, d_a, d_b, d_qi):
    x = x_ref[...]
    ms = jnp.mean(x * x, axis=-1, keepdims=True)
    h = (x * lax.rsqrt(ms + RMS_EPS) * gain_ref[...]).astype(BF16)

    def proj(c0, width):
        return jnp.dot(h, w_ref[:, c0:c0 + width], preferred_element_type=F32)

    def head_norm(y, g):
        sq = y * y
        hi = sq.astype(BF16)
        lo = (sq - hi.astype(F32)).astype(BF16)
        ssum = (jnp.dot(hi, gsum_ref[...], preferred_element_type=F32)
                + jnp.dot(lo, gsum_ref[...], preferred_element_type=F32))
        return y * lax.rsqrt(ssum * (1.0 / HEAD_DIM) + RMS_EPS) * g

    def silu(g):
        return g * (1.0 / (1.0 + jnp.exp(-g)))

    c = 0
    qa_ref[...] = head_norm(proj(c, d_a), qg_ref[...]).astype(BF16); c += d_a
    ka_ref[...] = head_norm(proj(c, d_a), kg_ref[...]).astype(BF16); c += d_a
    va_ref[...] = proj(c, d_a).astype(BF16); c += d_a
    ga_ref[...] = silu(proj(c, d_a)); c += d_a
    qi_ref[...] = proj(c, d_qi).astype(BF16); c += d_qi
    kw = proj(c, 128); c += 128
    ki_ref[...] = kw[:, :IDX_DIM].astype(BF16)
    wi_ref[...] = kw[:, IDX_DIM:IDX_DIM + IDX_HEADS]
    qb_ref[...] = proj(c, d_b).astype(BF16); c += d_b
    kb_ref[...] = proj(c, d_b).astype(BF16); c += d_b
    vb_ref[...] = proj(c, d_b).astype(BF16); c += d_b
    gb_ref[...] = silu(proj(c, d_b))


def _inproj(x2, gain, w_all, qg, kg, gsum, d_a, d_b, d_qi):
    n, d = x2.shape
    tm = TOKEN_TILE
    row = lambda width: pl.BlockSpec((tm, width), lambda i: (i, 0))
    const = lambda shape: pl.BlockSpec(shape, lambda i: (0, 0))
    out_shapes = [
        jax.ShapeDtypeStruct((n, d_a), BF16), jax.ShapeDtypeStruct((n, d_a), BF16),
        jax.ShapeDtypeStruct((n, d_a), BF16), jax.ShapeDtypeStruct((n, d_a), F32),
        jax.ShapeDtypeStruct((n, d_qi), BF16), jax.ShapeDtypeStruct((n, IDX_DIM), BF16),
        jax.ShapeDtypeStruct((n, IDX_HEADS), F32),
        jax.ShapeDtypeStruct((n, d_b), BF16), jax.ShapeDtypeStruct((n, d_b), BF16),
        jax.ShapeDtypeStruct((n, d_b), BF16), jax.ShapeDtypeStruct((n, d_b), F32),
    ]
    out_specs = [row(d_a), row(d_a), row(d_a), row(d_a), row(d_qi), row(IDX_DIM),
                 row(IDX_HEADS), row(d_b), row(d_b), row(d_b), row(d_b)]
    return pl.pallas_call(
        functools.partial(_inproj_kernel, d_a=d_a, d_b=d_b, d_qi=d_qi),
        grid=(n // tm,),
        in_specs=[row(d), const(gain.shape),
                  pl.BlockSpec(w_all.shape, lambda i: (0, 0), pipeline_mode=pl.Buffered(1)),
                  const(qg.shape), const(kg.shape), const(gsum.shape)],
        out_specs=out_specs,
        out_shape=out_shapes,
        compiler_params=_params("arbitrary"),
    )(x2, gain, w_all, qg, kg, gsum)


def _t5_bucket(dist):
    max_exact = NUM_BUCKETS // 2
    d = jnp.maximum(dist, 0)
    d_f = jnp.maximum(d, 1).astype(F32)
    large = max_exact + (jnp.log(d_f / max_exact) / math.log(MAX_DISTANCE / max_exact)
                         * (NUM_BUCKETS - max_exact)).astype(I32)
    large = jnp.minimum(large, NUM_BUCKETS - 1)
    return jnp.where(d < max_exact, d, large)


def _t5_table_kernel(rb_ref, tab_ref, *, n_heads):
    ts = tab_ref.shape[2]
    s_rel = lax.broadcasted_iota(I32, (ts, ts), 0)
    t_rel = lax.broadcasted_iota(I32, (ts, ts), 1)
    for i in range(3):
        bucket = _t5_bucket(t_rel - s_rel + i * ts)
        for h in range(n_heads):
            tab_ref[i, h] = jnp.zeros((ts, ts), F32)

        def body(j, _):
            hit = bucket == j
            for h in range(n_heads):
                tab_ref[i, h] = jnp.where(hit, rb_ref[j, h], tab_ref[i, h])
            return 0

        lax.fori_loop(0, NUM_BUCKETS, body, 0)


def _t5_table(rel_bias, ts):
    n_heads = rel_bias.shape[1]
    return pl.pallas_call(
        functools.partial(_t5_table_kernel, n_heads=n_heads),
        in_specs=[pl.BlockSpec(memory_space=pltpu.SMEM)],
        out_specs=pl.BlockSpec(memory_space=pltpu.VMEM),
        out_shape=jax.ShapeDtypeStruct((3, n_heads, ts, ts), F32),
        compiler_params=pltpu.CompilerParams(vmem_limit_bytes=VMEM_LIMIT_BYTES),
    )(rel_bias)


def _dsa_kernel(ki_ref, qi_ref, wit_ref, ka_ref, vat_ref, qa_ref, ga_ref, tab_ref, o_ref,
                keys_ref, madd_ref, m_ref, l_ref, acc_ref, *, topk, n_heads):
    tq = o_ref.shape[1]
    ts = tq
    q_blk = pl.program_id(1)
    nblk = q_blk + 1
    s_rel = lax.broadcasted_iota(I32, (ts, tq), 0)
    t_idx = q_blk * tq + lax.broadcasted_iota(I32, (ts, tq), 1)

    def block_start(j):
        return pl.multiple_of(j * ts, ts)

    def idx_body(j, _):
        s0 = block_start(j)
        kblk = ki_ref[0, pl.ds(s0, ts), :]
        score = jnp.zeros((ts, tq), F32)
        for h in range(IDX_HEADS):
            dots = lax.dot_general(kblk, qi_ref[0, h], _NT, preferred_element_type=F32)
            score = score + jnp.maximum(dots, 0.0) * wit_ref[0, h:h + 1, :]
        bits = lax.bitcast_convert_type(score, I32)
        key = bits ^ ((bits >> 31) & 0x7FFFFFFF)
        keys_ref[pl.ds(s0, ts), :] = jnp.where(s0 + s_rel > t_idx, INT_MIN, key)
        return 0

    lax.fori_loop(0, nblk, idx_body, 0)

    def count(pred_fn):
        def body(j, c):
            s0 = block_start(j)
            hit = pred_fn(keys_ref[pl.ds(s0, ts), :], s0).astype(I32)
            return c + hit.reshape(ts // 8, 8, tq).sum(axis=0)
        c8 = lax.fori_loop(0, nblk, body, jnp.zeros((8, tq), I32))
        return jnp.sum(c8, axis=0, keepdims=True)

    def bit_body(i, carry):
        thr, cnt_thr = carry
        cand = thr + jnp.left_shift(jnp.int32(1), 31 - i)
        cnt = count(lambda blk, s0: blk >= cand)
        take = cnt >= topk
        return jnp.where(take, cand, thr), jnp.where(take, cnt, cnt_thr)

    thr, cnt_thr = lax.fori_loop(
        0, 32, bit_body,
        (jnp.full((1, tq), INT_MIN, I32), jnp.zeros((1, tq), I32) + nblk * ts))

    tie = jnp.max(jnp.where((cnt_thr != topk) & (thr > INT_MIN), 1, 0)) > 0

    @pl.when(jnp.logical_not(tie))
    def _():
        floor = jnp.maximum(thr, INT_MIN + 1)

        def body(j, _):
            s0 = block_start(j)
            madd_ref[pl.ds(s0, ts), :] = jnp.where(keys_ref[pl.ds(s0, ts), :] >= floor, 0.0, NEG_BIG)
            return 0

        lax.fori_loop(0, nblk, body, 0)

    @pl.when(tie)
    def _():
        need = topk - count(lambda blk, s0: blk > thr)
        n_bits = keys_ref.shape[0].bit_length()

        def jbody(i, bound):
            cand = bound + jnp.left_shift(jnp.int32(1), n_bits - 1 - i)
            cnt = count(lambda blk, s0: (blk == thr) & (s0 + s_rel < cand))
            return jnp.where(cnt <= need, cand, bound)

        bound = lax.fori_loop(0, n_bits, jbody, jnp.zeros((1, tq), I32))

        def body(j, _):
            s0 = block_start(j)
            blk = keys_ref[pl.ds(s0, ts), :]
            sel = ((blk > thr) | ((blk == thr) & (s0 + s_rel < bound))) & (blk > INT_MIN)
            madd_ref[pl.ds(s0, ts), :] = jnp.where(sel, 0.0, NEG_BIG)
            return 0

        lax.fori_loop(0, nblk, body, 0)

    m_ref[...] = jnp.full(m_ref.shape, NEG_BIG, F32)
    l_ref[...] = jnp.zeros(l_ref.shape, F32)
    acc_ref[...] = jnp.zeros(acc_ref.shape, F32)

    def att_body(j, _):
        s0 = block_start(j)
        madd = madd_ref[pl.ds(s0, ts), :]
        near = jnp.minimum(q_blk - j, 2)
        for h in range(n_heads):
            rows = slice(h * HEAD_DIM, (h + 1) * HEAD_DIM)
            lg = lax.dot_general(ka_ref[0, h, pl.ds(s0, ts), :], qa_ref[0, h], _NT,
                                 preferred_element_type=F32)
            lg = lg + tab_ref[near, h] + madd
            m_old = m_ref[h:h + 1, :]
            m_new = jnp.maximum(m_old, jnp.max(lg, axis=0, keepdims=True))
            alpha = jnp.exp(m_old - m_new)
            p = jnp.exp(lg - m_new)
            l_ref[h:h + 1, :] = alpha * l_ref[h:h + 1, :] + jnp.sum(p, axis=0, keepdims=True)
            pv = jnp.dot(vat_ref[0, j, rows, :], p.astype(BF16), preferred_element_type=F32)
            acc_ref[rows, :] = alpha * acc_ref[rows, :] + pv
            m_ref[h:h + 1, :] = m_new
        return 0

    lax.fori_loop(0, nblk, att_body, 0)

    inv = 1.0 / l_ref[...]
    for h in range(n_heads):
        rows = slice(h * HEAD_DIM, (h + 1) * HEAD_DIM)
        acc_ref[rows, :] = acc_ref[rows, :] * inv[h:h + 1, :]
    o_ref[0] = (acc_ref[...].T * ga_ref[0]).astype(BF16)


def _dsa(ki, qi_h, wit, ka_h, vat, qa_h, ga, tab, topk):
    b, n_heads, l, _ = ka_h.shape
    d_a = n_heads * HEAD_DIM
    tq = ATT_TILE
    nq = l // tq
    kernel = functools.partial(_dsa_kernel, topk=topk, n_heads=n_heads)
    return pl.pallas_call(
        kernel,
        grid=(b, nq),
        in_specs=[
            pl.BlockSpec((1, l, IDX_DIM), lambda i, j: (i, 0, 0)),
            pl.BlockSpec((1, IDX_HEADS, tq, IDX_DIM), lambda i, j: (i, 0, j, 0)),
            pl.BlockSpec((1, IDX_HEADS, tq), lambda i, j: (i, 0, j)),
            pl.BlockSpec((1, n_heads, l, HEAD_DIM), lambda i, j: (i, 0, 0, 0)),
            pl.BlockSpec((1, nq, d_a, tq), lambda i, j: (i, 0, 0, 0)),
            pl.BlockSpec((1, n_heads, tq, HEAD_DIM), lambda i, j: (i, 0, j, 0)),
            pl.BlockSpec((1, tq, d_a), lambda i, j: (i, j, 0)),
            pl.BlockSpec(tab.shape, lambda i, j: (0, 0, 0, 0), pipeline_mode=pl.Buffered(1)),
        ],
        out_specs=pl.BlockSpec((1, tq, d_a), lambda i, j: (i, j, 0)),
        out_shape=jax.ShapeDtypeStruct((b, l, d_a), BF16),
        scratch_shapes=[
            pltpu.VMEM((l, tq), I32),
            pltpu.VMEM((l, tq), F32),
            pltpu.VMEM((n_heads, tq), F32),
            pltpu.VMEM((n_heads, tq), F32),
            pltpu.VMEM((d_a, tq), F32),
        ],
        compiler_params=_params("arbitrary", "arbitrary"),
    )(ki, qi_h, wit, ka_h, vat, qa_h, ga, tab)


def _stick_kernel(kb_ref, vbt_ref, qb_ref, gb_ref, u_ref, o_ref, acc_ref, *, n_heads):
    tq = o_ref.shape[1]
    ts = tq
    q_blk = pl.program_id(1)
    nblk = q_blk + 1
    s_rel = lax.broadcasted_iota(I32, (ts, tq), 0)
    t_idx = q_blk * tq + lax.broadcasted_iota(I32, (ts, tq), 1)

    for h in range(n_heads):
        rows = slice(h * HEAD_DIM, (h + 1) * HEAD_DIM)
        qh = qb_ref[0, h]

        def body(i, carry, rows=rows, qh=qh, h=h):
            below, acc = carry
            j = q_blk - i
            s0 = pl.multiple_of(j * ts, ts)
            z = lax.dot_general(kb_ref[0, h, pl.ds(s0, ts), :], qh, _NT, preferred_element_type=F32)
            strict = s0 + s_rel < t_idx
            sp = jnp.log1p(jnp.exp(-jnp.abs(z)))
            log_beta = jnp.minimum(z, 0.0) - sp
            log_om = jnp.where(strict, -jnp.maximum(z, 0.0) - sp, 0.0)
            hi = log_om.astype(BF16)
            lo = (log_om - hi.astype(F32)).astype(BF16)
            suffix = jnp.dot(u_ref[...], jnp.concatenate([hi, lo], axis=0), preferred_element_type=F32)
            a = jnp.where(strict, jnp.exp(log_beta + suffix + below), 0.0)
            acc = acc + jnp.dot(vbt_ref[0, j, rows, :], a.astype(BF16), preferred_element_type=F32)
            return below + suffix[0:1, :] + log_om[0:1, :], acc

        _, acc = lax.fori_loop(0, nblk, body,
                               (jnp.zeros((1, tq), F32), jnp.zeros((HEAD_DIM, tq), F32)))
        acc_ref[rows, :] = acc

    o_ref[0] = (acc_ref[...].T * gb_ref[0]).astype(BF16)


def _stick(kb_h, vbt, qb_h, gb, u2):
    b, n_heads, l, _ = kb_h.shape
    d_b = n_heads * HEAD_DIM
    tq = ATT_TILE
    nq = l // tq
    return pl.pallas_call(
        functools.partial(_stick_kernel, n_heads=n_heads),
        grid=(b, nq),
        in_specs=[
            pl.BlockSpec((1, n_heads, l, HEAD_DIM), lambda i, j: (i, 0, 0, 0)),
            pl.BlockSpec((1, nq, d_b, tq), lambda i, j: (i, 0, 0, 0)),
            pl.BlockSpec((1, n_heads, tq, HEAD_DIM), lambda i, j: (i, 0, j, 0)),
            pl.BlockSpec((1, tq, d_b), lambda i, j: (i, j, 0)),
            pl.BlockSpec(u2.shape, lambda i, j: (0, 0)),
        ],
        out_specs=pl.BlockSpec((1, tq, d_b), lambda i, j: (i, j, 0)),
        out_shape=jax.ShapeDtypeStruct((b, l, d_b), BF16),
        scratch_shapes=[pltpu.VMEM((d_b, tq), F32)],
        compiler_params=_params("arbitrary", "arbitrary"),
    )(kb_h, vbt, qb_h, gb, u2)


def _outproj_kernel(x_ref, oa_ref, ob_ref, wa_ref, wb_ref, y_ref):
    y_ref[...] = (x_ref[...]
                  + jnp.dot(oa_ref[...], wa_ref[...], preferred_element_type=F32)
                  + jnp.dot(ob_ref[...], wb_ref[...], preferred_element_type=F32))


def _outproj(x2, oa, ob, wa, wb):
    n, d = x2.shape
    tm = TOKEN_TILE
    row = lambda width: pl.BlockSpec((tm, width), lambda i: (i, 0))
    return pl.pallas_call(
        _outproj_kernel,
        grid=(n // tm,),
        in_specs=[row(d), row(oa.shape[1]), row(ob.shape[1]),
                  pl.BlockSpec(wa.shape, lambda i: (0, 0)),
                  pl.BlockSpec(wb.shape, lambda i: (0, 0))],
        out_specs=row(d),
        out_shape=jax.ShapeDtypeStruct((n, d), F32),
        compiler_params=_params("arbitrary"),
    )(x2, oa, ob, wa, wb)


def _heads_major(a, b, l, n_heads):
    return a.reshape(b, l, n_heads, HEAD_DIM).transpose(0, 2, 1, 3)


def _key_blocks_t(a, b, l, ts):
    return a.reshape(b, l // ts, ts, a.shape[-1]).transpose(0, 1, 3, 2)


def kernel(x, norm_gain, w_in, q_norm_gain, k_norm_gain, rel_bias, w_out):
    b, l, d = x.shape
    depth = w_in.shape[0]
    d_a = d // 2
    d_b = d - d_a
    h_a = d_a // HEAD_DIM
    h_b = d_b // HEAD_DIM
    d_qi = IDX_HEADS * IDX_DIM
    topk = min(TOPK_MAX, l // 4)
    ts = ATT_TILE
    scale = HEAD_DIM ** -0.5
    idx_scale = (IDX_HEADS * IDX_DIM) ** -0.5

    gsum = (jnp.arange(d_a)[:, None] // HEAD_DIM == jnp.arange(d_a)[None, :] // HEAD_DIM).astype(BF16)
    col = jnp.arange(2 * ts)[None, :] % ts
    u2 = (col > jnp.arange(ts)[:, None]).astype(BF16)
    tab = _t5_table(rel_bias.astype(F32), ts)

    x2 = x.reshape(b * l, d)
    for layer in range(depth):
        w = w_in[layer]
        o_ki = 4 * d_a + d_qi
        o_b = o_ki + IDX_DIM + IDX_HEADS
        w_kw = jnp.pad(w[:, o_ki:o_b], ((0, 0), (0, 128 - IDX_DIM - IDX_HEADS)))
        w_kw = w_kw.at[:, IDX_DIM:IDX_DIM + IDX_HEADS].multiply(idx_scale)
        w_b = w[:, o_b:]
        w_b = jnp.concatenate([w_b[:, :d_b] * scale, w_b[:, d_b:]], axis=1)
        w_all = jnp.concatenate([w[:, :o_ki], w_kw, w_b], axis=1).astype(BF16)
        qg = jnp.tile(q_norm_gain[layer] * scale, h_a)[None, :]
        kg = jnp.tile(k_norm_gain[layer], h_a)[None, :]

        (qa, ka, va, ga, qi, ki, wi, qb, kb, vb, gb) = _inproj(
            x2, norm_gain[layer][None, :], w_all, qg, kg, gsum, d_a, d_b, d_qi)

        oa = _dsa(ki.reshape(b, l, IDX_DIM),
                  _heads_major(qi, b, l, IDX_HEADS),
                  wi.reshape(b, l, IDX_HEADS).transpose(0, 2, 1),
                  _heads_major(ka, b, l, h_a),
                  _key_blocks_t(va, b, l, ts),
                  _heads_major(qa, b, l, h_a),
                  ga.reshape(b, l, d_a), tab, topk)
        ob = _stick(_heads_major(kb, b, l, h_b), _key_blocks_t(vb, b, l, ts),
                    _heads_major(qb, b, l, h_b), gb.reshape(b, l, d_b), u2)

        w_o = w_out[layer].astype(BF16)
        x2 = _outproj(x2, oa.reshape(b * l, d_a), ob.reshape(b * l, d_b), w_o[:d_a], w_o[d_a:])
    return x2.reshape(b, l, d)
```

```python
import functools
import math

import jax
import jax.numpy as jnp
from jax import lax
from jax.experimental import pallas as pl
from jax.experimental.pallas import tpu as pltpu

F32 = jnp.float32
BF16 = jnp.bfloat16
I32 = jnp.int32

HEAD_DIM = 64
IDX_HEADS = 16
IDX_DIM = 64
TOPK_MAX = 256
NUM_BUCKETS = 32
MAX_DISTANCE = 128
RMS_EPS = 1e-6

LANES = 128
INT_MIN = -(2 ** 31)
NEG_BIG = -1e30
EXP_UNDERFLOW = -104.0

VMEM_LIMIT_BYTES = 56 * 1024 * 1024
TOKEN_TILE = 512
ATT_TILE = 256

_NT = (((1,), (1,)), ((), ()))


def _params(*sem):
    return pltpu.CompilerParams(dimension_semantics=sem, vmem_limit_bytes=VMEM_LIMIT_BYTES)


def _pair(h):
    return slice((h // 2) * LANES, (h // 2 + 1) * LANES)


def _store_head_masked(dst_ref, src, n_heads):
    tq = src.shape[0]
    low = lax.broadcasted_iota(I32, (tq, LANES), 1) < HEAD_DIM
    for h in range(n_heads):
        pair = src[:, _pair(h)].astype(F32)
        keep = low if h % 2 == 0 else jnp.logical_not(low)
        dst_ref[h] = jnp.where(keep, pair, 0.0).astype(BF16)


def _inproj_kernel(x_ref, gain_ref, w_ref, qg_ref, kg_ref, gsum_ref,
                   qa_ref, ka_ref, vat_ref, ga_ref, qi_ref, ki_ref, wit_ref,
                   qb_ref, kb_ref, vbt_ref, gb_ref, *, d_a, d_b, d_qi):
    x = x_ref[0]
    ms = jnp.mean(x * x, axis=-1, keepdims=True)
    h = (x * lax.rsqrt(ms + RMS_EPS) * gain_ref[...]).astype(BF16)
    ts = vat_ref.shape[3]

    def proj(c0, width):
        return jnp.dot(h, w_ref[:, c0:c0 + width], preferred_element_type=F32)

    def head_norm(y, g):
        sq = y * y
        hi = sq.astype(BF16)
        lo = (sq - hi.astype(F32)).astype(BF16)
        ssum = (jnp.dot(hi, gsum_ref[...], preferred_element_type=F32)
                + jnp.dot(lo, gsum_ref[...], preferred_element_type=F32))
        return y * lax.rsqrt(ssum * (1.0 / HEAD_DIM) + RMS_EPS) * g

    def silu(g):
        return g * (1.0 / (1.0 + jnp.exp(-g)))

    def store_key_blocks_t(dst_ref, v):
        vt = v.T.astype(BF16)
        for c in range(dst_ref.shape[1]):
            dst_ref[0, c] = vt[:, c * ts:(c + 1) * ts]

    c = 0
    qa_ref[0] = head_norm(proj(c, d_a), qg_ref[...]).astype(BF16); c += d_a
    ka_ref[0] = head_norm(proj(c, d_a), kg_ref[...]).astype(BF16); c += d_a
    store_key_blocks_t(vat_ref, proj(c, d_a)); c += d_a
    ga_ref[0] = silu(proj(c, d_a)); c += d_a
    qi_ref[0] = proj(c, d_qi).astype(BF16); c += d_qi
    ki_ref[0] = proj(c, LANES).astype(BF16); c += LANES
    wit_ref[0] = proj(c, LANES).T[:IDX_HEADS, :]; c += LANES
    qb_ref[0] = proj(c, d_b).astype(BF16); c += d_b
    kb_ref[0] = proj(c, d_b).astype(BF16); c += d_b
    store_key_blocks_t(vbt_ref, proj(c, d_b)); c += d_b
    gb_ref[0] = silu(proj(c, d_b))


def _inproj(x, gain, w_all, qg, kg, gsum, d_a, d_b, d_qi):
    b, l, d = x.shape
    tm = TOKEN_TILE
    ts = ATT_TILE
    nb = l // ts
    row = lambda width: pl.BlockSpec((1, tm, width), lambda i, j: (i, j, 0))
    const = lambda shape: pl.BlockSpec(shape, lambda i, j: (0, 0))
    vt_spec = lambda ch: pl.BlockSpec((1, tm // ts, ch, ts), lambda i, j: (i, j, 0, 0))
    tok = lambda width, dt: jax.ShapeDtypeStruct((b, l, width), dt)
    out_shapes = [
        tok(d_a, BF16), tok(d_a, BF16), jax.ShapeDtypeStruct((b, nb, d_a, ts), BF16), tok(d_a, F32),
        tok(d_qi, BF16), tok(LANES, BF16), jax.ShapeDtypeStruct((b, IDX_HEADS, l), F32),
        tok(d_b, BF16), tok(d_b, BF16), jax.ShapeDtypeStruct((b, nb, d_b, ts), BF16), tok(d_b, F32),
    ]
    out_specs = [row(d_a), row(d_a), vt_spec(d_a), row(d_a), row(d_qi), row(LANES),
                 pl.BlockSpec((1, IDX_HEADS, tm), lambda i, j: (i, 0, j)),
                 row(d_b), row(d_b), vt_spec(d_b), row(d_b)]
    return pl.pallas_call(
        functools.partial(_inproj_kernel, d_a=d_a, d_b=d_b, d_qi=d_qi),
        grid=(b, l // tm),
        in_specs=[row(d), const(gain.shape),
                  pl.BlockSpec(w_all.shape, lambda i, j: (0, 0), pipeline_mode=pl.Buffered(1)),
                  const(qg.shape), const(kg.shape), const(gsum.shape)],
        out_specs=out_specs,
        out_shape=out_shapes,
        compiler_params=_params("arbitrary", "arbitrary"),
        name="inproj",
    )(x, gain, w_all, qg, kg, gsum)


def _t5_bucket(dist):
    max_exact = NUM_BUCKETS // 2
    d = jnp.maximum(dist, 0)
    d_f = jnp.maximum(d, 1).astype(F32)
    large = max_exact + (jnp.log(d_f / max_exact) / math.log(MAX_DISTANCE / max_exact)
                         * (NUM_BUCKETS - max_exact)).astype(I32)
    large = jnp.minimum(large, NUM_BUCKETS - 1)
    return jnp.where(d < max_exact, d, large)


def _t5_table_kernel(rb_ref, tab_ref, *, n_heads):
    ts = tab_ref.shape[2]
    s_rel = lax.broadcasted_iota(I32, (ts, ts), 0)
    t_rel = lax.broadcasted_iota(I32, (ts, ts), 1)
    for i in range(3):
        bucket = _t5_bucket(t_rel - s_rel + i * ts)
        for h in range(n_heads):
            tab_ref[i, h] = jnp.zeros((ts, ts), F32)

        def body(j, _):
            hit = bucket == j
            for h in range(n_heads):
                tab_ref[i, h] = jnp.where(hit, rb_ref[j, h], tab_ref[i, h])
            return 0

        lax.fori_loop(0, NUM_BUCKETS, body, 0)


def _t5_table(rel_bias, ts):
    n_heads = rel_bias.shape[1]
    return pl.pallas_call(
        functools.partial(_t5_table_kernel, n_heads=n_heads),
        in_specs=[pl.BlockSpec(memory_space=pltpu.SMEM)],
        out_specs=pl.BlockSpec(memory_space=pltpu.VMEM),
        out_shape=jax.ShapeDtypeStruct((3, n_heads, ts, ts), F32),
        compiler_params=pltpu.CompilerParams(vmem_limit_bytes=VMEM_LIMIT_BYTES),
        name="t5_table",
    )(rel_bias)


def _dsa_kernel(ki_ref, qi_ref, wit_ref, ka_ref, vat_ref, qa_ref, ga_ref, tab_ref, o_ref,
                keys_ref, madd_ref, qim_ref, qam_ref, m_ref, l_ref, acc_ref, *, topk, n_heads):
    tq = o_ref.shape[1]
    ts = tq
    q_blk = pl.program_id(1)
    nblk = q_blk + 1
    s_rel = lax.broadcasted_iota(I32, (ts, tq), 0)
    t_idx = q_blk * tq + lax.broadcasted_iota(I32, (ts, tq), 1)

    def block_start(j):
        return pl.multiple_of(j * ts, ts)

    _store_head_masked(qim_ref, qi_ref[0], IDX_HEADS)
    _store_head_masked(qam_ref, qa_ref[0], n_heads)

    def idx_body(j, _):
        s0 = block_start(j)
        kblk = ki_ref[0, pl.ds(s0, ts), :]
        score = jnp.zeros((ts, tq), F32)
        for h in range(IDX_HEADS):
            dots = lax.dot_general(kblk, qim_ref[h], _NT, preferred_element_type=F32)
            score = score + jnp.maximum(dots, 0.0) * wit_ref[0, h:h + 1, :]
        bits = lax.bitcast_convert_type(score, I32)
        key = bits ^ ((bits >> 31) & 0x7FFFFFFF)
        keys_ref[pl.ds(s0, ts), :] = jnp.where(s0 + s_rel > t_idx, INT_MIN, key)
        return 0

    lax.fori_loop(0, nblk, idx_body, 0)

    def count(pred_fn):
        def body(j, c):
            s0 = block_start(j)
            hit = pred_fn(keys_ref[pl.ds(s0, ts), :], s0).astype(I32)
            return c + hit.reshape(ts // 8, 8, tq).sum(axis=0)
        c8 = lax.fori_loop(0, nblk, body, jnp.zeros((8, tq), I32))
        return jnp.sum(c8, axis=0, keepdims=True)

    def bit_body(i, carry):
        thr, cnt_thr = carry
        cand = thr + jnp.left_shift(jnp.int32(1), 31 - i)
        cnt = count(lambda blk, s0: blk >= cand)
        take = cnt >= topk
        return jnp.where(take, cand, thr), jnp.where(take, cnt, cnt_thr)

    thr, cnt_thr = lax.fori_loop(
        0, 32, bit_body,
        (jnp.full((1, tq), INT_MIN, I32), jnp.zeros((1, tq), I32) + nblk * ts))

    tie = jnp.max(jnp.where((cnt_thr != topk) & (thr > INT_MIN), 1, 0)) > 0

    @pl.when(jnp.logical_not(tie))
    def _():
        floor = jnp.maximum(thr, INT_MIN + 1)

        def body(j, _):
            s0 = block_start(j)
            madd_ref[pl.ds(s0, ts), :] = jnp.where(keys_ref[pl.ds(s0, ts), :] >= floor, 0.0, NEG_BIG)
            return 0

        lax.fori_loop(0, nblk, body, 0)

    @pl.when(tie)
    def _():
        need = topk - count(lambda blk, s0: blk > thr)
        n_bits = keys_ref.shape[0].bit_length()

        def jbody(i, bound):
            cand = bound + jnp.left_shift(jnp.int32(1), n_bits - 1 - i)
            cnt = count(lambda blk, s0: (blk == thr) & (s0 + s_rel < cand))
            return jnp.where(cnt <= need, cand, bound)

        bound = lax.fori_loop(0, n_bits, jbody, jnp.zeros((1, tq), I32))

        def body(j, _):
            s0 = block_start(j)
            blk = keys_ref[pl.ds(s0, ts), :]
            sel = ((blk > thr) | ((blk == thr) & (s0 + s_rel < bound))) & (blk > INT_MIN)
            madd_ref[pl.ds(s0, ts), :] = jnp.where(sel, 0.0, NEG_BIG)
            return 0

        lax.fori_loop(0, nblk, body, 0)

    m_ref[...] = jnp.full(m_ref.shape, NEG_BIG, F32)
    l_ref[...] = jnp.zeros(l_ref.shape, F32)
    acc_ref[...] = jnp.zeros(acc_ref.shape, F32)

    def att_body(j, _):
        s0 = block_start(j)
        madd = madd_ref[pl.ds(s0, ts), :]
        near = jnp.minimum(q_blk - j, 2)

        def logits(h):
            return lax.dot_general(ka_ref[0, pl.ds(s0, ts), _pair(h)], qam_ref[h], _NT,
                                   preferred_element_type=F32)

        lg_next = logits(0)
        for h in range(n_heads):
            rows = slice(h * HEAD_DIM, (h + 1) * HEAD_DIM)
            lg = lg_next + tab_ref[near, h] + madd
            if h + 1 < n_heads:
                lg_next = logits(h + 1)
            m_old = m_ref[h]
            m_new = jnp.maximum(m_old, jnp.max(lg, axis=0, keepdims=True))
            alpha = jnp.exp(m_old - m_new)
            p = jnp.exp(lg - m_new)
            l_ref[h] = alpha * l_ref[h] + jnp.sum(p, axis=0, keepdims=True)
            pv = jnp.dot(vat_ref[0, j, rows, :], p.astype(BF16), preferred_element_type=F32)
            acc_ref[rows, :] = alpha * acc_ref[rows, :] + pv
            m_ref[h] = m_new
        return 0

    lax.fori_loop(0, nblk, att_body, 0)

    for h in range(n_heads):
        rows = slice(h * HEAD_DIM, (h + 1) * HEAD_DIM)
        acc_ref[rows, :] = acc_ref[rows, :] * (1.0 / l_ref[h])
    o_ref[0] = (acc_ref[...].T * ga_ref[0]).astype(BF16)


def _dsa(ki, qi, wit, ka, vat, qa, ga, tab, topk):
    b, l, d_a = ka.shape
    n_heads = d_a // HEAD_DIM
    d_qi = qi.shape[2]
    tq = ATT_TILE
    nq = l // tq
    kernel = functools.partial(_dsa_kernel, topk=topk, n_heads=n_heads)
    return pl.pallas_call(
        kernel,
        grid=(b, nq),
        in_specs=[
            pl.BlockSpec((1, l, LANES), lambda i, j: (i, 0, 0)),
            pl.BlockSpec((1, tq, d_qi), lambda i, j: (i, j, 0)),
            pl.BlockSpec((1, IDX_HEADS, tq), lambda i, j: (i, 0, j)),
            pl.BlockSpec((1, l, d_a), lambda i, j: (i, 0, 0)),
            pl.BlockSpec((1, nq, d_a, tq), lambda i, j: (i, 0, 0, 0)),
            pl.BlockSpec((1, tq, d_a), lambda i, j: (i, j, 0)),
            pl.BlockSpec((1, tq, d_a), lambda i, j: (i, j, 0)),
            pl.BlockSpec(tab.shape, lambda i, j: (0, 0, 0, 0), pipeline_mode=pl.Buffered(1)),
        ],
        out_specs=pl.BlockSpec((1, tq, d_a), lambda i, j: (i, j, 0)),
        out_shape=jax.ShapeDtypeStruct((b, l, d_a), BF16),
        scratch_shapes=[
            pltpu.VMEM((l, tq), I32),
            pltpu.VMEM((l, tq), F32),
            pltpu.VMEM((IDX_HEADS, tq, LANES), BF16),
            pltpu.VMEM((n_heads, tq, LANES), BF16),
            pltpu.VMEM((n_heads, 1, tq), F32),
            pltpu.VMEM((n_heads, 1, tq), F32),
            pltpu.VMEM((d_a, tq), F32),
        ],
        compiler_params=_params("arbitrary", "arbitrary"),
        name="dsa",
    )(ki, qi, wit, ka, vat, qa, ga, tab)


def _stick_kernel(kb_ref, vbt_ref, qb_ref, gb_ref, u_ref, o_ref,
                  qbm_ref, below_ref, acc_ref, *, n_heads):
    tq = o_ref.shape[1]
    ts = tq
    q_blk = pl.program_id(1)
    s_rel = lax.broadcasted_iota(I32, (ts, tq), 0)
    t_rel = lax.broadcasted_iota(I32, (ts, tq), 1)

    _store_head_masked(qbm_ref, qb_ref[0], n_heads)
    below_ref[...] = jnp.zeros(below_ref.shape, F32)
    acc_ref[...] = jnp.zeros(acc_ref.shape, F32)

    def block(j, diagonal):
        s0 = pl.multiple_of(j * ts, ts)
        strict = s_rel < t_rel

        def z_dot(h):
            return lax.dot_general(kb_ref[0, pl.ds(s0, ts), _pair(h)], qbm_ref[h], _NT,
                                   preferred_element_type=F32)

        def terms(z):
            sp = jnp.log(1.0 + jnp.exp(-jnp.abs(z)))
            log_beta = jnp.minimum(z, 0.0) - sp
            log_om = -jnp.maximum(z, 0.0) - sp
            if diagonal:
                log_om = jnp.where(strict, log_om, 0.0)
            hi = log_om.astype(BF16)
            lo = (log_om - hi.astype(F32)).astype(BF16)
            suffix = jnp.dot(u_ref[...], jnp.concatenate([hi, lo], axis=0),
                             preferred_element_type=F32)
            return log_beta, log_om[0:1, :], suffix

        def finish(h, log_beta, om0, suffix):
            rows = slice(h * HEAD_DIM, (h + 1) * HEAD_DIM)
            below = below_ref[h]
            a = jnp.exp(log_beta + suffix + below)
            if diagonal:
                a = jnp.where(strict, a, 0.0)
            acc_ref[rows, :] += jnp.dot(vbt_ref[0, j, rows, :], a.astype(BF16),
                                        preferred_element_type=F32)
            below = below + suffix[0:1, :] + om0
            below_ref[h] = below
            return below

        zs = {0: z_dot(0)}
        if n_heads > 1:
            zs[1] = z_dot(1)
        mid = {0: terms(zs.pop(0))}
        worst = None
        for h in range(n_heads):
            if h + 2 < n_heads:
                zs[h + 2] = z_dot(h + 2)
            if h + 1 < n_heads:
                mid[h + 1] = terms(zs.pop(h + 1))
            below = finish(h, *mid.pop(h))
            worst = below if worst is None else jnp.maximum(worst, below)
        return jnp.max(worst)

    top = block(q_blk, diagonal=True)

    def cond(carry):
        i, top = carry
        return (i <= q_blk) & (top >= EXP_UNDERFLOW)

    def body(carry):
        i, _ = carry
        return i + 1, block(q_blk - i, diagonal=False)

    lax.while_loop(cond, body, (jnp.int32(1), top))

    o_ref[0] = (acc_ref[...].T * gb_ref[0]).astype(BF16)


def _stick(kb, vbt, qb, gb, u2):
    b, l, d_b = kb.shape
    n_heads = d_b // HEAD_DIM
    tq = ATT_TILE
    nq = l // tq
    return pl.pallas_call(
        functools.partial(_stick_kernel, n_heads=n_heads),
        grid=(b, nq),
        in_specs=[
            pl.BlockSpec((1, l, d_b), lambda i, j: (i, 0, 0)),
            pl.BlockSpec((1, nq, d_b, tq), lambda i, j: (i, 0, 0, 0)),
            pl.BlockSpec((1, tq, d_b), lambda i, j: (i, j, 0)),
            pl.BlockSpec((1, tq, d_b), lambda i, j: (i, j, 0)),
            pl.BlockSpec(u2.shape, lambda i, j: (0, 0)),
        ],
        out_specs=pl.BlockSpec((1, tq, d_b), lambda i, j: (i, j, 0)),
        out_shape=jax.ShapeDtypeStruct((b, l, d_b), BF16),
        scratch_shapes=[
            pltpu.VMEM((n_heads, tq, LANES), BF16),
            pltpu.VMEM((n_heads, 1, tq), F32),
            pltpu.VMEM((d_b, tq), F32),
        ],
        compiler_params=_params("arbitrary", "arbitrary"),
        name="stick",
    )(kb, vbt, qb, gb, u2)


def _outproj_kernel(x_ref, oa_ref, ob_ref, wa_ref, wb_ref, y_ref):
    y_ref[...] = (x_ref[...]
                  + jnp.dot(oa_ref[...], wa_ref[...], preferred_element_type=F32)
                  + jnp.dot(ob_ref[...], wb_ref[...], preferred_element_type=F32))


def _outproj(x2, oa, ob, wa, wb):
    n, d = x2.shape
    tm = TOKEN_TILE
    row = lambda width: pl.BlockSpec((tm, width), lambda i: (i, 0))
    return pl.pallas_call(
        _outproj_kernel,
        grid=(n // tm,),
        in_specs=[row(d), row(oa.shape[1]), row(ob.shape[1]),
                  pl.BlockSpec(wa.shape, lambda i: (0, 0)),
                  pl.BlockSpec(wb.shape, lambda i: (0, 0))],
        out_specs=row(d),
        out_shape=jax.ShapeDtypeStruct((n, d), F32),
        compiler_params=_params("arbitrary"),
        name="outproj",
    )(x2, oa, ob, wa, wb)


def kernel(x, norm_gain, w_in, q_norm_gain, k_norm_gain, rel_bias, w_out):
    b, l, d = x.shape
    depth = w_in.shape[0]
    d_a = d // 2
    d_b = d - d_a
    h_a = d_a // HEAD_DIM
    d_qi = IDX_HEADS * IDX_DIM
    topk = min(TOPK_MAX, l // 4)
    ts = ATT_TILE
    scale = HEAD_DIM ** -0.5
    idx_scale = (IDX_HEADS * IDX_DIM) ** -0.5

    gsum = (jnp.arange(d_a)[:, None] // HEAD_DIM == jnp.arange(d_a)[None, :] // HEAD_DIM).astype(BF16)
    col = jnp.arange(2 * ts)[None, :] % ts
    u2 = (col > jnp.arange(ts)[:, None]).astype(BF16)
    tab = _t5_table(rel_bias.astype(F32), ts)

    for layer in range(depth):
        w = w_in[layer]
        o_ki = 4 * d_a + d_qi
        o_wi = o_ki + IDX_DIM
        o_b = o_wi + IDX_HEADS
        w_ki = w[:, o_ki:o_wi]
        w_wi = jnp.pad(w[:, o_wi:o_b] * idx_scale, ((0, 0), (0, LANES - IDX_HEADS)))
        w_b = jnp.concatenate([w[:, o_b:o_b + d_b] * scale, w[:, o_b + d_b:]], axis=1)
        w_all = jnp.concatenate([w[:, :o_ki], w_ki, w_ki, w_wi, w_b], axis=1).astype(BF16)
        qg = jnp.tile(q_norm_gain[layer] * scale, h_a)[None, :]
        kg = jnp.tile(k_norm_gain[layer], h_a)[None, :]

        (qa, ka, vat, ga, qi, ki, wit, qb, kb, vbt, gb) = _inproj(
            x, norm_gain[layer][None, :], w_all, qg, kg, gsum, d_a, d_b, d_qi)
        oa = _dsa(ki, qi, wit, ka, vat, qa, ga, tab, topk)
        ob = _stick(kb, vbt, qb, gb, u2)

        w_o = w_out[layer].astype(BF16)
        x = _outproj(x.reshape(b * l, d), oa.reshape(b * l, d_a), ob.reshape(b * l, d_b),
                     w_o[:d_a], w_o[d_a:]).reshape(b, l, d)
    return x
```

```python
import functools
import math

import jax
import jax.numpy as jnp
from jax import lax
from jax.experimental import pallas as pl
from jax.experimental.pallas import tpu as pltpu

F32 = jnp.float32
BF16 = jnp.bfloat16
I32 = jnp.int32
I16 = jnp.int16

HEAD_DIM = 64
IDX_HEADS = 16
IDX_DIM = 64
TOPK_MAX = 256
NUM_BUCKETS = 32
MAX_DISTANCE = 128
RMS_EPS = 1e-6

LANES = 128
INT_MIN = -(2 ** 31)
HALF16 = 2 ** 15
NEG_BIG = -1e30
EXP_UNDERFLOW = -104.0
LOG2E = math.log2(math.e)

VMEM_LIMIT_BYTES = 56 * 1024 * 1024
TOKEN_TILE = 512
ATT_TILE = 256

_NT = (((1,), (1,)), ((), ()))


def _params(*sem):
    return pltpu.CompilerParams(dimension_semantics=sem, vmem_limit_bytes=VMEM_LIMIT_BYTES)


def _pair(h):
    return slice((h // 2) * LANES, (h // 2 + 1) * LANES)


def _store_head_masked(dst_ref, src, n_heads):
    tq = src.shape[0]
    low = lax.broadcasted_iota(I32, (tq, LANES), 1) < HEAD_DIM
    for h in range(n_heads):
        pair = src[:, _pair(h)].astype(F32)
        keep = low if h % 2 == 0 else jnp.logical_not(low)
        dst_ref[h] = jnp.where(keep, pair, 0.0).astype(BF16)


def _inproj_kernel(x_ref, gain_ref, w_ref, qg_ref, kg_ref, gsum_ref,
                   qa_ref, ka_ref, vat_ref, ga_ref, qi_ref, ki_ref, wit_ref,
                   qb_ref, kb_ref, vbt_ref, gb_ref, *, d_a, d_b, d_qi):
    x = x_ref[0]
    ms = jnp.mean(x * x, axis=-1, keepdims=True)
    h = (x * lax.rsqrt(ms + RMS_EPS) * gain_ref[...]).astype(BF16)
    ts = vat_ref.shape[3]

    def proj(c0, width):
        return jnp.dot(h, w_ref[:, c0:c0 + width], preferred_element_type=F32)

    def head_norm(y, g):
        sq = y * y
        hi = sq.astype(BF16)
        lo = (sq - hi.astype(F32)).astype(BF16)
        ssum = (jnp.dot(hi, gsum_ref[...], preferred_element_type=F32)
                + jnp.dot(lo, gsum_ref[...], preferred_element_type=F32))
        return y * lax.rsqrt(ssum * (1.0 / HEAD_DIM) + RMS_EPS) * g

    def silu(g):
        return g * (1.0 / (1.0 + jnp.exp(-g)))

    def store_key_blocks_t(dst_ref, v):
        vt = v.T.astype(BF16)
        for c in range(dst_ref.shape[1]):
            dst_ref[0, c] = vt[:, c * ts:(c + 1) * ts]

    c = 0
    qa_ref[0] = head_norm(proj(c, d_a), qg_ref[...]).astype(BF16); c += d_a
    ka_ref[0] = head_norm(proj(c, d_a), kg_ref[...]).astype(BF16); c += d_a
    store_key_blocks_t(vat_ref, proj(c, d_a)); c += d_a
    ga_ref[0] = silu(proj(c, d_a)); c += d_a
    qi_ref[0] = proj(c, d_qi).astype(BF16); c += d_qi
    ki_ref[0] = proj(c, LANES).astype(BF16); c += LANES
    wit_ref[0] = proj(c, LANES).T[:IDX_HEADS, :]; c += LANES
    qb_ref[0] = proj(c, d_b).astype(BF16); c += d_b
    kb_ref[0] = proj(c, d_b).astype(BF16); c += d_b
    store_key_blocks_t(vbt_ref, proj(c, d_b)); c += d_b
    gb_ref[0] = silu(proj(c, d_b))


def _inproj(x, gain, w_all, qg, kg, gsum, d_a, d_b, d_qi):
    b, l, d = x.shape
    tm = TOKEN_TILE
    ts = ATT_TILE
    nb = l // ts
    row = lambda width: pl.BlockSpec((1, tm, width), lambda i, j: (i, j, 0))
    const = lambda shape: pl.BlockSpec(shape, lambda i, j: (0, 0))
    vt_spec = lambda ch: pl.BlockSpec((1, tm // ts, ch, ts), lambda i, j: (i, j, 0, 0))
    tok = lambda width, dt: jax.ShapeDtypeStruct((b, l, width), dt)
    out_shapes = [
        tok(d_a, BF16), tok(d_a, BF16), jax.ShapeDtypeStruct((b, nb, d_a, ts), BF16), tok(d_a, F32),
        tok(d_qi, BF16), tok(LANES, BF16), jax.ShapeDtypeStruct((b, IDX_HEADS, l), F32),
        tok(d_b, BF16), tok(d_b, BF16), jax.ShapeDtypeStruct((b, nb, d_b, ts), BF16), tok(d_b, F32),
    ]
    out_specs = [row(d_a), row(d_a), vt_spec(d_a), row(d_a), row(d_qi), row(LANES),
                 pl.BlockSpec((1, IDX_HEADS, tm), lambda i, j: (i, 0, j)),
                 row(d_b), row(d_b), vt_spec(d_b), row(d_b)]
    return pl.pallas_call(
        functools.partial(_inproj_kernel, d_a=d_a, d_b=d_b, d_qi=d_qi),
        grid=(b, l // tm),
        in_specs=[row(d), const(gain.shape),
                  pl.BlockSpec(w_all.shape, lambda i, j: (0, 0), pipeline_mode=pl.Buffered(1)),
                  const(qg.shape), const(kg.shape), const(gsum.shape)],
        out_specs=out_specs,
        out_shape=out_shapes,
        compiler_params=_params("arbitrary", "arbitrary"),
        name="inproj",
    )(x, gain, w_all, qg, kg, gsum)


def _t5_bucket(dist):
    max_exact = NUM_BUCKETS // 2
    d = jnp.maximum(dist, 0)
    d_f = jnp.maximum(d, 1).astype(F32)
    large = max_exact + (jnp.log(d_f / max_exact) / math.log(MAX_DISTANCE / max_exact)
                         * (NUM_BUCKETS - max_exact)).astype(I32)
    large = jnp.minimum(large, NUM_BUCKETS - 1)
    return jnp.where(d < max_exact, d, large)


def _t5_table_kernel(rb_ref, tab_ref, *, n_heads):
    ts = tab_ref.shape[2]
    s_rel = lax.broadcasted_iota(I32, (ts, ts), 0)
    t_rel = lax.broadcasted_iota(I32, (ts, ts), 1)
    for i in range(3):
        bucket = _t5_bucket(t_rel - s_rel + i * ts)
        for h in range(n_heads):
            tab_ref[i, h] = jnp.zeros((ts, ts), F32)

        def body(j, _):
            hit = bucket == j
            for h in range(n_heads):
                tab_ref[i, h] = jnp.where(hit, rb_ref[j, h], tab_ref[i, h])
            return 0

        lax.fori_loop(0, NUM_BUCKETS, body, 0)


def _t5_table(rel_bias, ts):
    n_heads = rel_bias.shape[1]
    return pl.pallas_call(
        functools.partial(_t5_table_kernel, n_heads=n_heads),
        in_specs=[pl.BlockSpec(memory_space=pltpu.SMEM)],
        out_specs=pl.BlockSpec(memory_space=pltpu.VMEM),
        out_shape=jax.ShapeDtypeStruct((3, n_heads, ts, ts), F32),
        compiler_params=pltpu.CompilerParams(vmem_limit_bytes=VMEM_LIMIT_BYTES),
        name="t5_table",
    )(rel_bias)


def _dsa_kernel(ki_ref, qi_ref, wit_ref, ka_ref, vat_ref, qa_ref, ga_ref, tab_ref, o_ref,
                keys_ref, khi_ref, klo_ref, madd_ref, qim_ref, qam_ref, m_ref, l_ref, acc_ref,
                lg_ref, bmax_ref, *, topk, n_heads):
    tq = o_ref.shape[1]
    ts = tq
    q_blk = pl.program_id(1)
    nblk = q_blk + 1
    s_rel = lax.broadcasted_iota(I32, (ts, tq), 0)
    t_idx = q_blk * tq + lax.broadcasted_iota(I32, (ts, tq), 1)

    def block_start(j):
        return pl.multiple_of(j * ts, ts)

    _store_head_masked(qim_ref, qi_ref[0], IDX_HEADS)
    _store_head_masked(qam_ref, qa_ref[0], n_heads)

    def idx_body(j, _):
        s0 = block_start(j)
        kblk = ki_ref[0, pl.ds(s0, ts), :]
        score = jnp.zeros((ts, tq), F32)
        for h in range(IDX_HEADS):
            dots = lax.dot_general(kblk, qim_ref[h], _NT, preferred_element_type=F32)
            score = score + jnp.maximum(dots, 0.0) * wit_ref[0, h:h + 1, :]
        bits = lax.bitcast_convert_type(score, I32)
        key = bits ^ ((bits >> 31) & 0x7FFFFFFF)
        key = jnp.where(s0 + s_rel > t_idx, INT_MIN, key)
        keys_ref[pl.ds(s0, ts), :] = key
        khi_ref[pl.ds(s0, ts), :] = (key >> 16).astype(I16)
        klo_ref[pl.ds(s0, ts), :] = ((key & 0xFFFF) - HALF16).astype(I16)
        return 0

    lax.fori_loop(0, nblk, idx_body, 0)

    def count(pred_fn):
        def body(j, c):
            s0 = block_start(j)
            hit = pred_fn(keys_ref[pl.ds(s0, ts), :], s0).astype(I32)
            return c + hit.reshape(ts // 8, 8, tq).sum(axis=0)
        c8 = lax.fori_loop(0, nblk, body, jnp.zeros((8, tq), I32))
        return jnp.sum(c8, axis=0, keepdims=True)

    def count16(ref, pred_fn):
        rows16 = 16
        def body(j, c):
            hit = jnp.where(pred_fn(ref[pl.ds(block_start(j), ts), :]), jnp.int16(1), jnp.int16(0))
            hit = hit.reshape(ts // rows16, rows16, tq)
            for r in range(ts // rows16):
                c = c + hit[r]
            return c
        c16 = lax.fori_loop(0, nblk, body, jnp.zeros((rows16, tq), I16))
        return jnp.sum(c16.astype(I32), axis=0, keepdims=True)

    def radix16(ref, need, cnt_floor):
        def bit_body(i, carry):
            thr, cnt_thr = carry
            cand = thr + jnp.left_shift(jnp.int32(1), 15 - i)
            cand16 = cand.astype(I16)
            cnt = count16(ref, lambda blk: blk >= cand16)
            take = cnt >= need
            return jnp.where(take, cand, thr), jnp.where(take, cnt, cnt_thr)
        return lax.fori_loop(0, 16, bit_body, (jnp.full((1, tq), -HALF16, I32), cnt_floor))

    total = jnp.zeros((1, tq), I32) + nblk * ts
    thr_hi, cnt_hi = radix16(khi_ref, topk, total)
    thr_hi16 = thr_hi.astype(I16)
    cnt_gt = count16(khi_ref, lambda blk: blk > thr_hi16)

    def bucket_body(j, _):
        s0 = block_start(j)
        klo_ref[pl.ds(s0, ts), :] = jnp.where(khi_ref[pl.ds(s0, ts), :] == thr_hi16,
                                              klo_ref[pl.ds(s0, ts), :], jnp.int16(-HALF16))
        return 0

    lax.fori_loop(0, nblk, bucket_body, 0)
    thr_lo, cnt_lo = radix16(klo_ref, topk - cnt_gt, cnt_hi - cnt_gt)
    thr = thr_hi * (2 * HALF16) + (thr_lo + HALF16)
    cnt_thr = cnt_gt + cnt_lo

    tie = jnp.max(jnp.where((cnt_thr != topk) & (thr > INT_MIN), 1, 0)) > 0

    @pl.when(jnp.logical_not(tie))
    def _():
        floor = jnp.maximum(thr, INT_MIN + 1)

        def body(j, _):
            s0 = block_start(j)
            madd_ref[pl.ds(s0, ts), :] = jnp.where(keys_ref[pl.ds(s0, ts), :] >= floor, 0.0, NEG_BIG)
            return 0

        lax.fori_loop(0, nblk, body, 0)

    @pl.when(tie)
    def _():
        need = topk - count(lambda blk, s0: blk > thr)
        n_bits = keys_ref.shape[0].bit_length()

        def jbody(i, bound):
            cand = bound + jnp.left_shift(jnp.int32(1), n_bits - 1 - i)
            cnt = count(lambda blk, s0: (blk == thr) & (s0 + s_rel < cand))
            return jnp.where(cnt <= need, cand, bound)

        bound = lax.fori_loop(0, n_bits, jbody, jnp.zeros((1, tq), I32))

        def body(j, _):
            s0 = block_start(j)
            blk = keys_ref[pl.ds(s0, ts), :]
            sel = ((blk > thr) | ((blk == thr) & (s0 + s_rel < bound))) & (blk > INT_MIN)
            madd_ref[pl.ds(s0, ts), :] = jnp.where(sel, 0.0, NEG_BIG)
            return 0

        lax.fori_loop(0, nblk, body, 0)

    m_ref[...] = jnp.full(m_ref.shape, NEG_BIG, F32)
    l_ref[...] = jnp.zeros(l_ref.shape, F32)
    acc_ref[...] = jnp.zeros(acc_ref.shape, F32)

    def att_body(j, _):
        s0 = block_start(j)
        madd = madd_ref[pl.ds(s0, ts), :]
        near = jnp.minimum(q_blk - j, 2)
        for h in range(n_heads):
            lg = lax.dot_general(ka_ref[0, pl.ds(s0, ts), _pair(h)], qam_ref[h], _NT,
                                 preferred_element_type=F32)
            lg = (lg + tab_ref[near, h] + madd) * LOG2E
            lg_ref[h] = lg
            bmax_ref[h] = jnp.max(lg, axis=0, keepdims=True)
        for h in range(n_heads):
            rows = slice(h * HEAD_DIM, (h + 1) * HEAD_DIM)
            m_old = m_ref[h]
            m_new = jnp.maximum(m_old, bmax_ref[h])
            alpha = jnp.exp2(m_old - m_new)
            p = jnp.exp2(lg_ref[h] - m_new)
            l_ref[h] = alpha * l_ref[h] + jnp.sum(p, axis=0, keepdims=True)
            pv = jnp.dot(vat_ref[0, j, rows, :], p.astype(BF16), preferred_element_type=F32)
            acc_ref[rows, :] = alpha * acc_ref[rows, :] + pv
            m_ref[h] = m_new
        return 0

    lax.fori_loop(0, nblk, att_body, 0)

    for h in range(n_heads):
        rows = slice(h * HEAD_DIM, (h + 1) * HEAD_DIM)
        acc_ref[rows, :] = acc_ref[rows, :] * (1.0 / l_ref[h])
    o_ref[0] = (acc_ref[...].T * ga_ref[0]).astype(BF16)


def _dsa(ki, qi, wit, ka, vat, qa, ga, tab, topk):
    b, l, d_a = ka.shape
    n_heads = d_a // HEAD_DIM
    d_qi = qi.shape[2]
    tq = ATT_TILE
    nq = l // tq
    kernel = functools.partial(_dsa_kernel, topk=topk, n_heads=n_heads)
    return pl.pallas_call(
        kernel,
        grid=(b, nq),
        in_specs=[
            pl.BlockSpec((1, l, LANES), lambda i, j: (i, 0, 0)),
            pl.BlockSpec((1, tq, d_qi), lambda i, j: (i, j, 0)),
            pl.BlockSpec((1, IDX_HEADS, tq), lambda i, j: (i, 0, j)),
            pl.BlockSpec((1, l, d_a), lambda i, j: (i, 0, 0)),
            pl.BlockSpec((1, nq, d_a, tq), lambda i, j: (i, 0, 0, 0)),
            pl.BlockSpec((1, tq, d_a), lambda i, j: (i, j, 0)),
            pl.BlockSpec((1, tq, d_a), lambda i, j: (i, j, 0)),
            pl.BlockSpec(tab.shape, lambda i, j: (0, 0, 0, 0), pipeline_mode=pl.Buffered(1)),
        ],
        out_specs=pl.BlockSpec((1, tq, d_a), lambda i, j: (i, j, 0)),
        out_shape=jax.ShapeDtypeStruct((b, l, d_a), BF16),
        scratch_shapes=[
            pltpu.VMEM((l, tq), I32),
            pltpu.VMEM((l, tq), I16),
            pltpu.VMEM((l, tq), I16),
            pltpu.VMEM((l, tq), F32),
            pltpu.VMEM((IDX_HEADS, tq, LANES), BF16),
            pltpu.VMEM((n_heads, tq, LANES), BF16),
            pltpu.VMEM((n_heads, 1, tq), F32),
            pltpu.VMEM((n_heads, 1, tq), F32),
            pltpu.VMEM((d_a, tq), F32),
            pltpu.VMEM((n_heads, tq, tq), F32),
            pltpu.VMEM((n_heads, 1, tq), F32),
        ],
        compiler_params=_params("arbitrary", "arbitrary"),
        name="dsa",
    )(ki, qi, wit, ka, vat, qa, ga, tab)


def _stick_kernel(kb_ref, vbt_ref, qb_ref, gb_ref, u_ref, o_ref,
                  qbm_ref, below_ref, acc_ref, *, n_heads):
    tq = o_ref.shape[1]
    ts = tq
    q_blk = pl.program_id(1)
    s_rel = lax.broadcasted_iota(I32, (ts, tq), 0)
    t_rel = lax.broadcasted_iota(I32, (ts, tq), 1)

    _store_head_masked(qbm_ref, qb_ref[0], n_heads)
    below_ref[...] = jnp.zeros(below_ref.shape, F32)
    acc_ref[...] = jnp.zeros(acc_ref.shape, F32)

    def block(j, diagonal):
        s0 = pl.multiple_of(j * ts, ts)
        strict = s_rel < t_rel

        def z_dot(h):
            return lax.dot_general(kb_ref[0, pl.ds(s0, ts), _pair(h)], qbm_ref[h], _NT,
                                   preferred_element_type=F32)

        def terms(z):
            sp = jnp.log(1.0 + jnp.exp(-jnp.abs(z)))
            log_beta = jnp.minimum(z, 0.0) - sp
            log_om = -jnp.maximum(z, 0.0) - sp
            if diagonal:
                log_om = jnp.where(strict, log_om, 0.0)
            hi = log_om.astype(BF16)
            lo = (log_om - hi.astype(F32)).astype(BF16)
            suffix = jnp.dot(u_ref[...], jnp.concatenate([hi, lo], axis=0),
                             preferred_element_type=F32)
            return log_beta, log_om[0:1, :], suffix

        def finish(h, log_beta, om0, suffix):
            rows = slice(h * HEAD_DIM, (h + 1) * HEAD_DIM)
            below = below_ref[h]
            a = jnp.exp(log_beta + suffix + below)
            if diagonal:
                a = jnp.where(strict, a, 0.0)
            acc_ref[rows, :] += jnp.dot(vbt_ref[0, j, rows, :], a.astype(BF16),
                                        preferred_element_type=F32)
            below = below + suffix[0:1, :] + om0
            below_ref[h] = below
            return below

        zs = {0: z_dot(0)}
        if n_heads > 1:
            zs[1] = z_dot(1)
        mid = {0: terms(zs.pop(0))}
        worst = None
        for h in range(n_heads):
            if h + 2 < n_heads:
                zs[h + 2] = z_dot(h + 2)
            if h + 1 < n_heads:
                mid[h + 1] = terms(zs.pop(h + 1))
            below = finish(h, *mid.pop(h))
            worst = below if worst is None else jnp.maximum(worst, below)
        return jnp.max(worst)

    top = block(q_blk, diagonal=True)

    def cond(carry):
        i, top = carry
        return (i <= q_blk) & (top >= EXP_UNDERFLOW)

    def body(carry):
        i, _ = carry
        return i + 1, block(q_blk - i, diagonal=False)

    lax.while_loop(cond, body, (jnp.int32(1), top))

    o_ref[0] = (acc_ref[...].T * gb_ref[0]).astype(BF16)


def _stick(kb, vbt, qb, gb, u2):
    b, l, d_b = kb.shape
    n_heads = d_b // HEAD_DIM
    tq = ATT_TILE
    nq = l // tq
    return pl.pallas_call(
        functools.partial(_stick_kernel, n_heads=n_heads),
        grid=(b, nq),
        in_specs=[
            pl.BlockSpec((1, l, d_b), lambda i, j: (i, 0, 0)),
            pl.BlockSpec((1, nq, d_b, tq), lambda i, j: (i, 0, 0, 0)),
            pl.BlockSpec((1, tq, d_b), lambda i, j: (i, j, 0)),
            pl.BlockSpec((1, tq, d_b), lambda i, j: (i, j, 0)),
            pl.BlockSpec(u2.shape, lambda i, j: (0, 0)),
        ],
        out_specs=pl.BlockSpec((1, tq, d_b), lambda i, j: (i, j, 0)),
        out_shape=jax.ShapeDtypeStruct((b, l, d_b), BF16),
        scratch_shapes=[
            pltpu.VMEM((n_heads, tq, LANES), BF16),
            pltpu.VMEM((n_heads, 1, tq), F32),
            pltpu.VMEM((d_b, tq), F32),
        ],
        compiler_params=_params("arbitrary", "arbitrary"),
        name="stick",
    )(kb, vbt, qb, gb, u2)


def _outproj_kernel(x_ref, oa_ref, ob_ref, wa_ref, wb_ref, y_ref):
    y_ref[...] = (x_ref[...]
                  + jnp.dot(oa_ref[...], wa_ref[...], preferred_element_type=F32)
                  + jnp.dot(ob_ref[...], wb_ref[...], preferred_element_type=F32))


def _outproj(x2, oa, ob, wa, wb):
    n, d = x2.shape
    tm = TOKEN_TILE
    row = lambda width: pl.BlockSpec((tm, width), lambda i: (i, 0))
    return pl.pallas_call(
        _outproj_kernel,
        grid=(n // tm,),
        in_specs=[row(d), row(oa.shape[1]), row(ob.shape[1]),
                  pl.BlockSpec(wa.shape, lambda i: (0, 0)),
                  pl.BlockSpec(wb.shape, lambda i: (0, 0))],
        out_specs=row(d),
        out_shape=jax.ShapeDtypeStruct((n, d), F32),
        compiler_params=_params("arbitrary"),
        name="outproj",
    )(x2, oa, ob, wa, wb)


def kernel(x, norm_gain, w_in, q_norm_gain, k_norm_gain, rel_bias, w_out):
    b, l, d = x.shape
    depth = w_in.shape[0]
    d_a = d // 2
    d_b = d - d_a
    h_a = d_a // HEAD_DIM
    d_qi = IDX_HEADS * IDX_DIM
    topk = min(TOPK_MAX, l // 4)
    ts = ATT_TILE
    scale = HEAD_DIM ** -0.5
    idx_scale = (IDX_HEADS * IDX_DIM) ** -0.5

    gsum = (jnp.arange(d_a)[:, None] // HEAD_DIM == jnp.arange(d_a)[None, :] // HEAD_DIM).astype(BF16)
    col = jnp.arange(2 * ts)[None, :] % ts
    u2 = (col > jnp.arange(ts)[:, None]).astype(BF16)
    tab = _t5_table(rel_bias.astype(F32), ts)

    for layer in range(depth):
        w = w_in[layer]
        o_ki = 4 * d_a + d_qi
        o_wi = o_ki + IDX_DIM
        o_b = o_wi + IDX_HEADS
        w_ki = w[:, o_ki:o_wi]
        w_wi = jnp.pad(w[:, o_wi:o_b] * idx_scale, ((0, 0), (0, LANES - IDX_HEADS)))
        w_b = jnp.concatenate([w[:, o_b:o_b + d_b] * scale, w[:, o_b + d_b:]], axis=1)
        w_all = jnp.concatenate([w[:, :o_ki], w_ki, w_ki, w_wi, w_b], axis=1).astype(BF16)
        qg = jnp.tile(q_norm_gain[layer] * scale, h_a)[None, :]
        kg = jnp.tile(k_norm_gain[layer], h_a)[None, :]

        (qa, ka, vat, ga, qi, ki, wit, qb, kb, vbt, gb) = _inproj(
            x, norm_gain[layer][None, :], w_all, qg, kg, gsum, d_a, d_b, d_qi)
        oa = _dsa(ki, qi, wit, ka, vat, qa, ga, tab, topk)
        ob = _stick(kb, vbt, qb, gb, u2)

        w_o = w_out[layer].astype(BF16)
        x = _outproj(x.reshape(b * l, d), oa.reshape(b * l, d_a), ob.reshape(b * l, d_b),
                     w_o[:d_a], w_o[d_a:]).reshape(b, l, d)
    return x
```

```python
import functools
import math

import jax
import jax.numpy as jnp
import numpy as np
from jax import lax
from jax.experimental import pallas as pl
from jax.experimental.pallas import tpu as pltpu

F32 = jnp.float32
BF16 = jnp.bfloat16
I32 = jnp.int32

HEAD_DIM = 64
IDX_HEADS = 16
IDX_DIM = 64
TOPK_MAX = 256
NUM_BUCKETS = 32
MAX_DISTANCE = 128
RMS_EPS = 1e-6

LANES = 128
NEG_BIG = -1e30
EXP2_UNDERFLOW = -150.0
LOG2E = math.log2(math.e)
BISECT_CAP = 300
BISECT_GROUP = 4

VMEM_LIMIT_BYTES = 56 * 1024 * 1024
TOKEN_TILE = 512
ATT_TILE = 256

_NT = (((1,), (1,)), ((), ()))


def _params(*sem):
    return pltpu.CompilerParams(dimension_semantics=sem, vmem_limit_bytes=VMEM_LIMIT_BYTES)


def _pair(h):
    return slice((h // 2) * LANES, (h // 2 + 1) * LANES)


def _store_head_masked(dst_ref, src, n_heads):
    tq = src.shape[0]
    low = lax.broadcasted_iota(I32, (tq, LANES), 1) < HEAD_DIM
    for h in range(n_heads):
        pair = src[:, _pair(h)].astype(F32)
        keep = low if h % 2 == 0 else jnp.logical_not(low)
        dst_ref[h] = jnp.where(keep, pair, 0.0).astype(BF16)


def _inproj_kernel(x_ref, gain_ref, w_ref, qg_ref, kg_ref, gsum_ref,
                   qa_ref, ka_ref, vat_ref, ga_ref, qi_ref, ki_ref, wit_ref,
                   qb_ref, kb_ref, vbt_ref, gb_ref, *, d_a, d_b, d_qi):
    x = x_ref[0]
    ms = jnp.mean(x * x, axis=-1, keepdims=True)
    h = (x * lax.rsqrt(ms + RMS_EPS) * gain_ref[...]).astype(BF16)
    ts = vat_ref.shape[3]

    def proj(c0, width):
        return jnp.dot(h, w_ref[:, c0:c0 + width], preferred_element_type=F32)

    def head_norm(y, g):
        sq = y * y
        hi = sq.astype(BF16)
        lo = (sq - hi.astype(F32)).astype(BF16)
        ones = gsum_ref[...]
        ssum = jnp.concatenate(
            [jnp.dot(hi[:, c:c + LANES], ones, preferred_element_type=F32)
             + jnp.dot(lo[:, c:c + LANES], ones, preferred_element_type=F32)
             for c in range(0, y.shape[1], LANES)], axis=1)
        return y * lax.rsqrt(ssum * (1.0 / HEAD_DIM) + RMS_EPS) * g

    def silu(g):
        return g * (1.0 / (1.0 + jnp.exp(-g)))

    def store_key_blocks_t(dst_ref, v):
        vt = v.T.astype(BF16)
        for c in range(dst_ref.shape[1]):
            dst_ref[0, c] = vt[:, c * ts:(c + 1) * ts]

    c = 0
    qa_ref[0] = head_norm(proj(c, d_a), qg_ref[...]).astype(BF16); c += d_a
    ka_ref[0] = head_norm(proj(c, d_a), kg_ref[...]).astype(BF16); c += d_a
    store_key_blocks_t(vat_ref, proj(c, d_a)); c += d_a
    ga_ref[0] = silu(proj(c, d_a)); c += d_a
    qi_ref[0] = proj(c, d_qi).astype(BF16); c += d_qi
    ki_ref[0] = proj(c, LANES).astype(BF16); c += LANES
    wit_ref[0] = proj(c, LANES).T[:IDX_HEADS, :]; c += LANES
    qb_ref[0] = proj(c, d_b).astype(BF16); c += d_b
    kb_ref[0] = proj(c, d_b).astype(BF16); c += d_b
    store_key_blocks_t(vbt_ref, proj(c, d_b)); c += d_b
    gb_ref[0] = silu(proj(c, d_b))


def _inproj(x, gain, w_all, qg, kg, gsum, d_a, d_b, d_qi):
    b, l, d = x.shape
    tm = TOKEN_TILE
    ts = ATT_TILE
    nb = l // ts
    row = lambda width: pl.BlockSpec((1, tm, width), lambda i, j: (i, j, 0))
    const = lambda shape: pl.BlockSpec(shape, lambda i, j: (0, 0))
    vt_spec = lambda ch: pl.BlockSpec((1, tm // ts, ch, ts), lambda i, j: (i, j, 0, 0))
    tok = lambda width, dt: jax.ShapeDtypeStruct((b, l, width), dt)
    out_shapes = [
        tok(d_a, BF16), tok(d_a, BF16), jax.ShapeDtypeStruct((b, nb, d_a, ts), BF16), tok(d_a, F32),
        tok(d_qi, BF16), tok(LANES, BF16), jax.ShapeDtypeStruct((b, IDX_HEADS, l), F32),
        tok(d_b, BF16), tok(d_b, BF16), jax.ShapeDtypeStruct((b, nb, d_b, ts), BF16), tok(d_b, F32),
    ]
    out_specs = [row(d_a), row(d_a), vt_spec(d_a), row(d_a), row(d_qi), row(LANES),
                 pl.BlockSpec((1, IDX_HEADS, tm), lambda i, j: (i, 0, j)),
                 row(d_b), row(d_b), vt_spec(d_b), row(d_b)]
    return pl.pallas_call(
        functools.partial(_inproj_kernel, d_a=d_a, d_b=d_b, d_qi=d_qi),
        grid=(b, l // tm),
        in_specs=[row(d), const(gain.shape),
                  pl.BlockSpec(w_all.shape, lambda i, j: (0, 0), pipeline_mode=pl.Buffered(1)),
                  const(qg.shape), const(kg.shape), const(gsum.shape)],
        out_specs=out_specs,
        out_shape=out_shapes,
        compiler_params=_params("arbitrary", "arbitrary"),
        name="inproj",
    )(x, gain, w_all, qg, kg, gsum)


def _t5_large_thresholds():
    max_exact = NUM_BUCKETS // 2
    d = np.arange(max_exact, 2 * MAX_DISTANCE + 1)
    large = max_exact + (np.log(d.astype(np.float32) / np.float32(max_exact))
                         / np.float32(math.log(MAX_DISTANCE / max_exact))
                         * np.float32(NUM_BUCKETS - max_exact)).astype(np.int32)
    large = np.minimum(large, NUM_BUCKETS - 1)
    return [int(d[np.argmax(large >= k)]) for k in range(max_exact + 1, NUM_BUCKETS)]


def _t5_bucket(dist):
    max_exact = NUM_BUCKETS // 2
    d = jnp.maximum(dist, 0)
    large = jnp.full(d.shape, max_exact, I32)
    for first in _t5_large_thresholds():
        large = large + (d >= first).astype(I32)
    return jnp.where(d < max_exact, d, large)


def _t5_table_kernel(rb_ref, tab_ref, *, n_heads):
    ts = tab_ref.shape[2]
    s_rel = lax.broadcasted_iota(I32, (ts, ts), 0)
    t_rel = lax.broadcasted_iota(I32, (ts, ts), 1)
    for i in range(3):
        bucket = _t5_bucket(t_rel - s_rel + i * ts)
        for h in range(n_heads):
            tab_ref[i, h] = jnp.zeros((ts, ts), F32)

        def body(j, _):
            hit = bucket == j
            for h in range(n_heads):
                tab_ref[i, h] = jnp.where(hit, rb_ref[j, h], tab_ref[i, h])
            return 0

        lax.fori_loop(0, NUM_BUCKETS, body, 0)


def _t5_table(rel_bias, ts):
    n_heads = rel_bias.shape[1]
    return pl.pallas_call(
        functools.partial(_t5_table_kernel, n_heads=n_heads),
        in_specs=[pl.BlockSpec(memory_space=pltpu.SMEM)],
        out_specs=pl.BlockSpec(memory_space=pltpu.VMEM),
        out_shape=jax.ShapeDtypeStruct((3, n_heads, ts, ts), F32),
        compiler_params=pltpu.CompilerParams(vmem_limit_bytes=VMEM_LIMIT_BYTES),
        name="t5_table",
    )(rel_bias)


def _dsa_kernel(ki_ref, qi_ref, wit_ref, ka_ref, vat_ref, qa_ref, ga_ref, tab_ref, o_ref,
                sc_ref, madd_ref, qim_ref, qam_ref, m_ref, l_ref, acc_ref, lg_ref, bmax_ref,
                *, topk, n_heads):
    tq = o_ref.shape[1]
    ts = tq
    q_blk = pl.program_id(1)
    nblk = q_blk + 1
    s_rel = lax.broadcasted_iota(I32, (ts, tq), 0)
    t_idx = q_blk * tq + lax.broadcasted_iota(I32, (ts, tq), 1)

    def block_start(j):
        return pl.multiple_of(j * ts, ts)

    _store_head_masked(qim_ref, qi_ref[0], IDX_HEADS)
    _store_head_masked(qam_ref, qa_ref[0], n_heads)

    all_selected = nblk * tq <= topk
    n_sel = jnp.where(all_selected, 0, nblk)

    @pl.when(all_selected)
    def _():
        def body(j, _):
            s0 = block_start(j)
            madd_ref[pl.ds(s0, ts), :] = jnp.where(s0 + s_rel > t_idx, NEG_BIG, 0.0)
            return 0

        lax.fori_loop(0, nblk, body, 0)

    def idx_body(j, carry):
        vmin, vmax = carry
        s0 = block_start(j)
        kblk = ki_ref[0, pl.ds(s0, ts), :]
        score = jnp.zeros((ts, tq), F32)
        for h in range(IDX_HEADS):
            dots = lax.dot_general(kblk, qim_ref[h], _NT, preferred_element_type=F32)
            score = score + jnp.maximum(dots, 0.0) * wit_ref[0, h:h + 1, :]
        future = s0 + s_rel > t_idx
        sc_ref[pl.ds(s0, ts), :] = jnp.where(future, jnp.nan, score)
        lo8 = jnp.where(future, jnp.inf, score).reshape(ts // 8, 8, tq).min(axis=0)
        hi8 = jnp.where(future, -jnp.inf, score).reshape(ts // 8, 8, tq).max(axis=0)
        return jnp.minimum(vmin, lo8), jnp.maximum(vmax, hi8)

    vmin, vmax = lax.fori_loop(0, n_sel, idx_body,
                               (jnp.full((8, tq), jnp.inf, F32), jnp.full((8, tq), -jnp.inf, F32)))
    vmin = jnp.min(vmin, axis=0, keepdims=True)
    vmax = jnp.max(vmax, axis=0, keepdims=True)

    def count(pred_fn):
        def body(j, c):
            s0 = block_start(j)
            hit = pred_fn(sc_ref[pl.ds(s0, ts), :], s0).reshape(ts // 8, 8, tq)
            c = list(c)
            for r in range(ts // 8):
                c[r % 4] = jnp.where(hit[r], c[r % 4] + 1, c[r % 4])
            return tuple(c)
        z = jnp.zeros((8, tq), I32)
        c8 = lax.fori_loop(0, n_sel, body, (z, z, z, z))
        return jnp.sum(c8[0] + c8[1] + c8[2] + c8[3], axis=0, keepdims=True)

    def midpoint(lo, hi):
        return 0.5 * lo + 0.5 * hi

    def searching(lo, hi, cnt_lo):
        mid = midpoint(lo, hi)
        return (cnt_lo > topk) & (mid > lo) & (mid < hi)

    def any_lane(flag):
        return jnp.max(flag.astype(I32)) > 0

    n_causal = t_idx[0:1, :] + 1
    lo0 = jnp.where(n_causal > topk, vmin, -jnp.inf)
    hi0 = vmax + (jnp.abs(vmax) * 2.0 ** -20 + 1e-30)

    def bisect_cond(carry):
        it, _, _, _, live = carry
        return live & (it < BISECT_CAP)

    def bisect_step(_, carry):
        lo, hi, cnt_lo = carry
        go = searching(lo, hi, cnt_lo)
        mid = midpoint(lo, hi)
        cnt = count(lambda blk, s0: blk >= mid)
        take = go & (cnt >= topk)
        return (jnp.where(take, mid, lo), jnp.where(go & (cnt < topk), mid, hi),
                jnp.where(take, cnt, cnt_lo))

    def bisect_body(carry):
        it, lo, hi, cnt_lo, _ = carry
        lo, hi, cnt_lo = lax.fori_loop(0, BISECT_GROUP, bisect_step, (lo, hi, cnt_lo))
        return it + BISECT_GROUP, lo, hi, cnt_lo, any_lane(searching(lo, hi, cnt_lo))

    _, thr, _, cnt_thr, _ = lax.while_loop(
        bisect_cond, bisect_body,
        (jnp.int32(0), lo0, hi0, n_causal, any_lane(searching(lo0, hi0, n_causal))))

    tie = any_lane(cnt_thr > topk)

    @pl.when(jnp.logical_not(tie))
    def _():
        def body(j, _):
            s0 = block_start(j)
            madd_ref[pl.ds(s0, ts), :] = jnp.where(sc_ref[pl.ds(s0, ts), :] >= thr, 0.0, NEG_BIG)
            return 0

        lax.fori_loop(0, n_sel, body, 0)

    @pl.when(tie)
    def _():
        need = topk - count(lambda blk, s0: blk > thr)
        n_bits = sc_ref.shape[0].bit_length()

        def jbody(i, bound):
            cand = bound + jnp.left_shift(jnp.int32(1), n_bits - 1 - i)
            cnt = count(lambda blk, s0: (blk == thr) & (s0 + s_rel < cand))
            return jnp.where(cnt <= need, cand, bound)

        bound = lax.fori_loop(0, n_bits, jbody, jnp.zeros((1, tq), I32))

        def body(j, _):
            s0 = block_start(j)
            blk = sc_ref[pl.ds(s0, ts), :]
            sel = (blk > thr) | ((blk == thr) & (s0 + s_rel < bound))
            madd_ref[pl.ds(s0, ts), :] = jnp.where(sel, 0.0, NEG_BIG)
            return 0

        lax.fori_loop(0, n_sel, body, 0)

    m_ref[...] = jnp.full(m_ref.shape, NEG_BIG, F32)
    l_ref[...] = jnp.zeros(l_ref.shape, F32)
    acc_ref[...] = jnp.zeros(acc_ref.shape, F32)

    def att_body(j, _):
        s0 = block_start(j)
        madd = madd_ref[pl.ds(s0, ts), :]
        near = jnp.minimum(q_blk - j, 2)
        for h in range(n_heads):
            lg = lax.dot_general(ka_ref[0, pl.ds(s0, ts), _pair(h)], qam_ref[h], _NT,
                                 preferred_element_type=F32)
            lg = (lg + tab_ref[near, h] + madd) * LOG2E
            lg_ref[h] = lg
            bmax_ref[h] = jnp.max(lg, axis=0, keepdims=True)
        for h in range(n_heads):
            rows = slice(h * HEAD_DIM, (h + 1) * HEAD_DIM)
            m_old = m_ref[h]
            m_new = jnp.maximum(m_old, bmax_ref[h])
            alpha = jnp.exp2(m_old - m_new)
            p = jnp.exp2(lg_ref[h] - m_new)
            l_ref[h] = alpha * l_ref[h] + jnp.sum(p, axis=0, keepdims=True)
            pv = jnp.dot(vat_ref[0, j, rows, :], p.astype(BF16), preferred_element_type=F32)
            acc_ref[rows, :] = alpha * acc_ref[rows, :] + pv
            m_ref[h] = m_new
        return 0

    lax.fori_loop(0, nblk, att_body, 0)

    for h in range(n_heads):
        rows = slice(h * HEAD_DIM, (h + 1) * HEAD_DIM)
        acc_ref[rows, :] = acc_ref[rows, :] * (1.0 / l_ref[h])
    o_ref[0] = (acc_ref[...].T * ga_ref[0]).astype(BF16)


def _dsa(ki, qi, wit, ka, vat, qa, ga, tab, topk):
    b, l, d_a = ka.shape
    n_heads = d_a // HEAD_DIM
    d_qi = qi.shape[2]
    tq = ATT_TILE
    nq = l // tq
    kernel = functools.partial(_dsa_kernel, topk=topk, n_heads=n_heads)
    return pl.pallas_call(
        kernel,
        grid=(b, nq),
        in_specs=[
            pl.BlockSpec((1, l, LANES), lambda i, j: (i, 0, 0)),
            pl.BlockSpec((1, tq, d_qi), lambda i, j: (i, j, 0)),
            pl.BlockSpec((1, IDX_HEADS, tq), lambda i, j: (i, 0, j)),
            pl.BlockSpec((1, l, d_a), lambda i, j: (i, 0, 0)),
            pl.BlockSpec((1, nq, d_a, tq), lambda i, j: (i, 0, 0, 0)),
            pl.BlockSpec((1, tq, d_a), lambda i, j: (i, j, 0)),
            pl.BlockSpec((1, tq, d_a), lambda i, j: (i, j, 0)),
            pl.BlockSpec(tab.shape, lambda i, j: (0, 0, 0, 0), pipeline_mode=pl.Buffered(1)),
        ],
        out_specs=pl.BlockSpec((1, tq, d_a), lambda i, j: (i, j, 0)),
        out_shape=jax.ShapeDtypeStruct((b, l, d_a), BF16),
        scratch_shapes=[
            pltpu.VMEM((l, tq), F32),
            pltpu.VMEM((l, tq), F32),
            pltpu.VMEM((IDX_HEADS, tq, LANES), BF16),
            pltpu.VMEM((n_heads, tq, LANES), BF16),
            pltpu.VMEM((n_heads, 1, tq), F32),
            pltpu.VMEM((n_heads, 1, tq), F32),
            pltpu.VMEM((d_a, tq), F32),
            pltpu.VMEM((n_heads, tq, tq), F32),
            pltpu.VMEM((n_heads, 1, tq), F32),
        ],
        compiler_params=_params("arbitrary", "arbitrary"),
        name="dsa",
    )(ki, qi, wit, ka, vat, qa, ga, tab)


def _stick_kernel(kb_ref, vbt_ref, qb_ref, gb_ref, u_ref, o_ref,
                  qbm_ref, below_ref, acc_ref, z_ref, lb_ref, *, n_heads):
    tq = o_ref.shape[1]
    ts = tq
    q_blk = pl.program_id(1)
    s_rel = lax.broadcasted_iota(I32, (ts, tq), 0)
    t_rel = lax.broadcasted_iota(I32, (ts, tq), 1)

    _store_head_masked(qbm_ref, qb_ref[0], n_heads)
    below_ref[...] = jnp.zeros(below_ref.shape, F32)
    acc_ref[...] = jnp.zeros(acc_ref.shape, F32)

    def block(j, diagonal):
        s0 = pl.multiple_of(j * ts, ts)
        strict = s_rel < t_rel

        def z_dot(h):
            return lax.dot_general(kb_ref[0, pl.ds(s0, ts), _pair(h)], qbm_ref[h], _NT,
                                   preferred_element_type=F32)

        def terms(z):
            z2 = z * LOG2E
            log_beta = jnp.minimum(z2, 0.0) - jnp.log2(1.0 + jnp.exp2(-jnp.abs(z2)))
            log_om = log_beta - z2
            if diagonal:
                log_om = jnp.where(strict, log_om, 0.0)
            hi = log_om.astype(BF16)
            lo = (log_om - hi.astype(F32)).astype(BF16)
            suffix = jnp.dot(u_ref[...], jnp.concatenate([hi, lo], axis=0),
                             preferred_element_type=F32)
            return log_beta, log_om[0:1, :], suffix

        def finish(h, log_a, om_total):
            rows = slice(h * HEAD_DIM, (h + 1) * HEAD_DIM)
            below = below_ref[h]
            a = jnp.exp2(log_a)
            if diagonal:
                a = jnp.where(strict, a, 0.0)
            pv = jnp.dot(vbt_ref[0, j, rows, :], a.astype(BF16), preferred_element_type=F32)
            acc_ref[rows, :] += pv * jnp.exp2(below)
            below = below + om_total
            below_ref[h] = below
            return below

        for h in range(n_heads):
            z_ref[h] = z_dot(h)
        om0 = []
        for h in range(n_heads):
            log_beta, om_row, suffix = terms(z_ref[h])
            lb_ref[h] = log_beta + suffix
            om0.append(om_row + suffix[0:1, :])
        worst = None
        for h in range(n_heads):
            below = finish(h, lb_ref[h], om0[h])
            worst = below if worst is None else jnp.maximum(worst, below)
        return jnp.max(worst)

    top = block(q_blk, diagonal=True)

    def cond(carry):
        i, top = carry
        return (i <= q_blk) & (top >= EXP2_UNDERFLOW)

    def body(carry):
        i, _ = carry
        return i + 1, block(q_blk - i, diagonal=False)

    lax.while_loop(cond, body, (jnp.int32(1), top))

    o_ref[0] = (acc_ref[...].T * gb_ref[0]).astype(BF16)


def _stick(kb, vbt, qb, gb, u2):
    b, l, d_b = kb.shape
    n_heads = d_b // HEAD_DIM
    tq = ATT_TILE
    nq = l // tq
    return pl.pallas_call(
        functools.partial(_stick_kernel, n_heads=n_heads),
        grid=(b, nq),
        in_specs=[
            pl.BlockSpec((1, l, d_b), lambda i, j: (i, 0, 0)),
            pl.BlockSpec((1, nq, d_b, tq), lambda i, j: (i, 0, 0, 0)),
            pl.BlockSpec((1, tq, d_b), lambda i, j: (i, j, 0)),
            pl.BlockSpec((1, tq, d_b), lambda i, j: (i, j, 0)),
            pl.BlockSpec(u2.shape, lambda i, j: (0, 0)),
        ],
        out_specs=pl.BlockSpec((1, tq, d_b), lambda i, j: (i, j, 0)),
        out_shape=jax.ShapeDtypeStruct((b, l, d_b), BF16),
        scratch_shapes=[
            pltpu.VMEM((n_heads, tq, LANES), BF16),
            pltpu.VMEM((n_heads, 1, tq), F32),
            pltpu.VMEM((d_b, tq), F32),
            pltpu.VMEM((n_heads, tq, tq), F32),
            pltpu.VMEM((n_heads, tq, tq), F32),
        ],
        compiler_params=_params("arbitrary", "arbitrary"),
        name="stick",
    )(kb, vbt, qb, gb, u2)


def _outproj_kernel(x_ref, oa_ref, ob_ref, wa_ref, wb_ref, y_ref):
    y_ref[...] = (x_ref[...]
                  + jnp.dot(oa_ref[...], wa_ref[...], preferred_element_type=F32)
                  + jnp.dot(ob_ref[...], wb_ref[...], preferred_element_type=F32))


def _outproj(x2, oa, ob, wa, wb):
    n, d = x2.shape
    tm = TOKEN_TILE
    row = lambda width: pl.BlockSpec((tm, width), lambda i: (i, 0))
    return pl.pallas_call(
        _outproj_kernel,
        grid=(n // tm,),
        in_specs=[row(d), row(oa.shape[1]), row(ob.shape[1]),
                  pl.BlockSpec(wa.shape, lambda i: (0, 0)),
                  pl.BlockSpec(wb.shape, lambda i: (0, 0))],
        out_specs=row(d),
        out_shape=jax.ShapeDtypeStruct((n, d), F32),
        compiler_params=_params("arbitrary"),
        name="outproj",
    )(x2, oa, ob, wa, wb)


def kernel(x, norm_gain, w_in, q_norm_gain, k_norm_gain, rel_bias, w_out):
    b, l, d = x.shape
    depth = w_in.shape[0]
    d_a = d // 2
    d_b = d - d_a
    h_a = d_a // HEAD_DIM
    d_qi = IDX_HEADS * IDX_DIM
    topk = min(TOPK_MAX, l // 4)
    ts = ATT_TILE
    scale = HEAD_DIM ** -0.5
    idx_scale = (IDX_HEADS * IDX_DIM) ** -0.5

    lane = jnp.arange(LANES)
    gsum = (lane[:, None] // HEAD_DIM == lane[None, :] // HEAD_DIM).astype(BF16)
    col = jnp.arange(2 * ts)[None, :] % ts
    u2 = (col > jnp.arange(ts)[:, None]).astype(BF16)
    tab = _t5_table(rel_bias.astype(F32), ts)

    for layer in range(depth):
        w = w_in[layer]
        o_ki = 4 * d_a + d_qi
        o_wi = o_ki + IDX_DIM
        o_b = o_wi + IDX_HEADS
        w_ki = w[:, o_ki:o_wi]
        w_wi = jnp.pad(w[:, o_wi:o_b] * idx_scale, ((0, 0), (0, LANES - IDX_HEADS)))
        w_b = jnp.concatenate([w[:, o_b:o_b + d_b] * scale, w[:, o_b + d_b:]], axis=1)
        w_all = jnp.concatenate([w[:, :o_ki], w_ki, w_ki, w_wi, w_b], axis=1).astype(BF16)
        qg = jnp.tile(q_norm_gain[layer] * scale, h_a)[None, :]
        kg = jnp.tile(k_norm_gain[layer], h_a)[None, :]

        (qa, ka, vat, ga, qi, ki, wit, qb, kb, vbt, gb) = _inproj(
            x, norm_gain[layer][None, :], w_all, qg, kg, gsum, d_a, d_b, d_qi)
        oa = _dsa(ki, qi, wit, ka, vat, qa, ga, tab, topk)
        ob = _stick(kb, vbt, qb, gb, u2)

        w_o = w_out[layer].astype(BF16)
        x = _outproj(x.reshape(b * l, d), oa.reshape(b * l, d_a), ob.reshape(b * l, d_b),
                     w_o[:d_a], w_o[d_a:]).reshape(b, l, d)
    return x
```

```python
import functools
import math

import jax
import jax.numpy as jnp
import numpy as np
from jax import lax
from jax.experimental import pallas as pl
from jax.experimental.pallas import tpu as pltpu

F32 = jnp.float32
BF16 = jnp.bfloat16
I32 = jnp.int32

HEAD_DIM = 64
IDX_HEADS = 16
IDX_DIM = 64
TOPK_MAX = 256
NUM_BUCKETS = 32
MAX_DISTANCE = 128
RMS_EPS = 1e-6

LANES = 128
NEG_BIG = -1e30
EXP2_UNDERFLOW = -150.0
LOG2E = math.log2(math.e)
BISECT_CAP = 300
BISECT_GROUP = 4

VMEM_LIMIT_BYTES = 56 * 1024 * 1024
TOKEN_TILE = 512
ATT_TILE = 256

_NT = (((1,), (1,)), ((), ()))


def _params(*sem):
    return pltpu.CompilerParams(dimension_semantics=sem, vmem_limit_bytes=VMEM_LIMIT_BYTES)


def _pair(h):
    return slice((h // 2) * LANES, (h // 2 + 1) * LANES)


def _store_head_masked(dst_ref, src, n_heads):
    tq = src.shape[0]
    low = lax.broadcasted_iota(I32, (tq, LANES), 1) < HEAD_DIM
    for h in range(n_heads):
        pair = src[:, _pair(h)].astype(F32)
        keep = low if h % 2 == 0 else jnp.logical_not(low)
        dst_ref[h] = jnp.where(keep, pair, 0.0).astype(BF16)


def _pack_w_in_kernel(wt_ref, o_ref, *, d_a, d_b, d_qi, scale, idx_scale):
    o_ki = 4 * d_a + d_qi
    o_wi = o_ki + IDX_DIM
    o_b = o_wi + IDX_HEADS
    d = wt_ref.shape[1]

    def put(dst, rows):
        o_ref[:, dst:dst + LANES] = rows.T.astype(BF16)

    for c in range(0, o_ki, LANES):
        put(c, wt_ref[c:c + LANES, :])
    k_idx = wt_ref[o_ki:o_wi, :]
    put(o_ki, jnp.concatenate([k_idx, k_idx], axis=0))
    w_idx = wt_ref[o_wi:o_b, :] * idx_scale
    put(o_ki + LANES, jnp.concatenate([w_idx, jnp.zeros((LANES - IDX_HEADS, d), F32)], axis=0))
    for c in range(0, 4 * d_b, LANES):
        rows = wt_ref[o_b + c:o_b + c + LANES, :]
        put(o_ki + 2 * LANES + c, rows * scale if c < d_b else rows)


def _pack_w_in(w_t, d_a, d_b, d_qi, scale, idx_scale):
    p, d = w_t.shape
    width = p + 2 * LANES - IDX_DIM - IDX_HEADS
    return pl.pallas_call(
        functools.partial(_pack_w_in_kernel, d_a=d_a, d_b=d_b, d_qi=d_qi, scale=scale,
                          idx_scale=idx_scale),
        out_shape=jax.ShapeDtypeStruct((d, width), BF16),
        compiler_params=pltpu.CompilerParams(vmem_limit_bytes=VMEM_LIMIT_BYTES),
        name="pack_w_in",
    )(w_t)


def _inproj_kernel(x_ref, gain_ref, w_ref, qg_ref, kg_ref, gsum_ref,
                   qa_ref, ka_ref, vat_ref, ga_ref, qi_ref, ki_ref, wit_ref,
                   qb_ref, kb_ref, vbt_ref, gb_ref, *, d_a, d_b, d_qi):
    x = x_ref[0]
    ms = jnp.mean(x * x, axis=-1, keepdims=True)
    h = (x * lax.rsqrt(ms + RMS_EPS) * gain_ref[...]).astype(BF16)
    ts = vat_ref.shape[3]

    def proj(c0, width):
        return jnp.dot(h, w_ref[:, c0:c0 + width], preferred_element_type=F32)

    def head_norm(y, g):
        sq = y * y
        hi = sq.astype(BF16)
        lo = (sq - hi.astype(F32)).astype(BF16)
        ones = gsum_ref[...]
        ssum = jnp.concatenate(
            [jnp.dot(hi[:, c:c + LANES], ones, preferred_element_type=F32)
             + jnp.dot(lo[:, c:c + LANES], ones, preferred_element_type=F32)
             for c in range(0, y.shape[1], LANES)], axis=1)
        return y * lax.rsqrt(ssum * (1.0 / HEAD_DIM) + RMS_EPS) * g

    def silu(g):
        return g * (1.0 / (1.0 + jnp.exp(-g)))

    def store_key_blocks_t(dst_ref, v):
        vt = v.T.astype(BF16)
        for c in range(dst_ref.shape[1]):
            dst_ref[0, c] = vt[:, c * ts:(c + 1) * ts]

    c = 0
    qa_ref[0] = head_norm(proj(c, d_a), qg_ref[...]).astype(BF16); c += d_a
    ka_ref[0] = head_norm(proj(c, d_a), kg_ref[...]).astype(BF16); c += d_a
    store_key_blocks_t(vat_ref, proj(c, d_a)); c += d_a
    ga_ref[0] = silu(proj(c, d_a)); c += d_a
    qi_ref[0] = proj(c, d_qi).astype(BF16); c += d_qi
    ki_ref[0] = proj(c, LANES).astype(BF16); c += LANES
    wit_ref[0] = proj(c, LANES).T[:IDX_HEADS, :]; c += LANES
    qb_ref[0] = proj(c, d_b).astype(BF16); c += d_b
    kb_ref[0] = proj(c, d_b).astype(BF16); c += d_b
    store_key_blocks_t(vbt_ref, proj(c, d_b)); c += d_b
    gb_ref[0] = silu(proj(c, d_b))


def _inproj(x, gain, w_all, qg, kg, gsum, d_a, d_b, d_qi):
    b, l, d = x.shape
    tm = TOKEN_TILE
    ts = ATT_TILE
    nb = l // ts
    row = lambda width: pl.BlockSpec((1, tm, width), lambda i, j: (i, j, 0))
    const = lambda shape: pl.BlockSpec(shape, lambda i, j: (0, 0))
    vt_spec = lambda ch: pl.BlockSpec((1, tm // ts, ch, ts), lambda i, j: (i, j, 0, 0))
    tok = lambda width, dt: jax.ShapeDtypeStruct((b, l, width), dt)
    out_shapes = [
        tok(d_a, BF16), tok(d_a, BF16), jax.ShapeDtypeStruct((b, nb, d_a, ts), BF16), tok(d_a, F32),
        tok(d_qi, BF16), tok(LANES, BF16), jax.ShapeDtypeStruct((b, IDX_HEADS, l), F32),
        tok(d_b, BF16), tok(d_b, BF16), jax.ShapeDtypeStruct((b, nb, d_b, ts), BF16), tok(d_b, F32),
    ]
    out_specs = [row(d_a), row(d_a), vt_spec(d_a), row(d_a), row(d_qi), row(LANES),
                 pl.BlockSpec((1, IDX_HEADS, tm), lambda i, j: (i, 0, j)),
                 row(d_b), row(d_b), vt_spec(d_b), row(d_b)]
    return pl.pallas_call(
        functools.partial(_inproj_kernel, d_a=d_a, d_b=d_b, d_qi=d_qi),
        grid=(b, l // tm),
        in_specs=[row(d), const(gain.shape),
                  pl.BlockSpec(w_all.shape, lambda i, j: (0, 0), pipeline_mode=pl.Buffered(1)),
                  const(qg.shape), const(kg.shape), const(gsum.shape)],
        out_specs=out_specs,
        out_shape=out_shapes,
        compiler_params=_params("arbitrary", "arbitrary"),
        name="inproj",
    )(x, gain, w_all, qg, kg, gsum)


def _t5_large_thresholds():
    max_exact = NUM_BUCKETS // 2
    d = np.arange(max_exact, 2 * MAX_DISTANCE + 1)
    large = max_exact + (np.log(d.astype(np.float32) / np.float32(max_exact))
                         / np.float32(math.log(MAX_DISTANCE / max_exact))
                         * np.float32(NUM_BUCKETS - max_exact)).astype(np.int32)
    large = np.minimum(large, NUM_BUCKETS - 1)
    return [int(d[np.argmax(large >= k)]) for k in range(max_exact + 1, NUM_BUCKETS)]


def _t5_bucket(dist):
    max_exact = NUM_BUCKETS // 2
    d = jnp.maximum(dist, 0)
    large = jnp.full(d.shape, max_exact, I32)
    for first in _t5_large_thresholds():
        large = large + (d >= first).astype(I32)
    return jnp.where(d < max_exact, d, large)


def _t5_table_kernel(rb_ref, tab_ref, *, n_heads):
    ts = tab_ref.shape[2]
    rows = 16
    far = _t5_large_thresholds()[-1]
    t_rel = lax.broadcasted_iota(I32, (rows, ts), 1)
    s_rel = lax.broadcasted_iota(I32, (rows, ts), 0)

    for i in range(3):
        n_far = max(0, min(ts, i * ts - far + 1)) // rows

        def fill(c, _, i=i):
            r0 = pl.multiple_of(c * rows, rows)
            for h in range(n_heads):
                tab_ref[i, h, pl.ds(r0, rows), :] = jnp.full((rows, ts), rb_ref[NUM_BUCKETS - 1, h], F32)
            return 0

        def compute(c, _, i=i):
            r0 = pl.multiple_of(c * rows, rows)
            bucket = _t5_bucket(t_rel - (r0 + s_rel) + i * ts)
            acc = [jnp.zeros((rows, ts), F32) for _ in range(n_heads)]
            for j in range(NUM_BUCKETS):
                hit = bucket == j
                acc = [jnp.where(hit, rb_ref[j, h], acc[h]) for h in range(n_heads)]
            for h in range(n_heads):
                tab_ref[i, h, pl.ds(r0, rows), :] = acc[h]
            return 0

        lax.fori_loop(0, n_far, fill, 0)
        lax.fori_loop(n_far, ts // rows, compute, 0)


def _t5_table(rel_bias, ts):
    n_heads = rel_bias.shape[1]
    return pl.pallas_call(
        functools.partial(_t5_table_kernel, n_heads=n_heads),
        in_specs=[pl.BlockSpec(memory_space=pltpu.SMEM)],
        out_specs=pl.BlockSpec(memory_space=pltpu.VMEM),
        out_shape=jax.ShapeDtypeStruct((3, n_heads, ts, ts), F32),
        compiler_params=pltpu.CompilerParams(vmem_limit_bytes=VMEM_LIMIT_BYTES),
        name="t5_table",
    )(rel_bias)


def _dsa_kernel(ki_ref, qi_ref, wit_ref, ka_ref, vat_ref, qa_ref, ga_ref, tab_ref, o_ref,
                sc_ref, madd_ref, qim_ref, qam_ref, m_ref, l_ref, acc_ref, lg_ref, bmax_ref,
                *, topk, n_heads):
    tq = o_ref.shape[1]
    ts = tq
    q_blk = pl.program_id(1)
    nblk = q_blk + 1
    s_rel = lax.broadcasted_iota(I32, (ts, tq), 0)
    t_idx = q_blk * tq + lax.broadcasted_iota(I32, (ts, tq), 1)

    def block_start(j):
        return pl.multiple_of(j * ts, ts)

    _store_head_masked(qim_ref, qi_ref[0], IDX_HEADS)
    _store_head_masked(qam_ref, qa_ref[0], n_heads)

    all_selected = nblk * tq <= topk
    n_sel = jnp.where(all_selected, 0, nblk)

    @pl.when(all_selected)
    def _():
        def body(j, _):
            s0 = block_start(j)
            madd_ref[pl.ds(s0, ts), :] = jnp.where(s0 + s_rel > t_idx, NEG_BIG, 0.0)
            return 0

        lax.fori_loop(0, nblk, body, 0)

    def idx_body(j, carry):
        vmin, vmax = carry
        s0 = block_start(j)
        kblk = ki_ref[0, pl.ds(s0, ts), :]
        score = jnp.zeros((ts, tq), F32)
        for h in range(IDX_HEADS):
            dots = lax.dot_general(kblk, qim_ref[h], _NT, preferred_element_type=F32)
            score = score + jnp.maximum(dots, 0.0) * wit_ref[0, h:h + 1, :]
        future = s0 + s_rel > t_idx
        masked = jnp.where(future, -jnp.inf, score)
        sc_ref[pl.ds(s0, ts), :] = masked
        lo8 = jnp.where(future, jnp.inf, score).reshape(ts // 8, 8, tq).min(axis=0)
        hi8 = masked.reshape(ts // 8, 8, tq).max(axis=0)
        return jnp.minimum(vmin, lo8), jnp.maximum(vmax, hi8)

    vmin, vmax = lax.fori_loop(0, n_sel, idx_body,
                               (jnp.full((8, tq), jnp.inf, F32), jnp.full((8, tq), -jnp.inf, F32)))
    vmin = jnp.min(vmin, axis=0, keepdims=True)
    vmax = jnp.max(vmax, axis=0, keepdims=True)

    def count(pred_fn):
        def body(j, c):
            s0 = block_start(j)
            hit = pred_fn(sc_ref[pl.ds(s0, ts), :], s0).reshape(ts // 8, 8, tq)
            c = list(c)
            for r in range(ts // 8):
                c[r % 4] = jnp.where(hit[r], c[r % 4] + 1, c[r % 4])
            return tuple(c)
        z = jnp.zeros((8, tq), I32)
        c8 = lax.fori_loop(0, n_sel, body, (z, z, z, z))
        return jnp.sum(c8[0] + c8[1] + c8[2] + c8[3], axis=0, keepdims=True)

    def midpoint(lo, hi):
        return 0.5 * lo + 0.5 * hi

    def searching(lo, hi, cnt_lo):
        mid = midpoint(lo, hi)
        return (cnt_lo > topk) & (mid > lo) & (mid < hi)

    def any_lane(flag):
        return jnp.max(flag.astype(I32)) > 0

    n_causal = t_idx[0:1, :] + 1
    lo0 = jnp.where(n_causal > topk, vmin, -jnp.inf)
    hi0 = vmax + (jnp.abs(vmax) * 2.0 ** -20 + 1e-30)

    def bisect_cond(carry):
        it, _, _, _, live = carry
        return live & (it < BISECT_CAP)

    def bisect_step(_, carry):
        lo, hi, cnt_lo = carry
        go = searching(lo, hi, cnt_lo)
        mid = midpoint(lo, hi)
        cnt = count(lambda blk, s0: blk >= mid)
        take = go & (cnt >= topk)
        return (jnp.where(take, mid, lo), jnp.where(go & (cnt < topk), mid, hi),
                jnp.where(take, cnt, cnt_lo))

    def bisect_body(carry):
        it, lo, hi, cnt_lo, _ = carry
        lo, hi, cnt_lo = lax.fori_loop(0, BISECT_GROUP, bisect_step, (lo, hi, cnt_lo))
        return it + BISECT_GROUP, lo, hi, cnt_lo, any_lane(searching(lo, hi, cnt_lo))

    _, thr, _, cnt_thr, _ = lax.while_loop(
        bisect_cond, bisect_body,
        (jnp.int32(0), lo0, hi0, n_causal, any_lane(searching(lo0, hi0, n_causal))))

    tie = any_lane(cnt_thr > topk)

    @pl.when(jnp.logical_not(tie))
    def _():
        def body(j, _):
            s0 = block_start(j)
            sel = (sc_ref[pl.ds(s0, ts), :] >= thr) & (s0 + s_rel <= t_idx)
            madd_ref[pl.ds(s0, ts), :] = jnp.where(sel, 0.0, NEG_BIG)
            return 0

        lax.fori_loop(0, n_sel, body, 0)

    @pl.when(tie)
    def _():
        need = topk - count(lambda blk, s0: blk > thr)
        n_bits = sc_ref.shape[0].bit_length()

        def jbody(i, bound):
            cand = bound + jnp.left_shift(jnp.int32(1), n_bits - 1 - i)
            cnt = count(lambda blk, s0: (blk == thr) & (s0 + s_rel < cand))
            return jnp.where(cnt <= need, cand, bound)

        bound = lax.fori_loop(0, n_bits, jbody, jnp.zeros((1, tq), I32))

        def body(j, _):
            s0 = block_start(j)
            blk = sc_ref[pl.ds(s0, ts), :]
            sel = ((blk > thr) | ((blk == thr) & (s0 + s_rel < bound))) & (s0 + s_rel <= t_idx)
            madd_ref[pl.ds(s0, ts), :] = jnp.where(sel, 0.0, NEG_BIG)
            return 0

        lax.fori_loop(0, n_sel, body, 0)

    m_ref[...] = jnp.full(m_ref.shape, NEG_BIG, F32)
    l_ref[...] = jnp.zeros(l_ref.shape, F32)
    acc_ref[...] = jnp.zeros(acc_ref.shape, F32)

    def att_body(j, _):
        s0 = block_start(j)
        madd = madd_ref[pl.ds(s0, ts), :]
        near = jnp.minimum(q_blk - j, 2)
        for h in range(n_heads):
            lg = lax.dot_general(ka_ref[0, pl.ds(s0, ts), _pair(h)], qam_ref[h], _NT,
                                 preferred_element_type=F32)
            lg = (lg + tab_ref[near, h] + madd) * LOG2E
            lg_ref[h] = lg
            bmax_ref[h] = jnp.max(lg, axis=0, keepdims=True)
        for h in range(n_heads):
            rows = slice(h * HEAD_DIM, (h + 1) * HEAD_DIM)
            m_old = m_ref[h]
            m_new = jnp.maximum(m_old, bmax_ref[h])
            alpha = jnp.exp2(m_old - m_new)
            p = jnp.exp2(lg_ref[h] - m_new)
            l_ref[h] = alpha * l_ref[h] + jnp.sum(p, axis=0, keepdims=True)
            pv = jnp.dot(vat_ref[0, j, rows, :], p.astype(BF16), preferred_element_type=F32)
            acc_ref[rows, :] = alpha * acc_ref[rows, :] + pv
            m_ref[h] = m_new
        return 0

    lax.fori_loop(0, nblk, att_body, 0)

    for h in range(n_heads):
        rows = slice(h * HEAD_DIM, (h + 1) * HEAD_DIM)
        acc_ref[rows, :] = acc_ref[rows, :] * (1.0 / l_ref[h])
    o_ref[0] = (acc_ref[...].T * ga_ref[0]).astype(BF16)


def _dsa(ki, qi, wit, ka, vat, qa, ga, tab, topk):
    b, l, d_a = ka.shape
    n_heads = d_a // HEAD_DIM
    d_qi = qi.shape[2]
    tq = ATT_TILE
    nq = l // tq
    kernel = functools.partial(_dsa_kernel, topk=topk, n_heads=n_heads)
    return pl.pallas_call(
        kernel,
        grid=(b, nq),
        in_specs=[
            pl.BlockSpec((1, l, LANES), lambda i, j: (i, 0, 0)),
            pl.BlockSpec((1, tq, d_qi), lambda i, j: (i, j, 0)),
            pl.BlockSpec((1, IDX_HEADS, tq), lambda i, j: (i, 0, j)),
            pl.BlockSpec((1, l, d_a), lambda i, j: (i, 0, 0)),
            pl.BlockSpec((1, nq, d_a, tq), lambda i, j: (i, 0, 0, 0)),
            pl.BlockSpec((1, tq, d_a), lambda i, j: (i, j, 0)),
            pl.BlockSpec((1, tq, d_a), lambda i, j: (i, j, 0)),
            pl.BlockSpec(tab.shape, lambda i, j: (0, 0, 0, 0), pipeline_mode=pl.Buffered(1)),
        ],
        out_specs=pl.BlockSpec((1, tq, d_a), lambda i, j: (i, j, 0)),
        out_shape=jax.ShapeDtypeStruct((b, l, d_a), BF16),
        scratch_shapes=[
            pltpu.VMEM((l, tq), F32),
            pltpu.VMEM((l, tq), F32),
            pltpu.VMEM((IDX_HEADS, tq, LANES), BF16),
            pltpu.VMEM((n_heads, tq, LANES), BF16),
            pltpu.VMEM((n_heads, 1, tq), F32),
            pltpu.VMEM((n_heads, 1, tq), F32),
            pltpu.VMEM((d_a, tq), F32),
            pltpu.VMEM((n_heads, tq, tq), F32),
            pltpu.VMEM((n_heads, 1, tq), F32),
        ],
        compiler_params=_params("arbitrary", "arbitrary"),
        name="dsa",
    )(ki, qi, wit, ka, vat, qa, ga, tab)


def _stick_kernel(kb_ref, vbt_ref, qb_ref, gb_ref, u_ref, x_ref, oa_ref, wa_ref, wb_ref, y_ref,
                  qbm_ref, below_ref, acc_ref, z_ref, lb_ref, *, n_heads):
    tq = y_ref.shape[1]
    ts = tq
    q_blk = pl.program_id(1)
    s_rel = lax.broadcasted_iota(I32, (ts, tq), 0)
    t_rel = lax.broadcasted_iota(I32, (ts, tq), 1)

    _store_head_masked(qbm_ref, qb_ref[0], n_heads)
    below_ref[...] = jnp.zeros(below_ref.shape, F32)
    acc_ref[...] = jnp.zeros(acc_ref.shape, F32)

    def block(j, diagonal):
        s0 = pl.multiple_of(j * ts, ts)
        strict = s_rel < t_rel

        def z_dot(h):
            return lax.dot_general(kb_ref[0, pl.ds(s0, ts), _pair(h)], qbm_ref[h], _NT,
                                   preferred_element_type=F32)

        def terms(z):
            z2 = z * LOG2E
            log_beta = jnp.minimum(z2, 0.0) - jnp.log2(1.0 + jnp.exp2(-jnp.abs(z2)))
            log_om = log_beta - z2
            if diagonal:
                log_om = jnp.where(strict, log_om, 0.0)
            hi = log_om.astype(BF16)
            lo = (log_om - hi.astype(F32)).astype(BF16)
            suffix = jnp.dot(u_ref[...], jnp.concatenate([hi, lo], axis=0),
                             preferred_element_type=F32)
            return log_beta, log_om[0:1, :], suffix

        def finish(h, log_a, om_total):
            rows = slice(h * HEAD_DIM, (h + 1) * HEAD_DIM)
            below = below_ref[h]
            a = jnp.exp2(log_a)
            if diagonal:
                a = jnp.where(strict, a, 0.0)
            pv = jnp.dot(vbt_ref[0, j, rows, :], a.astype(BF16), preferred_element_type=F32)
            acc_ref[rows, :] += pv * jnp.exp2(below)
            below = below + om_total
            below_ref[h] = below
            return below

        for h in range(n_heads):
            z_ref[h] = z_dot(h)
        om0 = []
        for h in range(n_heads):
            log_beta, om_row, suffix = terms(z_ref[h])
            lb_ref[h] = log_beta + suffix
            om0.append(om_row + suffix[0:1, :])
        worst = None
        for h in range(n_heads):
            below = finish(h, lb_ref[h], om0[h])
            worst = below if worst is None else jnp.maximum(worst, below)
        return jnp.max(worst)

    top = block(q_blk, diagonal=True)

    def cond(carry):
        i, top = carry
        return (i <= q_blk) & (top >= EXP2_UNDERFLOW)

    def body(carry):
        i, _ = carry
        return i + 1, block(q_blk - i, diagonal=False)

    lax.while_loop(cond, body, (jnp.int32(1), top))

    ob = (acc_ref[...].T * gb_ref[0]).astype(BF16)
    y_ref[0] = (x_ref[0]
                + jnp.dot(oa_ref[0], wa_ref[...], preferred_element_type=F32)
                + jnp.dot(ob, wb_ref[...], preferred_element_type=F32))


def _stick(kb, vbt, qb, gb, u2, x, oa, wa, wb):
    b, l, d_b = kb.shape
    d = x.shape[2]
    n_heads = d_b // HEAD_DIM
    tq = ATT_TILE
    nq = l // tq
    return pl.pallas_call(
        functools.partial(_stick_kernel, n_heads=n_heads),
        grid=(b, nq),
        in_specs=[
            pl.BlockSpec((1, l, d_b), lambda i, j: (i, 0, 0)),
            pl.BlockSpec((1, nq, d_b, tq), lambda i, j: (i, 0, 0, 0)),
            pl.BlockSpec((1, tq, d_b), lambda i, j: (i, j, 0)),
            pl.BlockSpec((1, tq, d_b), lambda i, j: (i, j, 0)),
            pl.BlockSpec(u2.shape, lambda i, j: (0, 0)),
            pl.BlockSpec((1, tq, d), lambda i, j: (i, j, 0)),
            pl.BlockSpec((1, tq, oa.shape[2]), lambda i, j: (i, j, 0)),
            pl.BlockSpec(wa.shape, lambda i, j: (0, 0)),
            pl.BlockSpec(wb.shape, lambda i, j: (0, 0)),
        ],
        out_specs=pl.BlockSpec((1, tq, d), lambda i, j: (i, j, 0)),
        out_shape=jax.ShapeDtypeStruct((b, l, d), F32),
        scratch_shapes=[
            pltpu.VMEM((n_heads, tq, LANES), BF16),
            pltpu.VMEM((n_heads, 1, tq), F32),
            pltpu.VMEM((d_b, tq), F32),
            pltpu.VMEM((n_heads, tq, tq), F32),
            pltpu.VMEM((n_heads, tq, tq), F32),
        ],
        compiler_params=_params("arbitrary", "arbitrary"),
        name="stick",
    )(kb, vbt, qb, gb, u2, x, oa, wa, wb)


def kernel(x, norm_gain, w_in, q_norm_gain, k_norm_gain, rel_bias, w_out):
    b, l, d = x.shape
    depth = w_in.shape[0]
    d_a = d // 2
    d_b = d - d_a
    h_a = d_a // HEAD_DIM
    d_qi = IDX_HEADS * IDX_DIM
    topk = min(TOPK_MAX, l // 4)
    ts = ATT_TILE
    scale = HEAD_DIM ** -0.5
    idx_scale = (IDX_HEADS * IDX_DIM) ** -0.5

    lane = jnp.arange(LANES)
    gsum = (lane[:, None] // HEAD_DIM == lane[None, :] // HEAD_DIM).astype(BF16)
    col = jnp.arange(2 * ts)[None, :] % ts
    u2 = (col > jnp.arange(ts)[:, None]).astype(BF16)
    tab = _t5_table(rel_bias.astype(F32), ts)

    for layer in range(depth):
        w_all = _pack_w_in(jnp.swapaxes(w_in[layer], 0, 1), d_a, d_b, d_qi, scale, idx_scale)
        qg = jnp.tile(q_norm_gain[layer] * scale, h_a)[None, :]
        kg = jnp.tile(k_norm_gain[layer], h_a)[None, :]

        (qa, ka, vat, ga, qi, ki, wit, qb, kb, vbt, gb) = _inproj(
            x, norm_gain[layer][None, :], w_all, qg, kg, gsum, d_a, d_b, d_qi)
        oa = _dsa(ki, qi, wit, ka, vat, qa, ga, tab, topk)
        w_o = w_out[layer].astype(BF16)
        x = _stick(kb, vbt, qb, gb, u2, x, oa, w_o[:d_a], w_o[d_a:])
    return x
```

```python
import functools
import math

import jax
import jax.numpy as jnp
import numpy as np
from jax import lax
from jax.experimental import pallas as pl
from jax.experimental.pallas import tpu as pltpu

F32 = jnp.float32
BF16 = jnp.bfloat16
I32 = jnp.int32

HEAD_DIM = 64
IDX_HEADS = 16
IDX_DIM = 64
TOPK_MAX = 256
NUM_BUCKETS = 32
MAX_DISTANCE = 128
RMS_EPS = 1e-6

LANES = 128
NEG_BIG = -1e30
EXP2_UNDERFLOW = -150.0
LOG2E = math.log2(math.e)
BISECT_CAP = 300
BISECT_BLIND = 20
BISECT_GROUP = 2

VMEM_LIMIT_BYTES = 56 * 1024 * 1024
TOKEN_TILE = 512
ATT_TILE = 256

_NT = (((1,), (1,)), ((), ()))


def _params(*sem):
    return pltpu.CompilerParams(dimension_semantics=sem, vmem_limit_bytes=VMEM_LIMIT_BYTES)


def _pair(h):
    return slice((h // 2) * LANES, (h // 2 + 1) * LANES)


def _store_head_masked(dst_ref, src):
    rows = src.shape[0]
    low = lax.broadcasted_iota(I32, (rows, LANES), 1) < HEAD_DIM
    for h in range(dst_ref.shape[1]):
        keep = low if h % 2 == 0 else jnp.logical_not(low)
        dst_ref[0, h] = jnp.where(keep, src[:, _pair(h)], 0.0).astype(BF16)


def _pack_w_in_kernel(wt_ref, o_ref, *, d_a, d_b, d_qi, scale, idx_scale):
    o_ki = 4 * d_a + d_qi
    o_wi = o_ki + IDX_DIM
    o_b = o_wi + IDX_HEADS
    d = wt_ref.shape[1]

    def put(dst, rows):
        o_ref[:, dst:dst + LANES] = rows.T.astype(BF16)

    for c in range(0, o_ki, LANES):
        put(c, wt_ref[c:c + LANES, :])
    k_idx = wt_ref[o_ki:o_wi, :]
    put(o_ki, jnp.concatenate([k_idx, k_idx], axis=0))
    w_idx = wt_ref[o_wi:o_b, :] * idx_scale
    put(o_ki + LANES, jnp.concatenate([w_idx, jnp.zeros((LANES - IDX_HEADS, d), F32)], axis=0))
    for c in range(0, 4 * d_b, LANES):
        rows = wt_ref[o_b + c:o_b + c + LANES, :]
        put(o_ki + 2 * LANES + c, rows * scale if c < d_b else rows)


def _pack_w_in(w_t, d_a, d_b, d_qi, scale, idx_scale):
    p, d = w_t.shape
    width = p + 2 * LANES - IDX_DIM - IDX_HEADS
    return pl.pallas_call(
        functools.partial(_pack_w_in_kernel, d_a=d_a, d_b=d_b, d_qi=d_qi, scale=scale,
                          idx_scale=idx_scale),
        out_shape=jax.ShapeDtypeStruct((d, width), BF16),
        compiler_params=pltpu.CompilerParams(vmem_limit_bytes=VMEM_LIMIT_BYTES),
        name="pack_w_in",
    )(w_t)


def _inproj_kernel(x_ref, gain_ref, w_ref, qg_ref, kg_ref, gsum_ref,
                   qa_ref, ka_ref, vat_ref, ga_ref, qi_ref, ki_ref, wit_ref,
                   qb_ref, kb_ref, vbt_ref, gb_ref, *, d_a, d_b, d_qi):
    x = x_ref[0]
    ms = jnp.mean(x * x, axis=-1, keepdims=True)
    h = (x * lax.rsqrt(ms + RMS_EPS) * gain_ref[...]).astype(BF16)
    ts = vat_ref.shape[3]

    def proj(c0, width):
        return jnp.dot(h, w_ref[:, c0:c0 + width], preferred_element_type=F32)

    def head_norm(y, g):
        sq = y * y
        hi = sq.astype(BF16)
        lo = (sq - hi.astype(F32)).astype(BF16)
        ones = gsum_ref[...]
        ssum = jnp.concatenate(
            [jnp.dot(hi[:, c:c + LANES], ones, preferred_element_type=F32)
             + jnp.dot(lo[:, c:c + LANES], ones, preferred_element_type=F32)
             for c in range(0, y.shape[1], LANES)], axis=1)
        return y * lax.rsqrt(ssum * (1.0 / HEAD_DIM) + RMS_EPS) * g

    def silu(g):
        return g * (1.0 / (1.0 + jnp.exp(-g)))

    def store_key_blocks_t(dst_ref, v):
        vt = v.T.astype(BF16)
        for c in range(dst_ref.shape[1]):
            dst_ref[0, c] = vt[:, c * ts:(c + 1) * ts]

    c = 0
    _store_head_masked(qa_ref, head_norm(proj(c, d_a), qg_ref[...])); c += d_a
    ka_ref[0] = head_norm(proj(c, d_a), kg_ref[...]).astype(BF16); c += d_a
    store_key_blocks_t(vat_ref, proj(c, d_a)); c += d_a
    ga_ref[0] = silu(proj(c, d_a)); c += d_a
    _store_head_masked(qi_ref, proj(c, d_qi)); c += d_qi
    ki_ref[0] = proj(c, LANES).astype(BF16); c += LANES
    wit_ref[0] = proj(c, LANES).T[:IDX_HEADS, :]; c += LANES
    _store_head_masked(qb_ref, proj(c, d_b)); c += d_b
    kb_ref[0] = proj(c, d_b).astype(BF16); c += d_b
    store_key_blocks_t(vbt_ref, proj(c, d_b)); c += d_b
    gb_ref[0] = silu(proj(c, d_b))


def _inproj(x, gain, w_all, qg, kg, gsum, d_a, d_b, d_qi):
    b, l, d = x.shape
    tm = TOKEN_TILE
    ts = ATT_TILE
    nb = l // ts
    row = lambda width: pl.BlockSpec((1, tm, width), lambda i, j: (i, j, 0))
    const = lambda shape: pl.BlockSpec(shape, lambda i, j: (0, 0))
    vt_spec = lambda ch: pl.BlockSpec((1, tm // ts, ch, ts), lambda i, j: (i, j, 0, 0))
    tok = lambda width, dt: jax.ShapeDtypeStruct((b, l, width), dt)
    heads = lambda width: jax.ShapeDtypeStruct((b, width // HEAD_DIM, l, LANES), BF16)
    heads_spec = lambda width: pl.BlockSpec((1, width // HEAD_DIM, tm, LANES), lambda i, j: (i, 0, j, 0))
    out_shapes = [
        heads(d_a), tok(d_a, BF16), jax.ShapeDtypeStruct((b, nb, d_a, ts), BF16), tok(d_a, F32),
        heads(d_qi), tok(LANES, BF16), jax.ShapeDtypeStruct((b, IDX_HEADS, l), F32),
        heads(d_b), tok(d_b, BF16), jax.ShapeDtypeStruct((b, nb, d_b, ts), BF16), tok(d_b, F32),
    ]
    out_specs = [heads_spec(d_a), row(d_a), vt_spec(d_a), row(d_a), heads_spec(d_qi), row(LANES),
                 pl.BlockSpec((1, IDX_HEADS, tm), lambda i, j: (i, 0, j)),
                 heads_spec(d_b), row(d_b), vt_spec(d_b), row(d_b)]
    return pl.pallas_call(
        functools.partial(_inproj_kernel, d_a=d_a, d_b=d_b, d_qi=d_qi),
        grid=(b, l // tm),
        in_specs=[row(d), const(gain.shape),
                  pl.BlockSpec(w_all.shape, lambda i, j: (0, 0), pipeline_mode=pl.Buffered(1)),
                  const(qg.shape), const(kg.shape), const(gsum.shape)],
        out_specs=out_specs,
        out_shape=out_shapes,
        compiler_params=_params("arbitrary", "arbitrary"),
        name="inproj",
    )(x, gain, w_all, qg, kg, gsum)


def _t5_large_thresholds():
    max_exact = NUM_BUCKETS // 2
    d = np.arange(max_exact, 2 * MAX_DISTANCE + 1)
    large = max_exact + (np.log(d.astype(np.float32) / np.float32(max_exact))
                         / np.float32(math.log(MAX_DISTANCE / max_exact))
                         * np.float32(NUM_BUCKETS - max_exact)).astype(np.int32)
    large = np.minimum(large, NUM_BUCKETS - 1)
    return [int(d[np.argmax(large >= k)]) for k in range(max_exact + 1, NUM_BUCKETS)]


def _t5_bucket(dist):
    max_exact = NUM_BUCKETS // 2
    d = jnp.maximum(dist, 0)
    large = jnp.full(d.shape, max_exact, I32)
    for first in _t5_large_thresholds():
        large = large + (d >= first).astype(I32)
    return jnp.where(d < max_exact, d, large)


def _t5_table_kernel(rb_ref, tab_ref, *, n_heads):
    ts = tab_ref.shape[2]
    rows = 16
    far = _t5_large_thresholds()[-1]
    t_rel = lax.broadcasted_iota(I32, (rows, ts), 1)
    s_rel = lax.broadcasted_iota(I32, (rows, ts), 0)

    for i in range(3):
        n_far = max(0, min(ts, i * ts - far + 1)) // rows

        def fill(c, _, i=i):
            r0 = pl.multiple_of(c * rows, rows)
            for h in range(n_heads):
                tab_ref[i, h, pl.ds(r0, rows), :] = jnp.full((rows, ts), rb_ref[NUM_BUCKETS - 1, h], F32)
            return 0

        def compute(c, _, i=i):
            r0 = pl.multiple_of(c * rows, rows)
            bucket = _t5_bucket(t_rel - (r0 + s_rel) + i * ts)
            acc = [jnp.zeros((rows, ts), F32) for _ in range(n_heads)]
            for j in range(NUM_BUCKETS):
                hit = bucket == j
                acc = [jnp.where(hit, rb_ref[j, h], acc[h]) for h in range(n_heads)]
            for h in range(n_heads):
                tab_ref[i, h, pl.ds(r0, rows), :] = acc[h]
            return 0

        lax.fori_loop(0, n_far, fill, 0)
        lax.fori_loop(n_far, ts // rows, compute, 0)


def _t5_table(rel_bias, ts):
    n_heads = rel_bias.shape[1]
    return pl.pallas_call(
        functools.partial(_t5_table_kernel, n_heads=n_heads),
        in_specs=[pl.BlockSpec(memory_space=pltpu.SMEM)],
        out_specs=pl.BlockSpec(memory_space=pltpu.VMEM),
        out_shape=jax.ShapeDtypeStruct((3, n_heads, ts, ts), F32),
        compiler_params=pltpu.CompilerParams(vmem_limit_bytes=VMEM_LIMIT_BYTES),
        name="t5_table",
    )(rel_bias)


def _dsa_kernel(ki_ref, qim_ref, wit_ref, ka_ref, vat_ref, qam_ref, ga_ref, tab_ref, o_ref,
                sc_ref, madd_ref, m_ref, l_ref, acc_ref, lg_ref, bmax_ref, *, topk, n_heads):
    tq = o_ref.shape[1]
    ts = tq
    q_blk = pl.program_id(1)
    nblk = q_blk + 1
    s_rel = lax.broadcasted_iota(I32, (ts, tq), 0)
    t_idx = q_blk * tq + lax.broadcasted_iota(I32, (ts, tq), 1)

    def block_start(j):
        return pl.multiple_of(j * ts, ts)

    all_selected = nblk * tq <= topk
    n_sel = jnp.where(all_selected, 0, nblk)

    @pl.when(all_selected)
    def _():
        def body(j, _):
            s0 = block_start(j)
            madd_ref[pl.ds(s0, ts), :] = jnp.where(s0 + s_rel > t_idx, NEG_BIG, 0.0)
            return 0

        lax.fori_loop(0, nblk, body, 0)

    def idx_body(j, carry):
        vmin, vmax = carry
        s0 = block_start(j)
        kblk = ki_ref[0, pl.ds(s0, ts), :]
        score = jnp.zeros((ts, tq), F32)
        for h in range(IDX_HEADS):
            dots = lax.dot_general(kblk, qim_ref[0, h], _NT, preferred_element_type=F32)
            score = score + jnp.maximum(dots, 0.0) * wit_ref[0, h:h + 1, :]
        future = s0 + s_rel > t_idx
        masked = jnp.where(future, -jnp.inf, score)
        sc_ref[pl.ds(s0, ts), :] = masked
        lo8 = jnp.where(future, jnp.inf, score).reshape(ts // 8, 8, tq).min(axis=0)
        hi8 = masked.reshape(ts // 8, 8, tq).max(axis=0)
        return jnp.minimum(vmin, lo8), jnp.maximum(vmax, hi8)

    vmin, vmax = lax.fori_loop(0, n_sel, idx_body,
                               (jnp.full((8, tq), jnp.inf, F32), jnp.full((8, tq), -jnp.inf, F32)))
    vmin = jnp.min(vmin, axis=0, keepdims=True)
    vmax = jnp.max(vmax, axis=0, keepdims=True)

    def count(pred_fn):
        def body(j, c):
            s0 = block_start(j)
            hit = pred_fn(sc_ref[pl.ds(s0, ts), :], s0).reshape(ts // 8, 8, tq)
            c = list(c)
            for r in range(ts // 8):
                c[r % 4] = jnp.where(hit[r], c[r % 4] + 1, c[r % 4])
            return tuple(c)
        z = jnp.zeros((8, tq), I32)
        c8 = lax.fori_loop(0, n_sel, body, (z, z, z, z))
        return jnp.sum(c8[0] + c8[1] + c8[2] + c8[3], axis=0, keepdims=True)

    def midpoint(lo, hi):
        return 0.5 * lo + 0.5 * hi

    def searching(lo, hi, cnt_lo):
        mid = midpoint(lo, hi)
        return (cnt_lo > topk) & (mid > lo) & (mid < hi)

    def any_lane(flag):
        return jnp.max(flag.astype(I32)) > 0

    n_causal = t_idx[0:1, :] + 1
    lo0 = jnp.where(n_causal > topk, vmin, -jnp.inf)
    hi0 = vmax + (jnp.abs(vmax) * 2.0 ** -20 + 1e-30)

    def bisect_cond(carry):
        it, _, _, _, live = carry
        return live & (it < BISECT_CAP)

    def bisect_step(_, carry):
        lo, hi, cnt_lo = carry
        go = searching(lo, hi, cnt_lo)
        mid = midpoint(lo, hi)
        cnt = count(lambda blk, s0: blk >= mid)
        take = go & (cnt >= topk)
        return (jnp.where(take, mid, lo), jnp.where(go & (cnt < topk), mid, hi),
                jnp.where(take, cnt, cnt_lo))

    def bisect_body(carry):
        it, lo, hi, cnt_lo, _ = carry
        lo, hi, cnt_lo = lax.fori_loop(0, BISECT_GROUP, bisect_step, (lo, hi, cnt_lo))
        return it + BISECT_GROUP, lo, hi, cnt_lo, any_lane(searching(lo, hi, cnt_lo))

    lo, hi, cnt_lo = lax.fori_loop(0, BISECT_BLIND, bisect_step, (lo0, hi0, n_causal))
    _, thr, _, cnt_thr, _ = lax.while_loop(
        bisect_cond, bisect_body,
        (jnp.int32(BISECT_BLIND), lo, hi, cnt_lo, any_lane(searching(lo, hi, cnt_lo))))

    tie = any_lane(cnt_thr > topk)

    @pl.when(jnp.logical_not(tie))
    def _():
        def body(j, _):
            s0 = block_start(j)
            sel = (sc_ref[pl.ds(s0, ts), :] >= thr) & (s0 + s_rel <= t_idx)
            madd_ref[pl.ds(s0, ts), :] = jnp.where(sel, 0.0, NEG_BIG)
            return 0

        lax.fori_loop(0, n_sel, body, 0)

    @pl.when(tie)
    def _():
        need = topk - count(lambda blk, s0: blk > thr)
        n_bits = sc_ref.shape[0].bit_length()

        def jbody(i, bound):
            cand = bound + jnp.left_shift(jnp.int32(1), n_bits - 1 - i)
            cnt = count(lambda blk, s0: (blk == thr) & (s0 + s_rel < cand))
            return jnp.where(cnt <= need, cand, bound)

        bound = lax.fori_loop(0, n_bits, jbody, jnp.zeros((1, tq), I32))

        def body(j, _):
            s0 = block_start(j)
            blk = sc_ref[pl.ds(s0, ts), :]
            sel = ((blk > thr) | ((blk == thr) & (s0 + s_rel < bound))) & (s0 + s_rel <= t_idx)
            madd_ref[pl.ds(s0, ts), :] = jnp.where(sel, 0.0, NEG_BIG)
            return 0

        lax.fori_loop(0, n_sel, body, 0)

    m_ref[...] = jnp.full(m_ref.shape, NEG_BIG, F32)
    l_ref[...] = jnp.zeros(l_ref.shape, F32)
    acc_ref[...] = jnp.zeros(acc_ref.shape, F32)

    def att_body(j, _):
        s0 = block_start(j)
        madd = madd_ref[pl.ds(s0, ts), :]
        near = jnp.minimum(q_blk - j, 2)
        for h in range(n_heads):
            lg = lax.dot_general(ka_ref[0, pl.ds(s0, ts), _pair(h)], qam_ref[0, h], _NT,
                                 preferred_element_type=F32)
            lg = (lg + tab_ref[near, h] + madd) * LOG2E
            lg_ref[h] = lg
            bmax_ref[h] = jnp.max(lg, axis=0, keepdims=True)
        for h in range(n_heads):
            rows = slice(h * HEAD_DIM, (h + 1) * HEAD_DIM)
            m_old = m_ref[h]
            m_new = jnp.maximum(m_old, bmax_ref[h])
            alpha = jnp.exp2(m_old - m_new)
            p = jnp.exp2(lg_ref[h] - m_new)
            l_ref[h] = alpha * l_ref[h] + jnp.sum(p, axis=0, keepdims=True)
            pv = jnp.dot(vat_ref[0, j, rows, :], p.astype(BF16), preferred_element_type=F32)
            acc_ref[rows, :] = alpha * acc_ref[rows, :] + pv
            m_ref[h] = m_new
        return 0

    lax.fori_loop(0, nblk, att_body, 0)

    for h in range(n_heads):
        rows = slice(h * HEAD_DIM, (h + 1) * HEAD_DIM)
        acc_ref[rows, :] = acc_ref[rows, :] * (1.0 / l_ref[h])
    o_ref[0] = (acc_ref[...].T * ga_ref[0]).astype(BF16)


def _dsa(ki, qim, wit, ka, vat, qam, ga, tab, topk):
    b, l, d_a = ka.shape
    n_heads = d_a // HEAD_DIM
    tq = ATT_TILE
    nq = l // tq
    kernel = functools.partial(_dsa_kernel, topk=topk, n_heads=n_heads)
    return pl.pallas_call(
        kernel,
        grid=(b, nq),
        in_specs=[
            pl.BlockSpec((1, l, LANES), lambda i, j: (i, 0, 0)),
            pl.BlockSpec((1, IDX_HEADS, tq, LANES), lambda i, j: (i, 0, j, 0)),
            pl.BlockSpec((1, IDX_HEADS, tq), lambda i, j: (i, 0, j)),
            pl.BlockSpec((1, l, d_a), lambda i, j: (i, 0, 0)),
            pl.BlockSpec((1, nq, d_a, tq), lambda i, j: (i, 0, 0, 0)),
            pl.BlockSpec((1, n_heads, tq, LANES), lambda i, j: (i, 0, j, 0)),
            pl.BlockSpec((1, tq, d_a), lambda i, j: (i, j, 0)),
            pl.BlockSpec(tab.shape, lambda i, j: (0, 0, 0, 0), pipeline_mode=pl.Buffered(1)),
        ],
        out_specs=pl.BlockSpec((1, tq, d_a), lambda i, j: (i, j, 0)),
        out_shape=jax.ShapeDtypeStruct((b, l, d_a), BF16),
        scratch_shapes=[
            pltpu.VMEM((l, tq), F32),
            pltpu.VMEM((l, tq), F32),
            pltpu.VMEM((n_heads, 1, tq), F32),
            pltpu.VMEM((n_heads, 1, tq), F32),
            pltpu.VMEM((d_a, tq), F32),
            pltpu.VMEM((n_heads, tq, tq), F32),
            pltpu.VMEM((n_heads, 1, tq), F32),
        ],
        compiler_params=_params("arbitrary", "arbitrary"),
        name="dsa",
    )(ki, qim, wit, ka, vat, qam, ga, tab)


def _stick_kernel(kb_ref, vbt_ref, qbm_ref, gb_ref, u_ref, x_ref, oa_ref, wa_ref, wb_ref, y_ref,
                  below_ref, acc_ref, z_ref, lb_ref, *, n_heads):
    tq = y_ref.shape[1]
    ts = tq
    q_blk = pl.program_id(1)
    s_rel = lax.broadcasted_iota(I32, (ts, tq), 0)
    t_rel = lax.broadcasted_iota(I32, (ts, tq), 1)

    below_ref[...] = jnp.zeros(below_ref.shape, F32)
    acc_ref[...] = jnp.zeros(acc_ref.shape, F32)

    def block(j, diagonal):
        s0 = pl.multiple_of(j * ts, ts)
        strict = s_rel < t_rel

        def z_dot(h):
            return lax.dot_general(kb_ref[0, pl.ds(s0, ts), _pair(h)], qbm_ref[0, h], _NT,
                                   preferred_element_type=F32)

        def terms(z):
            z2 = z * LOG2E
            log_beta = jnp.minimum(z2, 0.0) - jnp.log2(1.0 + jnp.exp2(-jnp.abs(z2)))
            log_om = log_beta - z2
            if diagonal:
                log_om = jnp.where(strict, log_om, 0.0)
            hi = log_om.astype(BF16)
            lo = (log_om - hi.astype(F32)).astype(BF16)
            suffix = jnp.dot(u_ref[...], jnp.concatenate([hi, lo], axis=0),
                             preferred_element_type=F32)
            return log_beta, log_om[0:1, :], suffix

        def finish(h, log_a, om_total):
            rows = slice(h * HEAD_DIM, (h + 1) * HEAD_DIM)
            below = below_ref[h]
            a = jnp.exp2(log_a)
            if diagonal:
                a = jnp.where(strict, a, 0.0)
            pv = jnp.dot(vbt_ref[0, j, rows, :], a.astype(BF16), preferred_element_type=F32)
            acc_ref[rows, :] += pv * jnp.exp2(below)
            below = below + om_total
            below_ref[h] = below
            return below

        for h in range(n_heads):
            z_ref[h] = z_dot(h)
        om0 = []
        for h in range(n_heads):
            log_beta, om_row, suffix = terms(z_ref[h])
            lb_ref[h] = log_beta + suffix
            om0.append(om_row + suffix[0:1, :])
        worst = None
        for h in range(n_heads):
            below = finish(h, lb_ref[h], om0[h])
            worst = below if worst is None else jnp.maximum(worst, below)
        return jnp.max(worst)

    top = block(q_blk, diagonal=True)

    def cond(carry):
        i, top = carry
        return (i <= q_blk) & (top >= EXP2_UNDERFLOW)

    def body(carry):
        i, _ = carry
        return i + 1, block(q_blk - i, diagonal=False)

    lax.while_loop(cond, body, (jnp.int32(1), top))

    ob = (acc_ref[...].T * gb_ref[0]).astype(BF16)
    y_ref[0] = (x_ref[0]
                + jnp.dot(oa_ref[0], wa_ref[...], preferred_element_type=F32)
                + jnp.dot(ob, wb_ref[...], preferred_element_type=F32))


def _stick(kb, vbt, qbm, gb, u2, x, oa, wa, wb):
    b, l, d_b = kb.shape
    d = x.shape[2]
    n_heads = d_b // HEAD_DIM
    tq = ATT_TILE
    nq = l // tq
    return pl.pallas_call(
        functools.partial(_stick_kernel, n_heads=n_heads),
        grid=(b, nq),
        in_specs=[
            pl.BlockSpec((1, l, d_b), lambda i, j: (i, 0, 0)),
            pl.BlockSpec((1, nq, d_b, tq), lambda i, j: (i, 0, 0, 0)),
            pl.BlockSpec((1, n_heads, tq, LANES), lambda i, j: (i, 0, j, 0)),
            pl.BlockSpec((1, tq, d_b), lambda i, j: (i, j, 0)),
            pl.BlockSpec(u2.shape, lambda i, j: (0, 0)),
            pl.BlockSpec((1, tq, d), lambda i, j: (i, j, 0)),
            pl.BlockSpec((1, tq, oa.shape[2]), lambda i, j: (i, j, 0)),
            pl.BlockSpec(wa.shape, lambda i, j: (0, 0)),
            pl.BlockSpec(wb.shape, lambda i, j: (0, 0)),
        ],
        out_specs=pl.BlockSpec((1, tq, d), lambda i, j: (i, j, 0)),
        out_shape=jax.ShapeDtypeStruct((b, l, d), F32),
        scratch_shapes=[
            pltpu.VMEM((n_heads, 1, tq), F32),
            pltpu.VMEM((d_b, tq), F32),
            pltpu.VMEM((n_heads, tq, tq), F32),
            pltpu.VMEM((n_heads, tq, tq), F32),
        ],
        compiler_params=_params("arbitrary", "arbitrary"),
        name="stick",
    )(kb, vbt, qbm, gb, u2, x, oa, wa, wb)


def kernel(x, norm_gain, w_in, q_norm_gain, k_norm_gain, rel_bias, w_out):
    b, l, d = x.shape
    depth = w_in.shape[0]
    d_a = d // 2
    d_b = d - d_a
    h_a = d_a // HEAD_DIM
    d_qi = IDX_HEADS * IDX_DIM
    topk = min(TOPK_MAX, l // 4)
    ts = ATT_TILE
    scale = HEAD_DIM ** -0.5
    idx_scale = (IDX_HEADS * IDX_DIM) ** -0.5

    lane = jnp.arange(LANES)
    gsum = (lane[:, None] // HEAD_DIM == lane[None, :] // HEAD_DIM).astype(BF16)
    col = jnp.arange(2 * ts)[None, :] % ts
    u2 = (col > jnp.arange(ts)[:, None]).astype(BF16)
    tab = _t5_table(rel_bias.astype(F32), ts)

    for layer in range(depth):
        w_all = _pack_w_in(jnp.swapaxes(w_in[layer], 0, 1), d_a, d_b, d_qi, scale, idx_scale)
        qg = jnp.tile(q_norm_gain[layer] * scale, h_a)[None, :]
        kg = jnp.tile(k_norm_gain[layer], h_a)[None, :]

        (qam, ka, vat, ga, qim, ki, wit, qbm, kb, vbt, gb) = _inproj(
            x, norm_gain[layer][None, :], w_all, qg, kg, gsum, d_a, d_b, d_qi)
        oa = _dsa(ki, qim, wit, ka, vat, qam, ga, tab, topk)
        w_o = w_out[layer].astype(BF16)
        x = _stick(kb, vbt, qbm, gb, u2, x, oa, w_o[:d_a], w_o[d_a:])
    return x
```

```python
import functools
import math

import jax
import jax.numpy as jnp
import numpy as np
from jax import lax
from jax.experimental import pallas as pl
from jax.experimental.pallas import tpu as pltpu

F32 = jnp.float32
BF16 = jnp.bfloat16
I32 = jnp.int32

HEAD_DIM = 64
IDX_HEADS = 16
IDX_DIM = 64
TOPK_MAX = 256
NUM_BUCKETS = 32
MAX_DISTANCE = 128
RMS_EPS = 1e-6

LANES = 128
NEG_BIG = -1e30
EXP2_UNDERFLOW = -150.0
LOG2E = math.log2(math.e)
BISECT_CAP = 300
BISECT_BLIND = 20
BISECT_GROUP = 2

VMEM_LIMIT_BYTES = 56 * 1024 * 1024
TOKEN_TILE = 512
ATT_TILE = 256

_NT = (((1,), (1,)), ((), ()))


def _params(*sem):
    return pltpu.CompilerParams(dimension_semantics=sem, vmem_limit_bytes=VMEM_LIMIT_BYTES)


def _pair(h):
    return slice((h // 2) * LANES, (h // 2 + 1) * LANES)


def _store_head_masked(dst_ref, src):
    rows = src.shape[0]
    low = lax.broadcasted_iota(I32, (rows, LANES), 1) < HEAD_DIM
    for h in range(dst_ref.shape[1]):
        keep = low if h % 2 == 0 else jnp.logical_not(low)
        dst_ref[0, h] = jnp.where(keep, src[:, _pair(h)], 0.0).astype(BF16)


def _pack_w_in_kernel(wt_ref, o_ref, *, d_a, d_b, d_qi, scale, idx_scale):
    o_ki = 4 * d_a + d_qi
    o_wi = o_ki + IDX_DIM
    o_b = o_wi + IDX_HEADS
    d = wt_ref.shape[1]

    def put(dst, rows):
        o_ref[:, dst:dst + LANES] = rows.T.astype(BF16)

    for c in range(0, o_ki, LANES):
        put(c, wt_ref[c:c + LANES, :])
    k_idx = wt_ref[o_ki:o_wi, :]
    put(o_ki, jnp.concatenate([k_idx, k_idx], axis=0))
    w_idx = wt_ref[o_wi:o_b, :] * idx_scale
    put(o_ki + LANES, jnp.concatenate([w_idx, jnp.zeros((LANES - IDX_HEADS, d), F32)], axis=0))
    for c in range(0, 4 * d_b, LANES):
        rows = wt_ref[o_b + c:o_b + c + LANES, :]
        put(o_ki + 2 * LANES + c, rows * scale if c < d_b else rows)


def _pack_w_in(w_t, d_a, d_b, d_qi, scale, idx_scale):
    p, d = w_t.shape
    width = p + 2 * LANES - IDX_DIM - IDX_HEADS
    return pl.pallas_call(
        functools.partial(_pack_w_in_kernel, d_a=d_a, d_b=d_b, d_qi=d_qi, scale=scale,
                          idx_scale=idx_scale),
        out_shape=jax.ShapeDtypeStruct((d, width), BF16),
        compiler_params=pltpu.CompilerParams(vmem_limit_bytes=VMEM_LIMIT_BYTES),
        name="pack_w_in",
    )(w_t)


def _inproj_kernel(x_ref, gain_ref, w_ref, qg_ref, kg_ref, gsum_ref,
                   qa_ref, ka_ref, vat_ref, ga_ref, qi_ref, ki_ref, wit_ref,
                   qb_ref, kb_ref, vbt_ref, gb_ref, *, d_a, d_b, d_qi):
    x = x_ref[0]
    ms = jnp.mean(x * x, axis=-1, keepdims=True)
    h = (x * lax.rsqrt(ms + RMS_EPS) * gain_ref[...]).astype(BF16)
    ts = vat_ref.shape[3]

    def proj(c0, width):
        return jnp.dot(h, w_ref[:, c0:c0 + width], preferred_element_type=F32)

    def head_norm(y, g):
        sq = y * y
        hi = sq.astype(BF16)
        lo = (sq - hi.astype(F32)).astype(BF16)
        ones = gsum_ref[...]
        ssum = jnp.concatenate(
            [jnp.dot(hi[:, c:c + LANES], ones, preferred_element_type=F32)
             + jnp.dot(lo[:, c:c + LANES], ones, preferred_element_type=F32)
             for c in range(0, y.shape[1], LANES)], axis=1)
        return y * lax.rsqrt(ssum * (1.0 / HEAD_DIM) + RMS_EPS) * g

    def silu(g):
        return g * (1.0 / (1.0 + jnp.exp(-g)))

    def store_key_blocks_t(dst_ref, v):
        vt = v.T.astype(BF16)
        for c in range(dst_ref.shape[1]):
            dst_ref[0, c] = vt[:, c * ts:(c + 1) * ts]

    c = 0
    _store_head_masked(qa_ref, head_norm(proj(c, d_a), qg_ref[...])); c += d_a
    ka_ref[0] = head_norm(proj(c, d_a), kg_ref[...]).astype(BF16); c += d_a
    store_key_blocks_t(vat_ref, proj(c, d_a)); c += d_a
    ga_ref[0] = silu(proj(c, d_a)); c += d_a
    _store_head_masked(qi_ref, proj(c, d_qi)); c += d_qi
    ki_ref[0] = proj(c, LANES).astype(BF16); c += LANES
    wit_ref[0] = proj(c, LANES).T[:IDX_HEADS, :]; c += LANES
    _store_head_masked(qb_ref, proj(c, d_b)); c += d_b
    kb_ref[0] = proj(c, d_b).astype(BF16); c += d_b
    store_key_blocks_t(vbt_ref, proj(c, d_b)); c += d_b
    gb_ref[0] = silu(proj(c, d_b))


def _inproj(x, gain, w_all, qg, kg, gsum, d_a, d_b, d_qi):
    b, l, d = x.shape
    tm = TOKEN_TILE
    ts = ATT_TILE
    nb = l // ts
    row = lambda width: pl.BlockSpec((1, tm, width), lambda i, j: (i, j, 0))
    const = lambda shape: pl.BlockSpec(shape, lambda i, j: (0, 0))
    vt_spec = lambda ch: pl.BlockSpec((1, tm // ts, ch, ts), lambda i, j: (i, j, 0, 0))
    tok = lambda width, dt: jax.ShapeDtypeStruct((b, l, width), dt)
    heads = lambda width: jax.ShapeDtypeStruct((b, width // HEAD_DIM, l, LANES), BF16)
    heads_spec = lambda width: pl.BlockSpec((1, width // HEAD_DIM, tm, LANES), lambda i, j: (i, 0, j, 0))
    out_shapes = [
        heads(d_a), tok(d_a, BF16), jax.ShapeDtypeStruct((b, nb, d_a, ts), BF16), tok(d_a, F32),
        heads(d_qi), tok(LANES, BF16), jax.ShapeDtypeStruct((b, IDX_HEADS, l), F32),
        heads(d_b), tok(d_b, BF16), jax.ShapeDtypeStruct((b, nb, d_b, ts), BF16), tok(d_b, F32),
    ]
    out_specs = [heads_spec(d_a), row(d_a), vt_spec(d_a), row(d_a), heads_spec(d_qi), row(LANES),
                 pl.BlockSpec((1, IDX_HEADS, tm), lambda i, j: (i, 0, j)),
                 heads_spec(d_b), row(d_b), vt_spec(d_b), row(d_b)]
    return pl.pallas_call(
        functools.partial(_inproj_kernel, d_a=d_a, d_b=d_b, d_qi=d_qi),
        grid=(b, l // tm),
        in_specs=[row(d), const(gain.shape),
                  pl.BlockSpec(w_all.shape, lambda i, j: (0, 0), pipeline_mode=pl.Buffered(1)),
                  const(qg.shape), const(kg.shape), const(gsum.shape)],
        out_specs=out_specs,
        out_shape=out_shapes,
        compiler_params=_params("arbitrary", "arbitrary"),
        name="inproj",
    )(x, gain, w_all, qg, kg, gsum)


def _t5_large_thresholds():
    max_exact = NUM_BUCKETS // 2
    d = np.arange(max_exact, 2 * MAX_DISTANCE + 1)
    large = max_exact + (np.log(d.astype(np.float32) / np.float32(max_exact))
                         / np.float32(math.log(MAX_DISTANCE / max_exact))
                         * np.float32(NUM_BUCKETS - max_exact)).astype(np.int32)
    large = np.minimum(large, NUM_BUCKETS - 1)
    return [int(d[np.argmax(large >= k)]) for k in range(max_exact + 1, NUM_BUCKETS)]


def _t5_bucket(dist):
    max_exact = NUM_BUCKETS // 2
    d = jnp.maximum(dist, 0)
    large = jnp.full(d.shape, max_exact, I32)
    for first in _t5_large_thresholds():
        large = large + (d >= first).astype(I32)
    return jnp.where(d < max_exact, d, large)


def _t5_table_kernel(rb_ref, tab_ref, *, n_heads):
    ts = tab_ref.shape[2]
    rows = 16
    far = _t5_large_thresholds()[-1]
    t_rel = lax.broadcasted_iota(I32, (rows, ts), 1)
    s_rel = lax.broadcasted_iota(I32, (rows, ts), 0)

    for i in range(3):
        n_far = max(0, min(ts, i * ts - far + 1)) // rows

        def fill(c, _, i=i):
            r0 = pl.multiple_of(c * rows, rows)
            for h in range(n_heads):
                tab_ref[i, h, pl.ds(r0, rows), :] = jnp.full((rows, ts), rb_ref[NUM_BUCKETS - 1, h], F32)
            return 0

        def compute(c, _, i=i):
            r0 = pl.multiple_of(c * rows, rows)
            bucket = _t5_bucket(t_rel - (r0 + s_rel) + i * ts)
            acc = [jnp.zeros((rows, ts), F32) for _ in range(n_heads)]
            for j in range(NUM_BUCKETS):
                hit = bucket == j
                acc = [jnp.where(hit, rb_ref[j, h], acc[h]) for h in range(n_heads)]
            for h in range(n_heads):
                tab_ref[i, h, pl.ds(r0, rows), :] = acc[h]
            return 0

        lax.fori_loop(0, n_far, fill, 0)
        lax.fori_loop(n_far, ts // rows, compute, 0)


def _t5_table(rel_bias, ts):
    n_heads = rel_bias.shape[1]
    return pl.pallas_call(
        functools.partial(_t5_table_kernel, n_heads=n_heads),
        in_specs=[pl.BlockSpec(memory_space=pltpu.SMEM)],
        out_specs=pl.BlockSpec(memory_space=pltpu.VMEM),
        out_shape=jax.ShapeDtypeStruct((3, n_heads, ts, ts), F32),
        compiler_params=pltpu.CompilerParams(vmem_limit_bytes=VMEM_LIMIT_BYTES),
        name="t5_table",
    )(rel_bias)


def _dsa_kernel(ki_ref, qin_ref, witn_ref, ka_ref, vat_ref, qam_ref, ga_ref, tab_ref, o_ref,
                sc_ref, mm_ref, madd_ref, m_ref, l_ref, acc_ref, lg_ref, bmax_ref, *, topk, n_heads):
    tq = o_ref.shape[1]
    ts = tq
    q_blk = pl.program_id(1)
    nblk = q_blk + 1
    cur = q_blk % 2
    nxt = 1 - cur
    s_rel = lax.broadcasted_iota(I32, (ts, tq), 0)
    t_rel = lax.broadcasted_iota(I32, (ts, tq), 1)
    t_idx = q_blk * tq + t_rel

    def block_start(j):
        return pl.multiple_of(j * ts, ts)

    all_selected = nblk * tq <= topk
    n_sel = jnp.where(all_selected, 0, nblk)

    @pl.when(all_selected)
    def _():
        mm_ref[cur] = jnp.zeros(mm_ref.shape[1:], F32)

        def body(j, _):
            s0 = block_start(j)
            madd_ref[pl.ds(s0, ts), :] = jnp.where(s0 + s_rel > t_idx, NEG_BIG, 0.0)
            return 0

        lax.fori_loop(0, nblk, body, 0)

    vmin = jnp.min(mm_ref[cur, 0:8, :], axis=0, keepdims=True)
    vmax = jnp.max(mm_ref[cur, 8:16, :], axis=0, keepdims=True)

    def count(pred_fn):
        def body(j, c):
            s0 = block_start(j)
            hit = pred_fn(sc_ref[cur, pl.ds(s0, ts), :], s0).reshape(ts // 8, 8, tq)
            c = list(c)
            for r in range(ts // 8):
                c[r % 4] = jnp.where(hit[r], c[r % 4] + 1, c[r % 4])
            return tuple(c)
        z = jnp.zeros((8, tq), I32)
        c8 = lax.fori_loop(0, n_sel, body, (z, z, z, z))
        return jnp.sum(c8[0] + c8[1] + c8[2] + c8[3], axis=0, keepdims=True)

    def midpoint(lo, hi):
        return 0.5 * lo + 0.5 * hi

    def searching(lo, hi, cnt_lo):
        mid = midpoint(lo, hi)
        return (cnt_lo > topk) & (mid > lo) & (mid < hi)

    def any_lane(flag):
        return jnp.max(flag.astype(I32)) > 0

    n_causal = t_idx[0:1, :] + 1
    lo0 = jnp.where(n_causal > topk, vmin, -jnp.inf)
    hi0 = vmax + (jnp.abs(vmax) * 2.0 ** -20 + 1e-30)

    def bisect_cond(carry):
        it, _, _, _, live = carry
        return live & (it < BISECT_CAP)

    def bisect_step(_, carry):
        lo, hi, cnt_lo = carry
        go = searching(lo, hi, cnt_lo)
        mid = midpoint(lo, hi)
        cnt = count(lambda blk, s0: blk >= mid)
        take = go & (cnt >= topk)
        return (jnp.where(take, mid, lo), jnp.where(go & (cnt < topk), mid, hi),
                jnp.where(take, cnt, cnt_lo))

    def bisect_body(carry):
        it, lo, hi, cnt_lo, _ = carry
        lo, hi, cnt_lo = lax.fori_loop(0, BISECT_GROUP, bisect_step, (lo, hi, cnt_lo))
        return it + BISECT_GROUP, lo, hi, cnt_lo, any_lane(searching(lo, hi, cnt_lo))

    n_blind = jnp.where(all_selected, 0, BISECT_BLIND)
    lo, hi, cnt_lo = lax.fori_loop(0, n_blind, bisect_step, (lo0, hi0, n_causal))
    _, thr, _, cnt_thr, _ = lax.while_loop(
        bisect_cond, bisect_body,
        (jnp.int32(BISECT_BLIND), lo, hi, cnt_lo,
         jnp.logical_not(all_selected) & any_lane(searching(lo, hi, cnt_lo))))

    tie = jnp.logical_not(all_selected) & any_lane(cnt_thr > topk)

    @pl.when(jnp.logical_not(tie))
    def _():
        def body(j, _):
            s0 = block_start(j)
            sel = (sc_ref[cur, pl.ds(s0, ts), :] >= thr) & (s0 + s_rel <= t_idx)
            madd_ref[pl.ds(s0, ts), :] = jnp.where(sel, 0.0, NEG_BIG)
            return 0

        lax.fori_loop(0, n_sel, body, 0)

    @pl.when(tie)
    def _():
        need = topk - count(lambda blk, s0: blk > thr)
        n_bits = sc_ref.shape[1].bit_length()

        def jbody(i, bound):
            cand = bound + jnp.left_shift(jnp.int32(1), n_bits - 1 - i)
            cnt = count(lambda blk, s0: (blk == thr) & (s0 + s_rel < cand))
            return jnp.where(cnt <= need, cand, bound)

        bound = lax.fori_loop(0, n_bits, jbody, jnp.zeros((1, tq), I32))

        def body(j, _):
            s0 = block_start(j)
            blk = sc_ref[cur, pl.ds(s0, ts), :]
            sel = ((blk > thr) | ((blk == thr) & (s0 + s_rel < bound))) & (s0 + s_rel <= t_idx)
            madd_ref[pl.ds(s0, ts), :] = jnp.where(sel, 0.0, NEG_BIG)
            return 0

        lax.fori_loop(0, n_sel, body, 0)

    m_ref[...] = jnp.full(m_ref.shape, NEG_BIG, F32)
    l_ref[...] = jnp.zeros(l_ref.shape, F32)
    acc_ref[...] = jnp.zeros(acc_ref.shape, F32)

    def attend(j):
        s0 = block_start(j)
        madd = madd_ref[pl.ds(s0, ts), :]
        near = jnp.minimum(q_blk - j, 2)
        for h in range(n_heads):
            lg = lax.dot_general(ka_ref[0, pl.ds(s0, ts), _pair(h)], qam_ref[0, h], _NT,
                                 preferred_element_type=F32)
            lg = (lg + tab_ref[near, h] + madd) * LOG2E
            lg_ref[h] = lg
            bmax_ref[h] = jnp.max(lg, axis=0, keepdims=True)
        for h in range(n_heads):
            rows = slice(h * HEAD_DIM, (h + 1) * HEAD_DIM)
            m_old = m_ref[h]
            m_new = jnp.maximum(m_old, bmax_ref[h])
            alpha = jnp.exp2(m_old - m_new)
            p = jnp.exp2(lg_ref[h] - m_new)
            l_ref[h] = alpha * l_ref[h] + jnp.sum(p, axis=0, keepdims=True)
            pv = jnp.dot(vat_ref[0, j, rows, :], p.astype(BF16), preferred_element_type=F32)
            acc_ref[rows, :] = alpha * acc_ref[rows, :] + pv
            m_ref[h] = m_new

    def index_next(j, carry, diagonal):
        vmin8, vmax8 = carry
        s0 = block_start(j)
        kblk = ki_ref[0, pl.ds(s0, ts), :]
        score = jnp.zeros((ts, tq), F32)
        for h in range(IDX_HEADS):
            dots = lax.dot_general(kblk, qin_ref[0, h], _NT, preferred_element_type=F32)
            score = score + jnp.maximum(dots, 0.0) * witn_ref[0, h:h + 1, :]
        if diagonal:
            future = s_rel > t_rel
            low = jnp.where(future, jnp.inf, score)
            score = jnp.where(future, -jnp.inf, score)
        else:
            low = score
        sc_ref[nxt, pl.ds(s0, ts), :] = score
        return (jnp.minimum(vmin8, low.reshape(ts // 8, 8, tq).min(axis=0)),
                jnp.maximum(vmax8, score.reshape(ts // 8, 8, tq).max(axis=0)))

    has_next = q_blk + 1 < pl.num_programs(1)

    @pl.when(has_next)
    def _():
        def body(j, carry):
            attend(j)
            return index_next(j, carry, diagonal=False)

        carry = lax.fori_loop(0, nblk, body, (jnp.full((8, tq), jnp.inf, F32),
                                              jnp.full((8, tq), -jnp.inf, F32)))
        vmin8, vmax8 = index_next(nblk, carry, diagonal=True)
        mm_ref[nxt, 0:8, :] = vmin8
        mm_ref[nxt, 8:16, :] = vmax8

    @pl.when(jnp.logical_not(has_next))
    def _():
        def body(j, _):
            attend(j)
            return 0

        lax.fori_loop(0, nblk, body, 0)

    for h in range(n_heads):
        rows = slice(h * HEAD_DIM, (h + 1) * HEAD_DIM)
        acc_ref[rows, :] = acc_ref[rows, :] * (1.0 / l_ref[h])
    o_ref[0] = (acc_ref[...].T * ga_ref[0]).astype(BF16)


def _dsa(ki, qim, wit, ka, vat, qam, ga, tab, topk):
    b, l, d_a = ka.shape
    n_heads = d_a // HEAD_DIM
    tq = ATT_TILE
    nq = l // tq
    assert tq <= topk, "the first query tile must not need indexer scores"
    next_tile = lambda j: jnp.minimum(j + 1, nq - 1)
    kernel = functools.partial(_dsa_kernel, topk=topk, n_heads=n_heads)
    return pl.pallas_call(
        kernel,
        grid=(b, nq),
        in_specs=[
            pl.BlockSpec((1, l, LANES), lambda i, j: (i, 0, 0)),
            pl.BlockSpec((1, IDX_HEADS, tq, LANES), lambda i, j: (i, 0, next_tile(j), 0)),
            pl.BlockSpec((1, IDX_HEADS, tq), lambda i, j: (i, 0, next_tile(j))),
            pl.BlockSpec((1, l, d_a), lambda i, j: (i, 0, 0)),
            pl.BlockSpec((1, nq, d_a, tq), lambda i, j: (i, 0, 0, 0)),
            pl.BlockSpec((1, n_heads, tq, LANES), lambda i, j: (i, 0, j, 0)),
            pl.BlockSpec((1, tq, d_a), lambda i, j: (i, j, 0)),
            pl.BlockSpec(tab.shape, lambda i, j: (0, 0, 0, 0), pipeline_mode=pl.Buffered(1)),
        ],
        out_specs=pl.BlockSpec((1, tq, d_a), lambda i, j: (i, j, 0)),
        out_shape=jax.ShapeDtypeStruct((b, l, d_a), BF16),
        scratch_shapes=[
            pltpu.VMEM((2, l, tq), F32),
            pltpu.VMEM((2, 16, tq), F32),
            pltpu.VMEM((l, tq), F32),
            pltpu.VMEM((n_heads, 1, tq), F32),
            pltpu.VMEM((n_heads, 1, tq), F32),
            pltpu.VMEM((d_a, tq), F32),
            pltpu.VMEM((n_heads, tq, tq), F32),
            pltpu.VMEM((n_heads, 1, tq), F32),
        ],
        compiler_params=_params("arbitrary", "arbitrary"),
        name="dsa",
    )(ki, qim, wit, ka, vat, qam, ga, tab)


def _stick_kernel(kb_ref, vbt_ref, qbm_ref, gb_ref, u_ref, x_ref, oa_ref, wa_ref, wb_ref, y_ref,
                  below_ref, acc_ref, z_ref, lb_ref, *, n_heads):
    tq = y_ref.shape[1]
    ts = tq
    q_blk = pl.program_id(1)
    s_rel = lax.broadcasted_iota(I32, (ts, tq), 0)
    t_rel = lax.broadcasted_iota(I32, (ts, tq), 1)

    below_ref[...] = jnp.zeros(below_ref.shape, F32)
    acc_ref[...] = jnp.zeros(acc_ref.shape, F32)

    def block(j, diagonal):
        s0 = pl.multiple_of(j * ts, ts)
        strict = s_rel < t_rel

        def z_dot(h):
            return lax.dot_general(kb_ref[0, pl.ds(s0, ts), _pair(h)], qbm_ref[0, h], _NT,
                                   preferred_element_type=F32)

        def terms(z):
            z2 = z * LOG2E
            log_beta = jnp.minimum(z2, 0.0) - jnp.log2(1.0 + jnp.exp2(-jnp.abs(z2)))
            log_om = log_beta - z2
            if diagonal:
                log_om = jnp.where(strict, log_om, 0.0)
            hi = log_om.astype(BF16)
            lo = (log_om - hi.astype(F32)).astype(BF16)
            suffix = jnp.dot(u_ref[...], jnp.concatenate([hi, lo], axis=0),
                             preferred_element_type=F32)
            return log_beta, log_om[0:1, :], suffix

        def finish(h, log_a, om_total):
            rows = slice(h * HEAD_DIM, (h + 1) * HEAD_DIM)
            below = below_ref[h]
            a = jnp.exp2(log_a)
            if diagonal:
                a = jnp.where(strict, a, 0.0)
            pv = jnp.dot(vbt_ref[0, j, rows, :], a.astype(BF16), preferred_element_type=F32)
            acc_ref[rows, :] += pv * jnp.exp2(below)
            below = below + om_total
            below_ref[h] = below
            return below

        for h in range(n_heads):
            z_ref[h] = z_dot(h)
        om0 = []
        for h in range(n_heads):
            log_beta, om_row, suffix = terms(z_ref[h])
            lb_ref[h] = log_beta + suffix
            om0.append(om_row + suffix[0:1, :])
        worst = None
        for h in range(n_heads):
            below = finish(h, lb_ref[h], om0[h])
            worst = below if worst is None else jnp.maximum(worst, below)
        return jnp.max(worst)

    top = block(q_blk, diagonal=True)

    def cond(carry):
        i, top = carry
        return (i <= q_blk) & (top >= EXP2_UNDERFLOW)

    def body(carry):
        i, _ = carry
        return i + 1, block(q_blk - i, diagonal=False)

    lax.while_loop(cond, body, (jnp.int32(1), top))

    ob = (acc_ref[...].T * gb_ref[0]).astype(BF16)
    y_ref[0] = (x_ref[0]
                + jnp.dot(oa_ref[0], wa_ref[...], preferred_element_type=F32)
                + jnp.dot(ob, wb_ref[...], preferred_element_type=F32))


def _stick(kb, vbt, qbm, gb, u2, x, oa, wa, wb):
    b, l, d_b = kb.shape
    d = x.shape[2]
    n_heads = d_b // HEAD_DIM
    tq = ATT_TILE
    nq = l // tq
    return pl.pallas_call(
        functools.partial(_stick_kernel, n_heads=n_heads),
        grid=(b, nq),
        in_specs=[
            pl.BlockSpec((1, l, d_b), lambda i, j: (i, 0, 0)),
            pl.BlockSpec((1, nq, d_b, tq), lambda i, j: (i, 0, 0, 0)),
            pl.BlockSpec((1, n_heads, tq, LANES), lambda i, j: (i, 0, j, 0)),
            pl.BlockSpec((1, tq, d_b), lambda i, j: (i, j, 0)),
            pl.BlockSpec(u2.shape, lambda i, j: (0, 0)),
            pl.BlockSpec((1, tq, d), lambda i, j: (i, j, 0)),
            pl.BlockSpec((1, tq, oa.shape[2]), lambda i, j: (i, j, 0)),
            pl.BlockSpec(wa.shape, lambda i, j: (0, 0)),
            pl.BlockSpec(wb.shape, lambda i, j: (0, 0)),
        ],
        out_specs=pl.BlockSpec((1, tq, d), lambda i, j: (i, j, 0)),
        out_shape=jax.ShapeDtypeStruct((b, l, d), F32),
        scratch_shapes=[
            pltpu.VMEM((n_heads, 1, tq), F32),
            pltpu.VMEM((d_b, tq), F32),
            pltpu.VMEM((n_heads, tq, tq), F32),
            pltpu.VMEM((n_heads, tq, tq), F32),
        ],
        compiler_params=_params("arbitrary", "arbitrary"),
        name="stick",
    )(kb, vbt, qbm, gb, u2, x, oa, wa, wb)


def kernel(x, norm_gain, w_in, q_norm_gain, k_norm_gain, rel_bias, w_out):
    b, l, d = x.shape
    depth = w_in.shape[0]
    d_a = d // 2
    d_b = d - d_a
    h_a = d_a // HEAD_DIM
    d_qi = IDX_HEADS * IDX_DIM
    topk = min(TOPK_MAX, l // 4)
    ts = ATT_TILE
    scale = HEAD_DIM ** -0.5
    idx_scale = (IDX_HEADS * IDX_DIM) ** -0.5

    lane = jnp.arange(LANES)
    gsum = (lane[:, None] // HEAD_DIM == lane[None, :] // HEAD_DIM).astype(BF16)
    col = jnp.arange(2 * ts)[None, :] % ts
    u2 = (col > jnp.arange(ts)[:, None]).astype(BF16)
    tab = _t5_table(rel_bias.astype(F32), ts)

    for layer in range(depth):
        w_all = _pack_w_in(jnp.swapaxes(w_in[layer], 0, 1), d_a, d_b, d_qi, scale, idx_scale)
        qg = jnp.tile(q_norm_gain[layer] * scale, h_a)[None, :]
        kg = jnp.tile(k_norm_gain[layer], h_a)[None, :]

        (qam, ka, vat, ga, qim, ki, wit, qbm, kb, vbt, gb) = _inproj(
            x, norm_gain[layer][None, :], w_all, qg, kg, gsum, d_a, d_b, d_qi)
        oa = _dsa(ki, qim, wit, ka, vat, qam, ga, tab, topk)
        w_o = w_out[layer].astype(BF16)
        x = _stick(kb, vbt, qbm, gb, u2, x, oa, w_o[:d_a], w_o[d_a:])
    return x
```

```python
import functools
import math

import jax
import jax.numpy as jnp
import numpy as np
from jax import lax
from jax.experimental import pallas as pl
from jax.experimental.pallas import tpu as pltpu

F32 = jnp.float32
BF16 = jnp.bfloat16
I32 = jnp.int32

HEAD_DIM = 64
IDX_HEADS = 16
IDX_DIM = 64
TOPK_MAX = 256
NUM_BUCKETS = 32
MAX_DISTANCE = 128
RMS_EPS = 1e-6

LANES = 128
NEG_BIG = -1e30
EXP2_UNDERFLOW = -150.0
LOG2E = math.log2(math.e)
BISECT_CAP = 300
BISECT_BLIND = 20
BISECT_GROUP = 2

VMEM_LIMIT_BYTES = 56 * 1024 * 1024
TOKEN_TILE = 512
ATT_TILE = 256

_NT = (((1,), (1,)), ((), ()))


def _params(*sem):
    return pltpu.CompilerParams(dimension_semantics=sem, vmem_limit_bytes=VMEM_LIMIT_BYTES)


def _pair(h):
    return slice((h // 2) * LANES, (h // 2 + 1) * LANES)


def _store_head_masked(dst_ref, src):
    rows = src.shape[0]
    low = lax.broadcasted_iota(I32, (rows, LANES), 1) < HEAD_DIM
    for h in range(dst_ref.shape[1]):
        keep = low if h % 2 == 0 else jnp.logical_not(low)
        dst_ref[0, h] = jnp.where(keep, src[:, _pair(h)], 0.0).astype(BF16)


def _pack_w_in_kernel(wt_ref, o_ref, *, d_a, d_b, d_qi, scale, idx_scale):
    o_ki = 4 * d_a + d_qi
    o_wi = o_ki + IDX_DIM
    o_b = o_wi + IDX_HEADS
    d = wt_ref.shape[1]

    def put(dst, rows):
        o_ref[:, dst:dst + LANES] = rows.T.astype(BF16)

    for c in range(0, o_ki, LANES):
        put(c, wt_ref[c:c + LANES, :])
    k_idx = wt_ref[o_ki:o_wi, :]
    put(o_ki, jnp.concatenate([k_idx, k_idx], axis=0))
    w_idx = wt_ref[o_wi:o_b, :] * idx_scale
    put(o_ki + LANES, jnp.concatenate([w_idx, jnp.zeros((LANES - IDX_HEADS, d), F32)], axis=0))
    for c in range(0, 4 * d_b, LANES):
        rows = wt_ref[o_b + c:o_b + c + LANES, :]
        put(o_ki + 2 * LANES + c, rows * scale if c < d_b else rows)


def _pack_w_in(w_t, d_a, d_b, d_qi, scale, idx_scale):
    p, d = w_t.shape
    width = p + 2 * LANES - IDX_DIM - IDX_HEADS
    return pl.pallas_call(
        functools.partial(_pack_w_in_kernel, d_a=d_a, d_b=d_b, d_qi=d_qi, scale=scale,
                          idx_scale=idx_scale),
        out_shape=jax.ShapeDtypeStruct((d, width), BF16),
        compiler_params=pltpu.CompilerParams(vmem_limit_bytes=VMEM_LIMIT_BYTES),
        name="pack_w_in",
    )(w_t)


def _inproj_kernel(x_ref, gain_ref, w_ref, qg_ref, kg_ref, gsum_ref,
                   qa_ref, ka_ref, vat_ref, ga_ref, qi_ref, ki_ref, wit_ref,
                   qb_ref, kb_ref, vbt_ref, gb_ref, *, d_a, d_b, d_qi):
    x = x_ref[0]
    ms = jnp.mean(x * x, axis=-1, keepdims=True)
    h = (x * lax.rsqrt(ms + RMS_EPS) * gain_ref[...]).astype(BF16)
    ts = vat_ref.shape[3]

    def proj(c0, width):
        return jnp.dot(h, w_ref[:, c0:c0 + width], preferred_element_type=F32)

    def head_norm(y, g):
        sq = y * y
        hi = sq.astype(BF16)
        lo = (sq - hi.astype(F32)).astype(BF16)
        ones = gsum_ref[...]
        ssum = jnp.concatenate(
            [jnp.dot(hi[:, c:c + LANES], ones, preferred_element_type=F32)
             + jnp.dot(lo[:, c:c + LANES], ones, preferred_element_type=F32)
             for c in range(0, y.shape[1], LANES)], axis=1)
        return y * lax.rsqrt(ssum * (1.0 / HEAD_DIM) + RMS_EPS) * g

    def silu(g):
        return g * (1.0 / (1.0 + jnp.exp(-g)))

    def store_key_blocks_t(dst_ref, v):
        vt = v.T.astype(BF16)
        for c in range(dst_ref.shape[1]):
            dst_ref[0, c] = vt[:, c * ts:(c + 1) * ts]

    c = 0
    _store_head_masked(qa_ref, head_norm(proj(c, d_a), qg_ref[...])); c += d_a
    ka_ref[0] = head_norm(proj(c, d_a), kg_ref[...]).astype(BF16); c += d_a
    store_key_blocks_t(vat_ref, proj(c, d_a)); c += d_a
    ga_ref[0] = silu(proj(c, d_a)); c += d_a
    _store_head_masked(qi_ref, proj(c, d_qi)); c += d_qi
    ki_ref[0] = proj(c, LANES).astype(BF16); c += LANES
    wit_ref[0] = proj(c, LANES).T[:IDX_HEADS, :]; c += LANES
    _store_head_masked(qb_ref, proj(c, d_b)); c += d_b
    kb_ref[0] = proj(c, d_b).astype(BF16); c += d_b
    store_key_blocks_t(vbt_ref, proj(c, d_b)); c += d_b
    gb_ref[0] = silu(proj(c, d_b))


def _inproj(x, gain, w_all, qg, kg, gsum, d_a, d_b, d_qi):
    b, l, d = x.shape
    tm = TOKEN_TILE
    ts = ATT_TILE
    nb = l // ts
    row = lambda width: pl.BlockSpec((1, tm, width), lambda i, j: (i, j, 0))
    const = lambda shape: pl.BlockSpec(shape, lambda i, j: (0, 0))
    vt_spec = lambda ch: pl.BlockSpec((1, tm // ts, ch, ts), lambda i, j: (i, j, 0, 0))
    tok = lambda width, dt: jax.ShapeDtypeStruct((b, l, width), dt)
    heads = lambda width: jax.ShapeDtypeStruct((b, width // HEAD_DIM, l, LANES), BF16)
    heads_spec = lambda width: pl.BlockSpec((1, width // HEAD_DIM, tm, LANES), lambda i, j: (i, 0, j, 0))
    out_shapes = [
        heads(d_a), tok(d_a, BF16), jax.ShapeDtypeStruct((b, nb, d_a, ts), BF16), tok(d_a, F32),
        heads(d_qi), tok(LANES, BF16), jax.ShapeDtypeStruct((b, IDX_HEADS, l), F32),
        heads(d_b), tok(d_b, BF16), jax.ShapeDtypeStruct((b, nb, d_b, ts), BF16), tok(d_b, F32),
    ]
    out_specs = [heads_spec(d_a), row(d_a), vt_spec(d_a), row(d_a), heads_spec(d_qi), row(LANES),
                 pl.BlockSpec((1, IDX_HEADS, tm), lambda i, j: (i, 0, j)),
                 heads_spec(d_b), row(d_b), vt_spec(d_b), row(d_b)]
    return pl.pallas_call(
        functools.partial(_inproj_kernel, d_a=d_a, d_b=d_b, d_qi=d_qi),
        grid=(b, l // tm),
        in_specs=[row(d), const(gain.shape),
                  pl.BlockSpec(w_all.shape, lambda i, j: (0, 0), pipeline_mode=pl.Buffered(1)),
                  const(qg.shape), const(kg.shape), const(gsum.shape)],
        out_specs=out_specs,
        out_shape=out_shapes,
        compiler_params=_params("arbitrary", "arbitrary"),
        name="inproj",
    )(x, gain, w_all, qg, kg, gsum)


def _t5_large_thresholds():
    max_exact = NUM_BUCKETS // 2
    d = np.arange(max_exact, 2 * MAX_DISTANCE + 1)
    large = max_exact + (np.log(d.astype(np.float32) / np.float32(max_exact))
                         / np.float32(math.log(MAX_DISTANCE / max_exact))
                         * np.float32(NUM_BUCKETS - max_exact)).astype(np.int32)
    large = np.minimum(large, NUM_BUCKETS - 1)
    return [int(d[np.argmax(large >= k)]) for k in range(max_exact + 1, NUM_BUCKETS)]


def _t5_bucket(dist):
    max_exact = NUM_BUCKETS // 2
    d = jnp.maximum(dist, 0)
    large = jnp.full(d.shape, max_exact, I32)
    for first in _t5_large_thresholds():
        large = large + (d >= first).astype(I32)
    return jnp.where(d < max_exact, d, large)


def _t5_table_kernel(rb_ref, tab_ref, *, n_heads):
    ts = tab_ref.shape[2]
    rows = 16
    far = _t5_large_thresholds()[-1]
    t_rel = lax.broadcasted_iota(I32, (rows, ts), 1)
    s_rel = lax.broadcasted_iota(I32, (rows, ts), 0)

    for i in range(3):
        n_far = max(0, min(ts, i * ts - far + 1)) // rows

        def fill(c, _, i=i):
            r0 = pl.multiple_of(c * rows, rows)
            for h in range(n_heads):
                tab_ref[i, h, pl.ds(r0, rows), :] = jnp.full((rows, ts), rb_ref[NUM_BUCKETS - 1, h], F32)
            return 0

        def compute(c, _, i=i):
            r0 = pl.multiple_of(c * rows, rows)
            bucket = _t5_bucket(t_rel - (r0 + s_rel) + i * ts)
            acc = [jnp.zeros((rows, ts), F32) for _ in range(n_heads)]
            for j in range(NUM_BUCKETS):
                hit = bucket == j
                acc = [jnp.where(hit, rb_ref[j, h], acc[h]) for h in range(n_heads)]
            for h in range(n_heads):
                tab_ref[i, h, pl.ds(r0, rows), :] = acc[h]
            return 0

        lax.fori_loop(0, n_far, fill, 0)
        lax.fori_loop(n_far, ts // rows, compute, 0)


def _t5_table(rel_bias, ts):
    n_heads = rel_bias.shape[1]
    return pl.pallas_call(
        functools.partial(_t5_table_kernel, n_heads=n_heads),
        in_specs=[pl.BlockSpec(memory_space=pltpu.SMEM)],
        out_specs=pl.BlockSpec(memory_space=pltpu.VMEM),
        out_shape=jax.ShapeDtypeStruct((3, n_heads, ts, ts), F32),
        compiler_params=pltpu.CompilerParams(vmem_limit_bytes=VMEM_LIMIT_BYTES),
        name="t5_table",
    )(rel_bias)


def _dsa_kernel(ki_ref, qin_ref, witn_ref, ka_ref, vat_ref, qam_ref, ga_ref, tab_ref, o_ref,
                sc_ref, mm_ref, madd_ref, m_ref, l_ref, acc_ref, lg_ref, bmax_ref, *, topk, n_heads):
    tq = o_ref.shape[1]
    ts = tq
    q_blk = pl.program_id(1)
    nblk = q_blk + 1
    cur = q_blk % 2
    nxt = 1 - cur
    s_rel = lax.broadcasted_iota(I32, (ts, tq), 0)
    t_rel = lax.broadcasted_iota(I32, (ts, tq), 1)
    t_idx = q_blk * tq + t_rel

    def block_start(j):
        return pl.multiple_of(j * ts, ts)

    all_selected = nblk * tq <= topk
    n_sel = jnp.where(all_selected, 0, nblk)

    @pl.when(all_selected)
    def _():
        mm_ref[cur] = jnp.zeros(mm_ref.shape[1:], F32)

        def body(j, _):
            s0 = block_start(j)
            madd_ref[pl.ds(s0, ts), :] = jnp.where(s0 + s_rel > t_idx, NEG_BIG, 0.0)
            return 0

        lax.fori_loop(0, nblk, body, 0)

    vmin = jnp.min(mm_ref[cur, 0:8, :], axis=0, keepdims=True)
    vmax = jnp.max(mm_ref[cur, 8:16, :], axis=0, keepdims=True)

    def count(pred_fn):
        def body(j, c):
            s0 = block_start(j)
            hit = pred_fn(sc_ref[cur, pl.ds(s0, ts), :], s0).reshape(ts // 8, 8, tq)
            c = list(c)
            for r in range(ts // 8):
                c[r % 4] = jnp.where(hit[r], c[r % 4] + 1, c[r % 4])
            return tuple(c)
        z = jnp.zeros((8, tq), I32)
        c8 = lax.fori_loop(0, n_sel, body, (z, z, z, z))
        return jnp.sum(c8[0] + c8[1] + c8[2] + c8[3], axis=0, keepdims=True)

    def midpoint(lo, hi):
        return 0.5 * lo + 0.5 * hi

    def searching(lo, hi, cnt_lo):
        mid = midpoint(lo, hi)
        return (cnt_lo > topk) & (mid > lo) & (mid < hi)

    def any_lane(flag):
        return jnp.max(flag.astype(I32)) > 0

    n_causal = t_idx[0:1, :] + 1
    lo0 = jnp.where(n_causal > topk, vmin, -jnp.inf)
    hi0 = vmax + (jnp.abs(vmax) * 2.0 ** -20 + 1e-30)

    def bisect_cond(carry):
        it, _, _, _, live = carry
        return live & (it < BISECT_CAP)

    def bisect_step(_, carry):
        lo, hi, cnt_lo = carry
        go = searching(lo, hi, cnt_lo)
        mid = midpoint(lo, hi)
        cnt = count(lambda blk, s0: blk >= mid)
        take = go & (cnt >= topk)
        return (jnp.where(take, mid, lo), jnp.where(go & (cnt < topk), mid, hi),
                jnp.where(take, cnt, cnt_lo))

    def bisect_body(carry):
        it, lo, hi, cnt_lo, _ = carry
        lo, hi, cnt_lo = lax.fori_loop(0, BISECT_GROUP, bisect_step, (lo, hi, cnt_lo))
        return it + BISECT_GROUP, lo, hi, cnt_lo, any_lane(searching(lo, hi, cnt_lo))

    n_blind = jnp.where(all_selected, 0, BISECT_BLIND)
    lo, hi, cnt_lo = lax.fori_loop(0, n_blind, bisect_step, (lo0, hi0, n_causal))
    _, thr, _, cnt_thr, _ = lax.while_loop(
        bisect_cond, bisect_body,
        (jnp.int32(BISECT_BLIND), lo, hi, cnt_lo,
         jnp.logical_not(all_selected) & any_lane(searching(lo, hi, cnt_lo))))

    tie = jnp.logical_not(all_selected) & any_lane(cnt_thr > topk)

    @pl.when(jnp.logical_not(tie))
    def _():
        def body(j, _):
            s0 = block_start(j)
            sel = (sc_ref[cur, pl.ds(s0, ts), :] >= thr) & (s0 + s_rel <= t_idx)
            madd_ref[pl.ds(s0, ts), :] = jnp.where(sel, 0.0, NEG_BIG)
            return 0

        lax.fori_loop(0, n_sel, body, 0)

    @pl.when(tie)
    def _():
        need = topk - count(lambda blk, s0: blk > thr)
        n_bits = sc_ref.shape[1].bit_length()

        def jbody(i, bound):
            cand = bound + jnp.left_shift(jnp.int32(1), n_bits - 1 - i)
            cnt = count(lambda blk, s0: (blk == thr) & (s0 + s_rel < cand))
            return jnp.where(cnt <= need, cand, bound)

        bound = lax.fori_loop(0, n_bits, jbody, jnp.zeros((1, tq), I32))

        def body(j, _):
            s0 = block_start(j)
            blk = sc_ref[cur, pl.ds(s0, ts), :]
            sel = ((blk > thr) | ((blk == thr) & (s0 + s_rel < bound))) & (s0 + s_rel <= t_idx)
            madd_ref[pl.ds(s0, ts), :] = jnp.where(sel, 0.0, NEG_BIG)
            return 0

        lax.fori_loop(0, n_sel, body, 0)

    m_ref[...] = jnp.full(m_ref.shape, NEG_BIG, F32)
    l_ref[...] = jnp.zeros(l_ref.shape, F32)
    acc_ref[...] = jnp.zeros(acc_ref.shape, F32)

    def attend(j):
        s0 = block_start(j)
        madd = madd_ref[pl.ds(s0, ts), :]
        near = jnp.minimum(q_blk - j, 2)
        for h in range(n_heads):
            lg = lax.dot_general(ka_ref[0, pl.ds(s0, ts), _pair(h)], qam_ref[0, h], _NT,
                                 preferred_element_type=F32)
            lg = (lg + tab_ref[near, h] + madd) * LOG2E
            lg_ref[h] = lg
            bmax_ref[h] = jnp.max(lg, axis=0, keepdims=True)
        for h in range(n_heads):
            rows = slice(h * HEAD_DIM, (h + 1) * HEAD_DIM)
            m_old = m_ref[h]
            m_new = jnp.maximum(m_old, bmax_ref[h])
            alpha = jnp.exp2(m_old - m_new)
            p = jnp.exp2(lg_ref[h] - m_new)
            l_ref[h] = alpha * l_ref[h] + jnp.sum(p, axis=0, keepdims=True)
            pv = jnp.dot(vat_ref[0, j, rows, :], p.astype(BF16), preferred_element_type=F32)
            acc_ref[rows, :] = alpha * acc_ref[rows, :] + pv
            m_ref[h] = m_new

    def index_next(j, carry, diagonal):
        vmin8, vmax8 = carry
        s0 = block_start(j)
        kblk = ki_ref[0, pl.ds(s0, ts), :]
        score = jnp.zeros((ts, tq), F32)
        for h in range(IDX_HEADS):
            dots = lax.dot_general(kblk, qin_ref[0, h], _NT, preferred_element_type=F32)
            score = score + jnp.maximum(dots, 0.0) * witn_ref[0, h:h + 1, :]
        if diagonal:
            future = s_rel > t_rel
            low = jnp.where(future, jnp.inf, score)
            score = jnp.where(future, -jnp.inf, score)
        else:
            low = score
        sc_ref[nxt, pl.ds(s0, ts), :] = score
        return (jnp.minimum(vmin8, low.reshape(ts // 8, 8, tq).min(axis=0)),
                jnp.maximum(vmax8, score.reshape(ts // 8, 8, tq).max(axis=0)))

    has_next = q_blk + 1 < pl.num_programs(1)

    @pl.when(has_next)
    def _():
        def body(j, carry):
            attend(j)
            return index_next(j, carry, diagonal=False)

        carry = lax.fori_loop(0, nblk, body, (jnp.full((8, tq), jnp.inf, F32),
                                              jnp.full((8, tq), -jnp.inf, F32)))
        vmin8, vmax8 = index_next(nblk, carry, diagonal=True)
        mm_ref[nxt, 0:8, :] = vmin8
        mm_ref[nxt, 8:16, :] = vmax8

    @pl.when(jnp.logical_not(has_next))
    def _():
        def body(j, _):
            attend(j)
            return 0

        lax.fori_loop(0, nblk, body, 0)

    for h in range(n_heads):
        rows = slice(h * HEAD_DIM, (h + 1) * HEAD_DIM)
        acc_ref[rows, :] = acc_ref[rows, :] * (1.0 / l_ref[h])
    o_ref[0] = (acc_ref[...].T * ga_ref[0]).astype(BF16)


def _dsa(ki, qim, wit, ka, vat, qam, ga, tab, topk):
    b, l, d_a = ka.shape
    n_heads = d_a // HEAD_DIM
    tq = ATT_TILE
    nq = l // tq
    assert tq <= topk, "the first query tile must not need indexer scores"
    next_tile = lambda j: jnp.minimum(j + 1, nq - 1)
    kernel = functools.partial(_dsa_kernel, topk=topk, n_heads=n_heads)
    return pl.pallas_call(
        kernel,
        grid=(b, nq),
        in_specs=[
            pl.BlockSpec((1, l, LANES), lambda i, j: (i, 0, 0)),
            pl.BlockSpec((1, IDX_HEADS, tq, LANES), lambda i, j: (i, 0, next_tile(j), 0)),
            pl.BlockSpec((1, IDX_HEADS, tq), lambda i, j: (i, 0, next_tile(j))),
            pl.BlockSpec((1, l, d_a), lambda i, j: (i, 0, 0)),
            pl.BlockSpec((1, nq, d_a, tq), lambda i, j: (i, 0, 0, 0)),
            pl.BlockSpec((1, n_heads, tq, LANES), lambda i, j: (i, 0, j, 0)),
            pl.BlockSpec((1, tq, d_a), lambda i, j: (i, j, 0)),
            pl.BlockSpec(tab.shape, lambda i, j: (0, 0, 0, 0), pipeline_mode=pl.Buffered(1)),
        ],
        out_specs=pl.BlockSpec((1, tq, d_a), lambda i, j: (i, j, 0)),
        out_shape=jax.ShapeDtypeStruct((b, l, d_a), BF16),
        scratch_shapes=[
            pltpu.VMEM((2, l, tq), F32),
            pltpu.VMEM((2, 16, tq), F32),
            pltpu.VMEM((l, tq), F32),
            pltpu.VMEM((n_heads, 1, tq), F32),
            pltpu.VMEM((n_heads, 1, tq), F32),
            pltpu.VMEM((d_a, tq), F32),
            pltpu.VMEM((n_heads, tq, tq), F32),
            pltpu.VMEM((n_heads, 1, tq), F32),
        ],
        compiler_params=_params("arbitrary", "arbitrary"),
        name="dsa",
    )(ki, qim, wit, ka, vat, qam, ga, tab)


def _stick_kernel(kb_ref, vbt_ref, qbm_ref, gb_ref, u_ref, x_ref, oa_ref, wa_ref, wb_ref, y_ref,
                  below_ref, acc_ref, z_ref, lb_ref, *, n_heads):
    tq = y_ref.shape[1]
    ts = tq
    q_blk = pl.program_id(1)
    s_rel = lax.broadcasted_iota(I32, (ts, tq), 0)
    t_rel = lax.broadcasted_iota(I32, (ts, tq), 1)

    below_ref[...] = jnp.zeros(below_ref.shape, F32)
    acc_ref[...] = jnp.zeros(acc_ref.shape, F32)

    def block(j, diagonal):
        s0 = pl.multiple_of(j * ts, ts)
        strict = s_rel < t_rel

        def z_dot(h):
            return lax.dot_general(kb_ref[0, pl.ds(s0, ts), _pair(h)], qbm_ref[0, h], _NT,
                                   preferred_element_type=F32)

        def terms(z):
            z2 = z * LOG2E
            log_beta = jnp.minimum(z2, 0.0) - jnp.log2(1.0 + jnp.exp2(-jnp.abs(z2)))
            log_om = log_beta - z2
            if diagonal:
                log_om = jnp.where(strict, log_om, 0.0)
            suffix = jnp.dot(u_ref[...], log_om.astype(BF16), preferred_element_type=F32)
            return log_beta, log_om[0:1, :], suffix

        def finish(h, log_a, om_total):
            rows = slice(h * HEAD_DIM, (h + 1) * HEAD_DIM)
            below = below_ref[h]
            a = jnp.exp2(log_a)
            if diagonal:
                a = jnp.where(strict, a, 0.0)
            pv = jnp.dot(vbt_ref[0, j, rows, :], a.astype(BF16), preferred_element_type=F32)
            acc_ref[rows, :] += pv * jnp.exp2(below)
            below = below + om_total
            below_ref[h] = below
            return below

        for h in range(n_heads):
            z_ref[h] = z_dot(h)
        om0 = []
        for h in range(n_heads):
            log_beta, om_row, suffix = terms(z_ref[h])
            lb_ref[h] = log_beta + suffix
            om0.append(om_row + suffix[0:1, :])
        worst = None
        for h in range(n_heads):
            below = finish(h, lb_ref[h], om0[h])
            worst = below if worst is None else jnp.maximum(worst, below)
        return jnp.max(worst)

    top = block(q_blk, diagonal=True)

    def cond(carry):
        i, top = carry
        return (i <= q_blk) & (top >= EXP2_UNDERFLOW)

    def body(carry):
        i, _ = carry
        return i + 1, block(q_blk - i, diagonal=False)

    lax.while_loop(cond, body, (jnp.int32(1), top))

    ob = (acc_ref[...].T * gb_ref[0]).astype(BF16)
    y_ref[0] = (x_ref[0]
                + jnp.dot(oa_ref[0], wa_ref[...], preferred_element_type=F32)
                + jnp.dot(ob, wb_ref[...], preferred_element_type=F32))


def _stick(kb, vbt, qbm, gb, u2, x, oa, wa, wb):
    b, l, d_b = kb.shape
    d = x.shape[2]
    n_heads = d_b // HEAD_DIM
    tq = ATT_TILE
    nq = l // tq
    return pl.pallas_call(
        functools.partial(_stick_kernel, n_heads=n_heads),
        grid=(b, nq),
        in_specs=[
            pl.BlockSpec((1, l, d_b), lambda i, j: (i, 0, 0)),
            pl.BlockSpec((1, nq, d_b, tq), lambda i, j: (i, 0, 0, 0)),
            pl.BlockSpec((1, n_heads, tq, LANES), lambda i, j: (i, 0, j, 0)),
            pl.BlockSpec((1, tq, d_b), lambda i, j: (i, j, 0)),
            pl.BlockSpec(u2.shape, lambda i, j: (0, 0)),
            pl.BlockSpec((1, tq, d), lambda i, j: (i, j, 0)),
            pl.BlockSpec((1, tq, oa.shape[2]), lambda i, j: (i, j, 0)),
            pl.BlockSpec(wa.shape, lambda i, j: (0, 0)),
            pl.BlockSpec(wb.shape, lambda i, j: (0, 0)),
        ],
        out_specs=pl.BlockSpec((1, tq, d), lambda i, j: (i, j, 0)),
        out_shape=jax.ShapeDtypeStruct((b, l, d), F32),
        scratch_shapes=[
            pltpu.VMEM((n_heads, 1, tq), F32),
            pltpu.VMEM((d_b, tq), F32),
            pltpu.VMEM((n_heads, tq, tq), F32),
            pltpu.VMEM((n_heads, tq, tq), F32),
        ],
        compiler_params=_params("arbitrary", "arbitrary"),
        name="stick",
    )(kb, vbt, qbm, gb, u2, x, oa, wa, wb)


def kernel(x, norm_gain, w_in, q_norm_gain, k_norm_gain, rel_bias, w_out):
    b, l, d = x.shape
    depth = w_in.shape[0]
    d_a = d // 2
    d_b = d - d_a
    h_a = d_a // HEAD_DIM
    d_qi = IDX_HEADS * IDX_DIM
    topk = min(TOPK_MAX, l // 4)
    ts = ATT_TILE
    scale = HEAD_DIM ** -0.5
    idx_scale = (IDX_HEADS * IDX_DIM) ** -0.5

    lane = jnp.arange(LANES)
    gsum = (lane[:, None] // HEAD_DIM == lane[None, :] // HEAD_DIM).astype(BF16)
    u2 = (jnp.arange(ts)[None, :] > jnp.arange(ts)[:, None]).astype(BF16)
    tab = _t5_table(rel_bias.astype(F32), ts)

    for layer in range(depth):
        w_all = _pack_w_in(jnp.swapaxes(w_in[layer], 0, 1), d_a, d_b, d_qi, scale, idx_scale)
        qg = jnp.tile(q_norm_gain[layer] * scale, h_a)[None, :]
        kg = jnp.tile(k_norm_gain[layer], h_a)[None, :]

        (qam, ka, vat, ga, qim, ki, wit, qbm, kb, vbt, gb) = _inproj(
            x, norm_gain[layer][None, :], w_all, qg, kg, gsum, d_a, d_b, d_qi)
        oa = _dsa(ki, qim, wit, ka, vat, qam, ga, tab, topk)
        w_o = w_out[layer].astype(BF16)
        x = _stick(kb, vbt, qbm, gb, u2, x, oa, w_o[:d_a], w_o[d_a:])
    return x
```

```python
import functools
import math

import jax
import jax.numpy as jnp
import numpy as np
from jax import lax
from jax.experimental import pallas as pl
from jax.experimental.pallas import tpu as pltpu

F32 = jnp.float32
BF16 = jnp.bfloat16
I32 = jnp.int32

HEAD_DIM = 64
IDX_HEADS = 16
IDX_DIM = 64
TOPK_MAX = 256
NUM_BUCKETS = 32
MAX_DISTANCE = 128
RMS_EPS = 1e-6

LANES = 128
NEG_BIG = -1e30
EXP2_UNDERFLOW = -150.0
LOG2E = math.log2(math.e)
BISECT_CAP = 300
BISECT_BLIND = 20
BISECT_GROUP = 2

VMEM_LIMIT_BYTES = 56 * 1024 * 1024
TOKEN_TILE = 512
ATT_TILE = 256

_NT = (((1,), (1,)), ((), ()))


def _params(*sem):
    return pltpu.CompilerParams(dimension_semantics=sem, vmem_limit_bytes=VMEM_LIMIT_BYTES)


def _pair(h):
    return slice((h // 2) * LANES, (h // 2 + 1) * LANES)


def _store_head_masked(dst_ref, src):
    rows = src.shape[0]
    low = lax.broadcasted_iota(I32, (rows, LANES), 1) < HEAD_DIM
    for h in range(dst_ref.shape[1]):
        keep = low if h % 2 == 0 else jnp.logical_not(low)
        dst_ref[0, h] = jnp.where(keep, src[:, _pair(h)], 0.0).astype(BF16)


def _pack_w_in_kernel(wt_ref, o_ref, *, d_a, d_b, d_qi, scale, idx_scale):
    o_ki = 4 * d_a + d_qi
    o_wi = o_ki + IDX_DIM
    o_b = o_wi + IDX_HEADS
    d = wt_ref.shape[1]

    def put(dst, rows):
        o_ref[:, dst:dst + LANES] = rows.T.astype(BF16)

    for c in range(0, o_ki, LANES):
        put(c, wt_ref[c:c + LANES, :])
    k_idx = wt_ref[o_ki:o_wi, :]
    put(o_ki, jnp.concatenate([k_idx, k_idx], axis=0))
    w_idx = wt_ref[o_wi:o_b, :] * idx_scale
    put(o_ki + LANES, jnp.concatenate([w_idx, jnp.zeros((LANES - IDX_HEADS, d), F32)], axis=0))
    for c in range(0, 4 * d_b, LANES):
        rows = wt_ref[o_b + c:o_b + c + LANES, :]
        put(o_ki + 2 * LANES + c, rows * scale if c < d_b else rows)


def _pack_w_in(w_t, d_a, d_b, d_qi, scale, idx_scale):
    p, d = w_t.shape
    width = p + 2 * LANES - IDX_DIM - IDX_HEADS
    return pl.pallas_call(
        functools.partial(_pack_w_in_kernel, d_a=d_a, d_b=d_b, d_qi=d_qi, scale=scale,
                          idx_scale=idx_scale),
        out_shape=jax.ShapeDtypeStruct((d, width), BF16),
        compiler_params=pltpu.CompilerParams(vmem_limit_bytes=VMEM_LIMIT_BYTES),
        name="pack_w_in",
    )(w_t)


def _inproj_kernel(x_ref, gain_ref, w_ref, qg_ref, kg_ref, gsum_ref,
                   qa_ref, ka_ref, vat_ref, ga_ref, qi_ref, ki_ref, wit_ref,
                   qb_ref, kb_ref, vbt_ref, gb_ref, *, d_a, d_b, d_qi):
    x = x_ref[0]
    ms = jnp.mean(x * x, axis=-1, keepdims=True)
    h = (x * lax.rsqrt(ms + RMS_EPS) * gain_ref[...]).astype(BF16)
    ts = vat_ref.shape[3]

    def proj(c0, width):
        return jnp.dot(h, w_ref[:, c0:c0 + width], preferred_element_type=F32)

    def head_norm(y, g):
        sq = (y * y).astype(BF16)
        ones = gsum_ref[...]
        ssum = jnp.concatenate(
            [jnp.dot(sq[:, c:c + LANES], ones, preferred_element_type=F32)
             for c in range(0, y.shape[1], LANES)], axis=1)
        return y * lax.rsqrt(ssum * (1.0 / HEAD_DIM) + RMS_EPS) * g

    def silu(g):
        return g * (1.0 / (1.0 + jnp.exp(-g)))

    def store_key_blocks_t(dst_ref, v):
        vt = v.T.astype(BF16)
        for c in range(dst_ref.shape[1]):
            dst_ref[0, c] = vt[:, c * ts:(c + 1) * ts]

    c = 0
    _store_head_masked(qa_ref, head_norm(proj(c, d_a), qg_ref[...])); c += d_a
    ka_ref[0] = head_norm(proj(c, d_a), kg_ref[...]).astype(BF16); c += d_a
    store_key_blocks_t(vat_ref, proj(c, d_a)); c += d_a
    ga_ref[0] = silu(proj(c, d_a)); c += d_a
    _store_head_masked(qi_ref, proj(c, d_qi)); c += d_qi
    ki_ref[0] = proj(c, LANES).astype(BF16); c += LANES
    wit_ref[0] = proj(c, LANES).T[:IDX_HEADS, :]; c += LANES
    _store_head_masked(qb_ref, proj(c, d_b)); c += d_b
    kb_ref[0] = proj(c, d_b).astype(BF16); c += d_b
    store_key_blocks_t(vbt_ref, proj(c, d_b)); c += d_b
    gb_ref[0] = silu(proj(c, d_b))


def _inproj(x, gain, w_all, qg, kg, gsum, d_a, d_b, d_qi):
    b, l, d = x.shape
    tm = TOKEN_TILE
    ts = ATT_TILE
    nb = l // ts
    row = lambda width: pl.BlockSpec((1, tm, width), lambda i, j: (i, j, 0))
    const = lambda shape: pl.BlockSpec(shape, lambda i, j: (0, 0))
    vt_spec = lambda ch: pl.BlockSpec((1, tm // ts, ch, ts), lambda i, j: (i, j, 0, 0))
    tok = lambda width, dt: jax.ShapeDtypeStruct((b, l, width), dt)
    heads = lambda width: jax.ShapeDtypeStruct((b, width // HEAD_DIM, l, LANES), BF16)
    heads_spec = lambda width: pl.BlockSpec((1, width // HEAD_DIM, tm, LANES), lambda i, j: (i, 0, j, 0))
    out_shapes = [
        heads(d_a), tok(d_a, BF16), jax.ShapeDtypeStruct((b, nb, d_a, ts), BF16), tok(d_a, F32),
        heads(d_qi), tok(LANES, BF16), jax.ShapeDtypeStruct((b, IDX_HEADS, l), F32),
        heads(d_b), tok(d_b, BF16), jax.ShapeDtypeStruct((b, nb, d_b, ts), BF16), tok(d_b, F32),
    ]
    out_specs = [heads_spec(d_a), row(d_a), vt_spec(d_a), row(d_a), heads_spec(d_qi), row(LANES),
                 pl.BlockSpec((1, IDX_HEADS, tm), lambda i, j: (i, 0, j)),
                 heads_spec(d_b), row(d_b), vt_spec(d_b), row(d_b)]
    return pl.pallas_call(
        functools.partial(_inproj_kernel, d_a=d_a, d_b=d_b, d_qi=d_qi),
        grid=(b, l // tm),
        in_specs=[row(d), const(gain.shape),
                  pl.BlockSpec(w_all.shape, lambda i, j: (0, 0), pipeline_mode=pl.Buffered(1)),
                  const(qg.shape), const(kg.shape), const(gsum.shape)],
        out_specs=out_specs,
        out_shape=out_shapes,
        compiler_params=_params("arbitrary", "arbitrary"),
        name="inproj",
    )(x, gain, w_all, qg, kg, gsum)


def _t5_large_thresholds():
    max_exact = NUM_BUCKETS // 2
    d = np.arange(max_exact, 2 * MAX_DISTANCE + 1)
    large = max_exact + (np.log(d.astype(np.float32) / np.float32(max_exact))
                         / np.float32(math.log(MAX_DISTANCE / max_exact))
                         * np.float32(NUM_BUCKETS - max_exact)).astype(np.int32)
    large = np.minimum(large, NUM_BUCKETS - 1)
    return [int(d[np.argmax(large >= k)]) for k in range(max_exact + 1, NUM_BUCKETS)]


def _t5_bucket(dist):
    max_exact = NUM_BUCKETS // 2
    d = jnp.maximum(dist, 0)
    large = jnp.full(d.shape, max_exact, I32)
    for first in _t5_large_thresholds():
        large = large + (d >= first).astype(I32)
    return jnp.where(d < max_exact, d, large)


def _t5_table_kernel(rb_ref, tab_ref, *, n_heads):
    ts = tab_ref.shape[2]
    rows = 16
    far = _t5_large_thresholds()[-1]
    t_rel = lax.broadcasted_iota(I32, (rows, ts), 1)
    s_rel = lax.broadcasted_iota(I32, (rows, ts), 0)

    for i in range(3):
        n_far = max(0, min(ts, i * ts - far + 1)) // rows

        def fill(c, _, i=i):
            r0 = pl.multiple_of(c * rows, rows)
            for h in range(n_heads):
                tab_ref[i, h, pl.ds(r0, rows), :] = jnp.full((rows, ts), rb_ref[NUM_BUCKETS - 1, h], F32)
            return 0

        def compute(c, _, i=i):
            r0 = pl.multiple_of(c * rows, rows)
            bucket = _t5_bucket(t_rel - (r0 + s_rel) + i * ts)
            acc = [jnp.zeros((rows, ts), F32) for _ in range(n_heads)]
            for j in range(NUM_BUCKETS):
                hit = bucket == j
                acc = [jnp.where(hit, rb_ref[j, h], acc[h]) for h in range(n_heads)]
            for h in range(n_heads):
                tab_ref[i, h, pl.ds(r0, rows), :] = acc[h]
            return 0

        lax.fori_loop(0, n_far, fill, 0)
        lax.fori_loop(n_far, ts // rows, compute, 0)


def _t5_table(rel_bias, ts):
    n_heads = rel_bias.shape[1]
    return pl.pallas_call(
        functools.partial(_t5_table_kernel, n_heads=n_heads),
        in_specs=[pl.BlockSpec(memory_space=pltpu.SMEM)],
        out_specs=pl.BlockSpec(memory_space=pltpu.VMEM),
        out_shape=jax.ShapeDtypeStruct((3, n_heads, ts, ts), F32),
        compiler_params=pltpu.CompilerParams(vmem_limit_bytes=VMEM_LIMIT_BYTES),
        name="t5_table",
    )(rel_bias)


def _dsa_kernel(ki_ref, qin_ref, witn_ref, ka_ref, vat_ref, qam_ref, ga_ref, tab_ref, o_ref,
                sc_ref, mm_ref, madd_ref, m_ref, l_ref, acc_ref, lg_ref, bmax_ref, *, topk, n_heads):
    tq = o_ref.shape[1]
    ts = tq
    q_blk = pl.program_id(1)
    nblk = q_blk + 1
    cur = q_blk % 2
    nxt = 1 - cur
    s_rel = lax.broadcasted_iota(I32, (ts, tq), 0)
    t_rel = lax.broadcasted_iota(I32, (ts, tq), 1)
    t_idx = q_blk * tq + t_rel

    def block_start(j):
        return pl.multiple_of(j * ts, ts)

    all_selected = nblk * tq <= topk
    n_sel = jnp.where(all_selected, 0, nblk)

    @pl.when(all_selected)
    def _():
        mm_ref[cur] = jnp.zeros(mm_ref.shape[1:], F32)

        def body(j, _):
            s0 = block_start(j)
            madd_ref[pl.ds(s0, ts), :] = jnp.where(s0 + s_rel > t_idx, NEG_BIG, 0.0)
            return 0

        lax.fori_loop(0, nblk, body, 0)

    vmin = jnp.min(mm_ref[cur, 0:8, :], axis=0, keepdims=True)
    vmax = jnp.max(mm_ref[cur, 8:16, :], axis=0, keepdims=True)

    def count(pred_fn):
        def body(j, c):
            s0 = block_start(j)
            hit = pred_fn(sc_ref[cur, pl.ds(s0, ts), :], s0).reshape(ts // 8, 8, tq)
            c = list(c)
            for r in range(ts // 8):
                c[r % 4] = jnp.where(hit[r], c[r % 4] + 1, c[r % 4])
            return tuple(c)
        z = jnp.zeros((8, tq), I32)
        c8 = lax.fori_loop(0, n_sel, body, (z, z, z, z))
        return jnp.sum(c8[0] + c8[1] + c8[2] + c8[3], axis=0, keepdims=True)

    def midpoint(lo, hi):
        return 0.5 * lo + 0.5 * hi

    def searching(lo, hi, cnt_lo):
        mid = midpoint(lo, hi)
        return (cnt_lo > topk) & (mid > lo) & (mid < hi)

    def any_lane(flag):
        return jnp.max(flag.astype(I32)) > 0

    n_causal = t_idx[0:1, :] + 1
    lo0 = jnp.where(n_causal > topk, vmin, -jnp.inf)
    hi0 = vmax + (jnp.abs(vmax) * 2.0 ** -20 + 1e-30)

    def bisect_cond(carry):
        it, _, _, _, live = carry
        return live & (it < BISECT_CAP)

    def bisect_step(_, carry):
        lo, hi, cnt_lo = carry
        go = searching(lo, hi, cnt_lo)
        mid = midpoint(lo, hi)
        cnt = count(lambda blk, s0: blk >= mid)
        take = go & (cnt >= topk)
        return (jnp.where(take, mid, lo), jnp.where(go & (cnt < topk), mid, hi),
                jnp.where(take, cnt, cnt_lo))

    def bisect_body(carry):
        it, lo, hi, cnt_lo, _ = carry
        lo, hi, cnt_lo = lax.fori_loop(0, BISECT_GROUP, bisect_step, (lo, hi, cnt_lo))
        return it + BISECT_GROUP, lo, hi, cnt_lo, any_lane(searching(lo, hi, cnt_lo))

    n_blind = jnp.where(all_selected, 0, BISECT_BLIND)
    lo, hi, cnt_lo = lax.fori_loop(0, n_blind, bisect_step, (lo0, hi0, n_causal))
    _, thr, _, cnt_thr, _ = lax.while_loop(
        bisect_cond, bisect_body,
        (jnp.int32(BISECT_BLIND), lo, hi, cnt_lo,
         jnp.logical_not(all_selected) & any_lane(searching(lo, hi, cnt_lo))))

    tie = jnp.logical_not(all_selected) & any_lane(cnt_thr > topk)

    @pl.when(jnp.logical_not(tie))
    def _():
        def body(j, _):
            s0 = block_start(j)
            sel = (sc_ref[cur, pl.ds(s0, ts), :] >= thr) & (s0 + s_rel <= t_idx)
            madd_ref[pl.ds(s0, ts), :] = jnp.where(sel, 0.0, NEG_BIG)
            return 0

        lax.fori_loop(0, n_sel, body, 0)

    @pl.when(tie)
    def _():
        need = topk - count(lambda blk, s0: blk > thr)
        n_bits = sc_ref.shape[1].bit_length()

        def jbody(i, bound):
            cand = bound + jnp.left_shift(jnp.int32(1), n_bits - 1 - i)
            cnt = count(lambda blk, s0: (blk == thr) & (s0 + s_rel < cand))
            return jnp.where(cnt <= need, cand, bound)

        bound = lax.fori_loop(0, n_bits, jbody, jnp.zeros((1, tq), I32))

        def body(j, _):
            s0 = block_start(j)
            blk = sc_ref[cur, pl.ds(s0, ts), :]
            sel = ((blk > thr) | ((blk == thr) & (s0 + s_rel < bound))) & (s0 + s_rel <= t_idx)
            madd_ref[pl.ds(s0, ts), :] = jnp.where(sel, 0.0, NEG_BIG)
            return 0

        lax.fori_loop(0, n_sel, body, 0)

    m_ref[...] = jnp.full(m_ref.shape, NEG_BIG, F32)
    l_ref[...] = jnp.zeros(l_ref.shape, F32)
    acc_ref[...] = jnp.zeros(acc_ref.shape, F32)

    def attend(j):
        s0 = block_start(j)
        madd = madd_ref[pl.ds(s0, ts), :]
        near = jnp.minimum(q_blk - j, 2)
        for h in range(n_heads):
            lg = lax.dot_general(ka_ref[0, pl.ds(s0, ts), _pair(h)], qam_ref[0, h], _NT,
                                 preferred_element_type=F32)
            lg = (lg + tab_ref[near, h] + madd) * LOG2E
            lg_ref[h] = lg
            bmax_ref[h] = jnp.max(lg, axis=0, keepdims=True)
        for h in range(n_heads):
            rows = slice(h * HEAD_DIM, (h + 1) * HEAD_DIM)
            m_old = m_ref[h]
            m_new = jnp.maximum(m_old, bmax_ref[h])
            alpha = jnp.exp2(m_old - m_new)
            p = jnp.exp2(lg_ref[h] - m_new)
            l_ref[h] = alpha * l_ref[h] + jnp.sum(p, axis=0, keepdims=True)
            pv = jnp.dot(vat_ref[0, j, rows, :], p.astype(BF16), preferred_element_type=F32)
            acc_ref[rows, :] = alpha * acc_ref[rows, :] + pv
            m_ref[h] = m_new

    def index_next(j, carry, diagonal):
        vmin8, vmax8 = carry
        s0 = block_start(j)
        kblk = ki_ref[0, pl.ds(s0, ts), :]
        score = jnp.zeros((ts, tq), F32)
        for h in range(IDX_HEADS):
            dots = lax.dot_general(kblk, qin_ref[0, h], _NT, preferred_element_type=F32)
            score = score + jnp.maximum(dots, 0.0) * witn_ref[0, h:h + 1, :]
        if diagonal:
            future = s_rel > t_rel
            low = jnp.where(future, jnp.inf, score)
            score = jnp.where(future, -jnp.inf, score)
        else:
            low = score
        sc_ref[nxt, pl.ds(s0, ts), :] = score
        return (jnp.minimum(vmin8, low.reshape(ts // 8, 8, tq).min(axis=0)),
                jnp.maximum(vmax8, score.reshape(ts // 8, 8, tq).max(axis=0)))

    has_next = q_blk + 1 < pl.num_programs(1)

    @pl.when(has_next)
    def _():
        def body(j, carry):
            attend(j)
            return index_next(j, carry, diagonal=False)

        carry = lax.fori_loop(0, nblk, body, (jnp.full((8, tq), jnp.inf, F32),
                                              jnp.full((8, tq), -jnp.inf, F32)))
        vmin8, vmax8 = index_next(nblk, carry, diagonal=True)
        mm_ref[nxt, 0:8, :] = vmin8
        mm_ref[nxt, 8:16, :] = vmax8

    @pl.when(jnp.logical_not(has_next))
    def _():
        def body(j, _):
            attend(j)
            return 0

        lax.fori_loop(0, nblk, body, 0)

    for h in range(n_heads):
        rows = slice(h * HEAD_DIM, (h + 1) * HEAD_DIM)
        acc_ref[rows, :] = acc_ref[rows, :] * (1.0 / l_ref[h])
    o_ref[0] = (acc_ref[...].T * ga_ref[0]).astype(BF16)


def _dsa(ki, qim, wit, ka, vat, qam, ga, tab, topk):
    b, l, d_a = ka.shape
    n_heads = d_a // HEAD_DIM
    tq = ATT_TILE
    nq = l // tq
    assert tq <= topk, "the first query tile must not need indexer scores"
    next_tile = lambda j: jnp.minimum(j + 1, nq - 1)
    kernel = functools.partial(_dsa_kernel, topk=topk, n_heads=n_heads)
    return pl.pallas_call(
        kernel,
        grid=(b, nq),
        in_specs=[
            pl.BlockSpec((1, l, LANES), lambda i, j: (i, 0, 0)),
            pl.BlockSpec((1, IDX_HEADS, tq, LANES), lambda i, j: (i, 0, next_tile(j), 0)),
            pl.BlockSpec((1, IDX_HEADS, tq), lambda i, j: (i, 0, next_tile(j))),
            pl.BlockSpec((1, l, d_a), lambda i, j: (i, 0, 0)),
            pl.BlockSpec((1, nq, d_a, tq), lambda i, j: (i, 0, 0, 0)),
            pl.BlockSpec((1, n_heads, tq, LANES), lambda i, j: (i, 0, j, 0)),
            pl.BlockSpec((1, tq, d_a), lambda i, j: (i, j, 0)),
            pl.BlockSpec(tab.shape, lambda i, j: (0, 0, 0, 0), pipeline_mode=pl.Buffered(1)),
        ],
        out_specs=pl.BlockSpec((1, tq, d_a), lambda i, j: (i, j, 0)),
        out_shape=jax.ShapeDtypeStruct((b, l, d_a), BF16),
        scratch_shapes=[
            pltpu.VMEM((2, l, tq), F32),
            pltpu.VMEM((2, 16, tq), F32),
            pltpu.VMEM((l, tq), F32),
            pltpu.VMEM((n_heads, 1, tq), F32),
            pltpu.VMEM((n_heads, 1, tq), F32),
            pltpu.VMEM((d_a, tq), F32),
            pltpu.VMEM((n_heads, tq, tq), F32),
            pltpu.VMEM((n_heads, 1, tq), F32),
        ],
        compiler_params=_params("arbitrary", "arbitrary"),
        name="dsa",
    )(ki, qim, wit, ka, vat, qam, ga, tab)


def _stick_kernel(kb_ref, vbt_ref, qbm_ref, gb_ref, u_ref, x_ref, oa_ref, wa_ref, wb_ref, y_ref,
                  below_ref, acc_ref, z_ref, lb_ref, *, n_heads):
    tq = y_ref.shape[1]
    ts = tq
    q_blk = pl.program_id(1)
    s_rel = lax.broadcasted_iota(I32, (ts, tq), 0)
    t_rel = lax.broadcasted_iota(I32, (ts, tq), 1)

    below_ref[...] = jnp.zeros(below_ref.shape, F32)
    acc_ref[...] = jnp.zeros(acc_ref.shape, F32)

    def block(j, diagonal):
        s0 = pl.multiple_of(j * ts, ts)
        strict = s_rel < t_rel

        def z_dot(h):
            return lax.dot_general(kb_ref[0, pl.ds(s0, ts), _pair(h)], qbm_ref[0, h], _NT,
                                   preferred_element_type=F32)

        def terms(z):
            z2 = z * LOG2E
            log_beta = jnp.minimum(z2, 0.0) - jnp.log2(1.0 + jnp.exp2(-jnp.abs(z2)))
            log_om = log_beta - z2
            if diagonal:
                log_om = jnp.where(strict, log_om, 0.0)
            suffix = jnp.dot(u_ref[...], log_om.astype(BF16), preferred_element_type=F32)
            return log_beta, log_om[0:1, :], suffix

        def finish(h, log_a, om_total):
            rows = slice(h * HEAD_DIM, (h + 1) * HEAD_DIM)
            below = below_ref[h]
            a = jnp.exp2(log_a)
            if diagonal:
                a = jnp.where(strict, a, 0.0)
            pv = jnp.dot(vbt_ref[0, j, rows, :], a.astype(BF16), preferred_element_type=F32)
            acc_ref[rows, :] += pv * jnp.exp2(below)
            below = below + om_total
            below_ref[h] = below
            return below

        for h in range(n_heads):
            z_ref[h] = z_dot(h)
        om0 = []
        for h in range(n_heads):
            log_beta, om_row, suffix = terms(z_ref[h])
            lb_ref[h] = log_beta + suffix
            om0.append(om_row + suffix[0:1, :])
        worst = None
        for h in range(n_heads):
            below = finish(h, lb_ref[h], om0[h])
            worst = below if worst is None else jnp.maximum(worst, below)
        return jnp.max(worst)

    top = block(q_blk, diagonal=True)

    def cond(carry):
        i, top = carry
        return (i <= q_blk) & (top >= EXP2_UNDERFLOW)

    def body(carry):
        i, _ = carry
        return i + 1, block(q_blk - i, diagonal=False)

    lax.while_loop(cond, body, (jnp.int32(1), top))

    ob = (acc_ref[...].T * gb_ref[0]).astype(BF16)
    y_ref[0] = (x_ref[0]
                + jnp.dot(oa_ref[0], wa_ref[...], preferred_element_type=F32)
                + jnp.dot(ob, wb_ref[...], preferred_element_type=F32))


def _stick(kb, vbt, qbm, gb, u2, x, oa, wa, wb):
    b, l, d_b = kb.shape
    d = x.shape[2]
    n_heads = d_b // HEAD_DIM
    tq = ATT_TILE
    nq = l // tq
    return pl.pallas_call(
        functools.partial(_stick_kernel, n_heads=n_heads),
        grid=(b, nq),
        in_specs=[
            pl.BlockSpec((1, l, d_b), lambda i, j: (i, 0, 0)),
            pl.BlockSpec((1, nq, d_b, tq), lambda i, j: (i, 0, 0, 0)),
            pl.BlockSpec((1, n_heads, tq, LANES), lambda i, j: (i, 0, j, 0)),
            pl.BlockSpec((1, tq, d_b), lambda i, j: (i, j, 0)),
            pl.BlockSpec(u2.shape, lambda i, j: (0, 0)),
            pl.BlockSpec((1, tq, d), lambda i, j: (i, j, 0)),
            pl.BlockSpec((1, tq, oa.shape[2]), lambda i, j: (i, j, 0)),
            pl.BlockSpec(wa.shape, lambda i, j: (0, 0)),
            pl.BlockSpec(wb.shape, lambda i, j: (0, 0)),
        ],
        out_specs=pl.BlockSpec((1, tq, d), lambda i, j: (i, j, 0)),
        out_shape=jax.ShapeDtypeStruct((b, l, d), F32),
        scratch_shapes=[
            pltpu.VMEM((n_heads, 1, tq), F32),
            pltpu.VMEM((d_b, tq), F32),
            pltpu.VMEM((n_heads, tq, tq), F32),
            pltpu.VMEM((n_heads, tq, tq), F32),
        ],
        compiler_params=_params("arbitrary", "arbitrary"),
        name="stick",
    )(kb, vbt, qbm, gb, u2, x, oa, wa, wb)


def kernel(x, norm_gain, w_in, q_norm_gain, k_norm_gain, rel_bias, w_out):
    b, l, d = x.shape
    depth = w_in.shape[0]
    d_a = d // 2
    d_b = d - d_a
    h_a = d_a // HEAD_DIM
    d_qi = IDX_HEADS * IDX_DIM
    topk = min(TOPK_MAX, l // 4)
    ts = ATT_TILE
    scale = HEAD_DIM ** -0.5
    idx_scale = (IDX_HEADS * IDX_DIM) ** -0.5

    lane = jnp.arange(LANES)
    gsum = (lane[:, None] // HEAD_DIM == lane[None, :] // HEAD_DIM).astype(BF16)
    u2 = (jnp.arange(ts)[None, :] > jnp.arange(ts)[:, None]).astype(BF16)
    tab = _t5_table(rel_bias.astype(F32), ts)

    for layer in range(depth):
        w_all = _pack_w_in(jnp.swapaxes(w_in[layer], 0, 1), d_a, d_b, d_qi, scale, idx_scale)
        qg = jnp.tile(q_norm_gain[layer] * scale, h_a)[None, :]
        kg = jnp.tile(k_norm_gain[layer], h_a)[None, :]

        (qam, ka, vat, ga, qim, ki, wit, qbm, kb, vbt, gb) = _inproj(
            x, norm_gain[layer][None, :], w_all, qg, kg, gsum, d_a, d_b, d_qi)
        oa = _dsa(ki, qim, wit, ka, vat, qam, ga, tab, topk)
        w_o = w_out[layer].astype(BF16)
        x = _stick(kb, vbt, qbm, gb, u2, x, oa, w_o[:d_a], w_o[d_a:])
    return x
```

```python
import functools
import math

import jax
import jax.numpy as jnp
import numpy as np
from jax import lax
from jax.experimental import pallas as pl
from jax.experimental.pallas import tpu as pltpu

F32 = jnp.float32
BF16 = jnp.bfloat16
I32 = jnp.int32

HEAD_DIM = 64
IDX_HEADS = 16
IDX_DIM = 64
TOPK_MAX = 256
NUM_BUCKETS = 32
MAX_DISTANCE = 128
RMS_EPS = 1e-6

LANES = 128
NEG_BIG = -1e30
EXP2_UNDERFLOW = -150.0
LOG2E = math.log2(math.e)
BISECT_CAP = 300
BISECT_BLIND = 20
BISECT_GROUP = 2

VMEM_LIMIT_BYTES = 56 * 1024 * 1024
TOKEN_TILE = 512
ATT_TILE = 256

_NT = (((1,), (1,)), ((), ()))


def _params(*sem):
    return pltpu.CompilerParams(dimension_semantics=sem, vmem_limit_bytes=VMEM_LIMIT_BYTES)


def _pair(h):
    return slice((h // 2) * LANES, (h // 2 + 1) * LANES)


def _store_head_masked(dst_ref, src):
    rows = src.shape[0]
    low = lax.broadcasted_iota(I32, (rows, LANES), 1) < HEAD_DIM
    for h in range(dst_ref.shape[1]):
        keep = low if h % 2 == 0 else jnp.logical_not(low)
        dst_ref[0, h] = jnp.where(keep, src[:, _pair(h)], 0.0).astype(BF16)


def _pack_w_in_kernel(wt_ref, o_ref, *, d_a, d_b, d_qi, scale, idx_scale):
    o_ki = 4 * d_a + d_qi
    o_wi = o_ki + IDX_DIM
    o_b = o_wi + IDX_HEADS
    d = wt_ref.shape[1]

    def put(dst, rows):
        o_ref[:, dst:dst + LANES] = rows.T.astype(BF16)

    for c in range(0, o_ki, LANES):
        put(c, wt_ref[c:c + LANES, :])
    k_idx = wt_ref[o_ki:o_wi, :]
    put(o_ki, jnp.concatenate([k_idx, k_idx], axis=0))
    w_idx = wt_ref[o_wi:o_b, :] * idx_scale
    put(o_ki + LANES, jnp.concatenate([w_idx, jnp.zeros((LANES - IDX_HEADS, d), F32)], axis=0))
    for c in range(0, 4 * d_b, LANES):
        rows = wt_ref[o_b + c:o_b + c + LANES, :]
        put(o_ki + 2 * LANES + c, rows * scale if c < d_b else rows)


def _pack_w_in(w_t, d_a, d_b, d_qi, scale, idx_scale):
    p, d = w_t.shape
    width = p + 2 * LANES - IDX_DIM - IDX_HEADS
    return pl.pallas_call(
        functools.partial(_pack_w_in_kernel, d_a=d_a, d_b=d_b, d_qi=d_qi, scale=scale,
                          idx_scale=idx_scale),
        out_shape=jax.ShapeDtypeStruct((d, width), BF16),
        compiler_params=pltpu.CompilerParams(vmem_limit_bytes=VMEM_LIMIT_BYTES),
        name="pack_w_in",
    )(w_t)


def _inproj_kernel(x_ref, gain_ref, w_ref, qg_ref, kg_ref, gsum_ref,
                   qa_ref, ka_ref, vat_ref, ga_ref, qi_ref, ki_ref, wit_ref,
                   qb_ref, kb_ref, vbt_ref, gb_ref, *, d_a, d_b, d_qi):
    x = x_ref[0]
    ms = jnp.mean(x * x, axis=-1, keepdims=True)
    h = (x * lax.rsqrt(ms + RMS_EPS) * gain_ref[...]).astype(BF16)
    ts = vat_ref.shape[3]

    def proj(c0, width):
        return jnp.dot(h, w_ref[:, c0:c0 + width], preferred_element_type=F32)

    def head_norm(y, g):
        sq = (y * y).astype(BF16)
        ones = gsum_ref[...]
        ssum = jnp.concatenate(
            [jnp.dot(sq[:, c:c + LANES], ones, preferred_element_type=F32)
             for c in range(0, y.shape[1], LANES)], axis=1)
        return y * lax.rsqrt(ssum * (1.0 / HEAD_DIM) + RMS_EPS) * g

    def silu(g):
        return g * (1.0 / (1.0 + jnp.exp(-g)))

    def store_key_blocks_t(dst_ref, v):
        vt = v.T.astype(BF16)
        for c in range(dst_ref.shape[1]):
            dst_ref[0, c] = vt[:, c * ts:(c + 1) * ts]

    c = 0
    _store_head_masked(qa_ref, head_norm(proj(c, d_a), qg_ref[...])); c += d_a
    ka_ref[0] = head_norm(proj(c, d_a), kg_ref[...]).astype(BF16); c += d_a
    store_key_blocks_t(vat_ref, proj(c, d_a)); c += d_a
    ga_ref[0] = silu(proj(c, d_a)); c += d_a
    _store_head_masked(qi_ref, proj(c, d_qi)); c += d_qi
    ki_ref[0] = proj(c, LANES).astype(BF16); c += LANES
    wit_ref[0] = proj(c, LANES).T[:IDX_HEADS, :]; c += LANES
    _store_head_masked(qb_ref, proj(c, d_b)); c += d_b
    kb_ref[0] = proj(c, d_b).astype(BF16); c += d_b
    store_key_blocks_t(vbt_ref, proj(c, d_b)); c += d_b
    gb_ref[0] = silu(proj(c, d_b))


def _inproj(x, gain, w_all, qg, kg, gsum, d_a, d_b, d_qi):
    b, l, d = x.shape
    tm = TOKEN_TILE
    ts = ATT_TILE
    nb = l // ts
    row = lambda width: pl.BlockSpec((1, tm, width), lambda i, j: (i, j, 0))
    const = lambda shape: pl.BlockSpec(shape, lambda i, j: (0, 0))
    vt_spec = lambda ch: pl.BlockSpec((1, tm // ts, ch, ts), lambda i, j: (i, j, 0, 0))
    tok = lambda width, dt: jax.ShapeDtypeStruct((b, l, width), dt)
    heads = lambda width: jax.ShapeDtypeStruct((b, width // HEAD_DIM, l, LANES), BF16)
    heads_spec = lambda width: pl.BlockSpec((1, width // HEAD_DIM, tm, LANES), lambda i, j: (i, 0, j, 0))
    out_shapes = [
        heads(d_a), tok(d_a, BF16), jax.ShapeDtypeStruct((b, nb, d_a, ts), BF16), tok(d_a, F32),
        heads(d_qi), tok(LANES, BF16), jax.ShapeDtypeStruct((b, IDX_HEADS, l), F32),
        heads(d_b), tok(d_b, BF16), jax.ShapeDtypeStruct((b, nb, d_b, ts), BF16), tok(d_b, F32),
    ]
    out_specs = [heads_spec(d_a), row(d_a), vt_spec(d_a), row(d_a), heads_spec(d_qi), row(LANES),
                 pl.BlockSpec((1, IDX_HEADS, tm), lambda i, j: (i, 0, j)),
                 heads_spec(d_b), row(d_b), vt_spec(d_b), row(d_b)]
    return pl.pallas_call(
        functools.partial(_inproj_kernel, d_a=d_a, d_b=d_b, d_qi=d_qi),
        grid=(b, l // tm),
        in_specs=[row(d), const(gain.shape),
                  pl.BlockSpec(w_all.shape, lambda i, j: (0, 0), pipeline_mode=pl.Buffered(1)),
                  const(qg.shape), const(kg.shape), const(gsum.shape)],
        out_specs=out_specs,
        out_shape=out_shapes,
        compiler_params=_params("arbitrary", "arbitrary"),
        name="inproj",
    )(x, gain, w_all, qg, kg, gsum)


def _t5_large_thresholds():
    max_exact = NUM_BUCKETS // 2
    d = np.arange(max_exact, 2 * MAX_DISTANCE + 1)
    large = max_exact + (np.log(d.astype(np.float32) / np.float32(max_exact))
                         / np.float32(math.log(MAX_DISTANCE / max_exact))
                         * np.float32(NUM_BUCKETS - max_exact)).astype(np.int32)
    large = np.minimum(large, NUM_BUCKETS - 1)
    return [int(d[np.argmax(large >= k)]) for k in range(max_exact + 1, NUM_BUCKETS)]


def _t5_bucket(dist):
    max_exact = NUM_BUCKETS // 2
    d = jnp.maximum(dist, 0)
    large = jnp.full(d.shape, max_exact, I32)
    for first in _t5_large_thresholds():
        large = large + (d >= first).astype(I32)
    return jnp.where(d < max_exact, d, large)


def _t5_table_kernel(rb_ref, tab_ref, *, n_heads):
    ts = tab_ref.shape[2]
    rows = 16
    far = _t5_large_thresholds()[-1]
    t_rel = lax.broadcasted_iota(I32, (rows, ts), 1)
    s_rel = lax.broadcasted_iota(I32, (rows, ts), 0)

    for i in range(3):
        n_far = max(0, min(ts, i * ts - far + 1)) // rows

        def fill(c, _, i=i):
            r0 = pl.multiple_of(c * rows, rows)
            for h in range(n_heads):
                tab_ref[i, h, pl.ds(r0, rows), :] = jnp.full((rows, ts), rb_ref[NUM_BUCKETS - 1, h] * LOG2E, F32)
            return 0

        def compute(c, _, i=i):
            r0 = pl.multiple_of(c * rows, rows)
            bucket = _t5_bucket(t_rel - (r0 + s_rel) + i * ts)
            acc = [jnp.zeros((rows, ts), F32) for _ in range(n_heads)]
            for j in range(NUM_BUCKETS):
                hit = bucket == j
                acc = [jnp.where(hit, rb_ref[j, h] * LOG2E, acc[h]) for h in range(n_heads)]
            for h in range(n_heads):
                tab_ref[i, h, pl.ds(r0, rows), :] = acc[h]
            return 0

        lax.fori_loop(0, n_far, fill, 0)
        lax.fori_loop(n_far, ts // rows, compute, 0)


def _t5_table(rel_bias, ts):
    n_heads = rel_bias.shape[1]
    return pl.pallas_call(
        functools.partial(_t5_table_kernel, n_heads=n_heads),
        in_specs=[pl.BlockSpec(memory_space=pltpu.SMEM)],
        out_specs=pl.BlockSpec(memory_space=pltpu.VMEM),
        out_shape=jax.ShapeDtypeStruct((3, n_heads, ts, ts), F32),
        compiler_params=pltpu.CompilerParams(vmem_limit_bytes=VMEM_LIMIT_BYTES),
        name="t5_table",
    )(rel_bias)


def _dsa_kernel(ki_ref, qin_ref, witn_ref, ka_ref, vat_ref, qam_ref, ga_ref, tab_ref, o_ref,
                sc_ref, mm_ref, madd_ref, m_ref, l_ref, acc_ref, lg_ref, bmax_ref, *, topk, n_heads):
    tq = o_ref.shape[1]
    ts = tq
    q_blk = pl.program_id(1)
    nblk = q_blk + 1
    cur = q_blk % 2
    nxt = 1 - cur
    s_rel = lax.broadcasted_iota(I32, (ts, tq), 0)
    t_rel = lax.broadcasted_iota(I32, (ts, tq), 1)
    t_idx = q_blk * tq + t_rel

    def block_start(j):
        return pl.multiple_of(j * ts, ts)

    all_selected = nblk * tq <= topk
    n_sel = jnp.where(all_selected, 0, nblk)

    @pl.when(all_selected)
    def _():
        mm_ref[cur] = jnp.zeros(mm_ref.shape[1:], F32)

        def body(j, _):
            s0 = block_start(j)
            madd_ref[pl.ds(s0, ts), :] = jnp.where(s0 + s_rel > t_idx, NEG_BIG, 0.0)
            return 0

        lax.fori_loop(0, nblk, body, 0)

    vmin = jnp.min(mm_ref[cur, 0:8, :], axis=0, keepdims=True)
    vmax = jnp.max(mm_ref[cur, 8:16, :], axis=0, keepdims=True)

    def count(pred_fn):
        def body(j, c):
            s0 = block_start(j)
            hit = pred_fn(sc_ref[cur, pl.ds(s0, ts), :], s0).reshape(ts // 8, 8, tq)
            c = list(c)
            for r in range(ts // 8):
                c[r % 4] = jnp.where(hit[r], c[r % 4] + 1, c[r % 4])
            return tuple(c)
        z = jnp.zeros((8, tq), I32)
        c8 = lax.fori_loop(0, n_sel, body, (z, z, z, z))
        return jnp.sum(c8[0] + c8[1] + c8[2] + c8[3], axis=0, keepdims=True)

    def midpoint(lo, hi):
        return 0.5 * lo + 0.5 * hi

    def searching(lo, hi, cnt_lo):
        mid = midpoint(lo, hi)
        return (cnt_lo > topk) & (mid > lo) & (mid < hi)

    def any_lane(flag):
        return jnp.max(flag.astype(I32)) > 0

    n_causal = t_idx[0:1, :] + 1
    lo0 = jnp.where(n_causal > topk, vmin, -jnp.inf)
    hi0 = vmax + (jnp.abs(vmax) * 2.0 ** -20 + 1e-30)

    def bisect_cond(carry):
        it, _, _, _, live = carry
        return live & (it < BISECT_CAP)

    def bisect_step(_, carry):
        lo, hi, cnt_lo = carry
        go = searching(lo, hi, cnt_lo)
        mid = midpoint(lo, hi)
        cnt = count(lambda blk, s0: blk >= mid)
        take = go & (cnt >= topk)
        return (jnp.where(take, mid, lo), jnp.where(go & (cnt < topk), mid, hi),
                jnp.where(take, cnt, cnt_lo))

    def bisect_body(carry):
        it, lo, hi, cnt_lo, _ = carry
        lo, hi, cnt_lo = lax.fori_loop(0, BISECT_GROUP, bisect_step, (lo, hi, cnt_lo))
        return it + BISECT_GROUP, lo, hi, cnt_lo, any_lane(searching(lo, hi, cnt_lo))

    n_blind = jnp.where(all_selected, 0, BISECT_BLIND)
    lo, hi, cnt_lo = lax.fori_loop(0, n_blind, bisect_step, (lo0, hi0, n_causal))
    _, thr, _, cnt_thr, _ = lax.while_loop(
        bisect_cond, bisect_body,
        (jnp.int32(BISECT_BLIND), lo, hi, cnt_lo,
         jnp.logical_not(all_selected) & any_lane(searching(lo, hi, cnt_lo))))

    tie = jnp.logical_not(all_selected) & any_lane(cnt_thr > topk)

    @pl.when(jnp.logical_not(tie))
    def _():
        def body(j, _):
            s0 = block_start(j)
            sel = (sc_ref[cur, pl.ds(s0, ts), :] >= thr) & (s0 + s_rel <= t_idx)
            madd_ref[pl.ds(s0, ts), :] = jnp.where(sel, 0.0, NEG_BIG)
            return 0

        lax.fori_loop(0, n_sel, body, 0)

    @pl.when(tie)
    def _():
        need = topk - count(lambda blk, s0: blk > thr)
        n_bits = sc_ref.shape[1].bit_length()

        def jbody(i, bound):
            cand = bound + jnp.left_shift(jnp.int32(1), n_bits - 1 - i)
            cnt = count(lambda blk, s0: (blk == thr) & (s0 + s_rel < cand))
            return jnp.where(cnt <= need, cand, bound)

        bound = lax.fori_loop(0, n_bits, jbody, jnp.zeros((1, tq), I32))

        def body(j, _):
            s0 = block_start(j)
            blk = sc_ref[cur, pl.ds(s0, ts), :]
            sel = ((blk > thr) | ((blk == thr) & (s0 + s_rel < bound))) & (s0 + s_rel <= t_idx)
            madd_ref[pl.ds(s0, ts), :] = jnp.where(sel, 0.0, NEG_BIG)
            return 0

        lax.fori_loop(0, n_sel, body, 0)

    m_ref[...] = jnp.full(m_ref.shape, NEG_BIG, F32)
    l_ref[...] = jnp.zeros(l_ref.shape, F32)
    acc_ref[...] = jnp.zeros(acc_ref.shape, F32)

    def attend(j):
        s0 = block_start(j)
        madd = madd_ref[pl.ds(s0, ts), :]
        near = jnp.minimum(q_blk - j, 2)
        for h in range(n_heads):
            lg = lax.dot_general(ka_ref[0, pl.ds(s0, ts), _pair(h)], qam_ref[0, h], _NT,
                                 preferred_element_type=F32)
            lg = lg + tab_ref[near, h] + madd
            lg_ref[h] = lg
            bmax_ref[h] = jnp.max(lg, axis=0, keepdims=True)
        for h in range(n_heads):
            rows = slice(h * HEAD_DIM, (h + 1) * HEAD_DIM)
            m_old = m_ref[h]
            m_new = jnp.maximum(m_old, bmax_ref[h])
            alpha = jnp.exp2(m_old - m_new)
            p = jnp.exp2(lg_ref[h] - m_new)
            l_ref[h] = alpha * l_ref[h] + jnp.sum(p, axis=0, keepdims=True)
            pv = jnp.dot(vat_ref[0, j, rows, :], p.astype(BF16), preferred_element_type=F32)
            acc_ref[rows, :] = alpha * acc_ref[rows, :] + pv
            m_ref[h] = m_new

    def index_next(j, carry, diagonal):
        vmin8, vmax8 = carry
        s0 = block_start(j)
        kblk = ki_ref[0, pl.ds(s0, ts), :]
        score = jnp.zeros((ts, tq), F32)
        for h in range(IDX_HEADS):
            dots = lax.dot_general(kblk, qin_ref[0, h], _NT, preferred_element_type=F32)
            score = score + jnp.maximum(dots, 0.0) * witn_ref[0, h:h + 1, :]
        if diagonal:
            future = s_rel > t_rel
            low = jnp.where(future, jnp.inf, score)
            score = jnp.where(future, -jnp.inf, score)
        else:
            low = score
        sc_ref[nxt, pl.ds(s0, ts), :] = score
        return (jnp.minimum(vmin8, low.reshape(ts // 8, 8, tq).min(axis=0)),
                jnp.maximum(vmax8, score.reshape(ts // 8, 8, tq).max(axis=0)))

    has_next = q_blk + 1 < pl.num_programs(1)

    @pl.when(has_next)
    def _():
        def body(j, carry):
            attend(j)
            return index_next(j, carry, diagonal=False)

        carry = lax.fori_loop(0, nblk, body, (jnp.full((8, tq), jnp.inf, F32),
                                              jnp.full((8, tq), -jnp.inf, F32)))
        vmin8, vmax8 = index_next(nblk, carry, diagonal=True)
        mm_ref[nxt, 0:8, :] = vmin8
        mm_ref[nxt, 8:16, :] = vmax8

    @pl.when(jnp.logical_not(has_next))
    def _():
        def body(j, _):
            attend(j)
            return 0

        lax.fori_loop(0, nblk, body, 0)

    for h in range(n_heads):
        rows = slice(h * HEAD_DIM, (h + 1) * HEAD_DIM)
        acc_ref[rows, :] = acc_ref[rows, :] * (1.0 / l_ref[h])
    o_ref[0] = (acc_ref[...].T * ga_ref[0]).astype(BF16)


def _dsa(ki, qim, wit, ka, vat, qam, ga, tab, topk):
    b, l, d_a = ka.shape
    n_heads = d_a // HEAD_DIM
    tq = ATT_TILE
    nq = l // tq
    assert tq <= topk, "the first query tile must not need indexer scores"
    next_tile = lambda j: jnp.minimum(j + 1, nq - 1)
    kernel = functools.partial(_dsa_kernel, topk=topk, n_heads=n_heads)
    return pl.pallas_call(
        kernel,
        grid=(b, nq),
        in_specs=[
            pl.BlockSpec((1, l, LANES), lambda i, j: (i, 0, 0)),
            pl.BlockSpec((1, IDX_HEADS, tq, LANES), lambda i, j: (i, 0, next_tile(j), 0)),
            pl.BlockSpec((1, IDX_HEADS, tq), lambda i, j: (i, 0, next_tile(j))),
            pl.BlockSpec((1, l, d_a), lambda i, j: (i, 0, 0)),
            pl.BlockSpec((1, nq, d_a, tq), lambda i, j: (i, 0, 0, 0)),
            pl.BlockSpec((1, n_heads, tq, LANES), lambda i, j: (i, 0, j, 0)),
            pl.BlockSpec((1, tq, d_a), lambda i, j: (i, j, 0)),
            pl.BlockSpec(tab.shape, lambda i, j: (0, 0, 0, 0), pipeline_mode=pl.Buffered(1)),
        ],
        out_specs=pl.BlockSpec((1, tq, d_a), lambda i, j: (i, j, 0)),
        out_shape=jax.ShapeDtypeStruct((b, l, d_a), BF16),
        scratch_shapes=[
            pltpu.VMEM((2, l, tq), F32),
            pltpu.VMEM((2, 16, tq), F32),
            pltpu.VMEM((l, tq), F32),
            pltpu.VMEM((n_heads, 1, tq), F32),
            pltpu.VMEM((n_heads, 1, tq), F32),
            pltpu.VMEM((d_a, tq), F32),
            pltpu.VMEM((n_heads, tq, tq), F32),
            pltpu.VMEM((n_heads, 1, tq), F32),
        ],
        compiler_params=_params("arbitrary", "arbitrary"),
        name="dsa",
    )(ki, qim, wit, ka, vat, qam, ga, tab)


def _stick_kernel(kb_ref, vbt_ref, qbm_ref, gb_ref, u_ref, x_ref, oa_ref, wa_ref, wb_ref, y_ref,
                  below_ref, acc_ref, z_ref, lb_ref, *, n_heads):
    tq = y_ref.shape[1]
    ts = tq
    q_blk = pl.program_id(1)
    s_rel = lax.broadcasted_iota(I32, (ts, tq), 0)
    t_rel = lax.broadcasted_iota(I32, (ts, tq), 1)

    below_ref[...] = jnp.zeros(below_ref.shape, F32)
    acc_ref[...] = jnp.zeros(acc_ref.shape, F32)

    def block(j, diagonal):
        s0 = pl.multiple_of(j * ts, ts)
        strict = s_rel < t_rel

        def z_dot(h):
            return lax.dot_general(kb_ref[0, pl.ds(s0, ts), _pair(h)], qbm_ref[0, h], _NT,
                                   preferred_element_type=F32)

        def terms(z2):
            log_beta = jnp.minimum(z2, 0.0) - jnp.log2(1.0 + jnp.exp2(-jnp.abs(z2)))
            log_om = log_beta - z2
            if diagonal:
                log_om = jnp.where(strict, log_om, 0.0)
            suffix = jnp.dot(u_ref[...], log_om.astype(BF16), preferred_element_type=F32)
            return log_beta, log_om[0:1, :], suffix

        def finish(h, log_a, om_total):
            rows = slice(h * HEAD_DIM, (h + 1) * HEAD_DIM)
            below = below_ref[h]
            a = jnp.exp2(log_a)
            if diagonal:
                a = jnp.where(strict, a, 0.0)
            pv = jnp.dot(vbt_ref[0, j, rows, :], a.astype(BF16), preferred_element_type=F32)
            acc_ref[rows, :] += pv * jnp.exp2(below)
            below = below + om_total
            below_ref[h] = below
            return below

        for h in range(n_heads):
            z_ref[h] = z_dot(h)
        om0 = []
        for h in range(n_heads):
            log_beta, om_row, suffix = terms(z_ref[h])
            lb_ref[h] = log_beta + suffix
            om0.append(om_row + suffix[0:1, :])
        worst = None
        for h in range(n_heads):
            below = finish(h, lb_ref[h], om0[h])
            worst = below if worst is None else jnp.maximum(worst, below)
        return jnp.max(worst)

    top = block(q_blk, diagonal=True)

    def cond(carry):
        i, top = carry
        return (i <= q_blk) & (top >= EXP2_UNDERFLOW)

    def body(carry):
        i, _ = carry
        return i + 1, block(q_blk - i, diagonal=False)

    lax.while_loop(cond, body, (jnp.int32(1), top))

    ob = (acc_ref[...].T * gb_ref[0]).astype(BF16)
    y_ref[0] = (x_ref[0]
                + jnp.dot(oa_ref[0], wa_ref[...], preferred_element_type=F32)
                + jnp.dot(ob, wb_ref[...], preferred_element_type=F32))


def _stick(kb, vbt, qbm, gb, u2, x, oa, wa, wb):
    b, l, d_b = kb.shape
    d = x.shape[2]
    n_heads = d_b // HEAD_DIM
    tq = ATT_TILE
    nq = l // tq
    return pl.pallas_call(
        functools.partial(_stick_kernel, n_heads=n_heads),
        grid=(b, nq),
        in_specs=[
            pl.BlockSpec((1, l, d_b), lambda i, j: (i, 0, 0)),
            pl.BlockSpec((1, nq, d_b, tq), lambda i, j: (i, 0, 0, 0)),
            pl.BlockSpec((1, n_heads, tq, LANES), lambda i, j: (i, 0, j, 0)),
            pl.BlockSpec((1, tq, d_b), lambda i, j: (i, j, 0)),
            pl.BlockSpec(u2.shape, lambda i, j: (0, 0)),
            pl.BlockSpec((1, tq, d), lambda i, j: (i, j, 0)),
            pl.BlockSpec((1, tq, oa.shape[2]), lambda i, j: (i, j, 0)),
            pl.BlockSpec(wa.shape, lambda i, j: (0, 0)),
            pl.BlockSpec(wb.shape, lambda i, j: (0, 0)),
        ],
        out_specs=pl.BlockSpec((1, tq, d), lambda i, j: (i, j, 0)),
        out_shape=jax.ShapeDtypeStruct((b, l, d), F32),
        scratch_shapes=[
            pltpu.VMEM((n_heads, 1, tq), F32),
            pltpu.VMEM((d_b, tq), F32),
            pltpu.VMEM((n_heads, tq, tq), F32),
            pltpu.VMEM((n_heads, tq, tq), F32),
        ],
        compiler_params=_params("arbitrary", "arbitrary"),
        name="stick",
    )(kb, vbt, qbm, gb, u2, x, oa, wa, wb)


def kernel(x, norm_gain, w_in, q_norm_gain, k_norm_gain, rel_bias, w_out):
    b, l, d = x.shape
    depth = w_in.shape[0]
    d_a = d // 2
    d_b = d - d_a
    h_a = d_a // HEAD_DIM
    d_qi = IDX_HEADS * IDX_DIM
    topk = min(TOPK_MAX, l // 4)
    ts = ATT_TILE
    scale = HEAD_DIM ** -0.5 * LOG2E
    idx_scale = (IDX_HEADS * IDX_DIM) ** -0.5

    lane = jnp.arange(LANES)
    gsum = (lane[:, None] // HEAD_DIM == lane[None, :] // HEAD_DIM).astype(BF16)
    u2 = (jnp.arange(ts)[None, :] > jnp.arange(ts)[:, None]).astype(BF16)
    tab = _t5_table(rel_bias.astype(F32), ts)

    for layer in range(depth):
        w_all = _pack_w_in(jnp.swapaxes(w_in[layer], 0, 1), d_a, d_b, d_qi, scale, idx_scale)
        qg = jnp.tile(q_norm_gain[layer] * scale, h_a)[None, :]
        kg = jnp.tile(k_norm_gain[layer], h_a)[None, :]

        (qam, ka, vat, ga, qim, ki, wit, qbm, kb, vbt, gb) = _inproj(
            x, norm_gain[layer][None, :], w_all, qg, kg, gsum, d_a, d_b, d_qi)
        oa = _dsa(ki, qim, wit, ka, vat, qam, ga, tab, topk)
        w_o = w_out[layer].astype(BF16)
        x = _stick(kb, vbt, qbm, gb, u2, x, oa, w_o[:d_a], w_o[d_a:])
    return x
```

```python
import functools
import math

import jax
import jax.numpy as jnp
import numpy as np
from jax import lax
from jax.experimental import pallas as pl
from jax.experimental.pallas import tpu as pltpu

F32 = jnp.float32
BF16 = jnp.bfloat16
I32 = jnp.int32

HEAD_DIM = 64
IDX_HEADS = 16
IDX_DIM = 64
TOPK_MAX = 256
NUM_BUCKETS = 32
MAX_DISTANCE = 128
RMS_EPS = 1e-6

LANES = 128
NEG_BIG = -1e30
EXP2_UNDERFLOW = -150.0
LOG2E = math.log2(math.e)
BISECT_CAP = 300
BISECT_BLIND = 20
BISECT_GROUP = 2

VMEM_LIMIT_BYTES = 56 * 1024 * 1024
TOKEN_TILE = 512
ATT_TILE = 256

def _params(*sem):
    return pltpu.CompilerParams(dimension_semantics=sem, vmem_limit_bytes=VMEM_LIMIT_BYTES)


def _pair(h):
    return slice((h // 2) * LANES, (h // 2 + 1) * LANES)


def _store_head_masked(dst_ref, src):
    rows = src.shape[0]
    low = lax.broadcasted_iota(I32, (rows, LANES), 1) < HEAD_DIM
    for h in range(dst_ref.shape[1]):
        keep = low if h % 2 == 0 else jnp.logical_not(low)
        dst_ref[0, h] = jnp.where(keep, src[:, _pair(h)], 0.0).T.astype(BF16)


def _pack_w_in_kernel(wt_ref, o_ref, *, d_a, d_b, d_qi, scale, idx_scale):
    o_ki = 4 * d_a + d_qi
    o_wi = o_ki + IDX_DIM
    o_b = o_wi + IDX_HEADS
    d = wt_ref.shape[1]

    def put(dst, rows):
        o_ref[:, dst:dst + LANES] = rows.T.astype(BF16)

    for c in range(0, o_ki, LANES):
        put(c, wt_ref[c:c + LANES, :])
    k_idx = wt_ref[o_ki:o_wi, :]
    put(o_ki, jnp.concatenate([k_idx, k_idx], axis=0))
    w_idx = wt_ref[o_wi:o_b, :] * idx_scale
    put(o_ki + LANES, jnp.concatenate([w_idx, jnp.zeros((LANES - IDX_HEADS, d), F32)], axis=0))
    for c in range(0, 4 * d_b, LANES):
        rows = wt_ref[o_b + c:o_b + c + LANES, :]
        put(o_ki + 2 * LANES + c, rows * scale if c < d_b else rows)


def _pack_w_in(w_t, d_a, d_b, d_qi, scale, idx_scale):
    p, d = w_t.shape
    width = p + 2 * LANES - IDX_DIM - IDX_HEADS
    return pl.pallas_call(
        functools.partial(_pack_w_in_kernel, d_a=d_a, d_b=d_b, d_qi=d_qi, scale=scale,
                          idx_scale=idx_scale),
        out_shape=jax.ShapeDtypeStruct((d, width), BF16),
        compiler_params=pltpu.CompilerParams(vmem_limit_bytes=VMEM_LIMIT_BYTES),
        name="pack_w_in",
    )(w_t)


def _inproj_kernel(x_ref, gain_ref, w_ref, qg_ref, kg_ref, gsum_ref,
                   qa_ref, ka_ref, vat_ref, ga_ref, qi_ref, ki_ref, wit_ref,
                   qb_ref, kb_ref, vbt_ref, gb_ref, *, d_a, d_b, d_qi):
    x = x_ref[0]
    ms = jnp.mean(x * x, axis=-1, keepdims=True)
    h = (x * lax.rsqrt(ms + RMS_EPS) * gain_ref[...]).astype(BF16)
    ts = vat_ref.shape[3]

    def proj(c0, width):
        return jnp.dot(h, w_ref[:, c0:c0 + width], preferred_element_type=F32)

    def head_norm(y, g):
        sq = (y * y).astype(BF16)
        ones = gsum_ref[...]
        ssum = jnp.concatenate(
            [jnp.dot(sq[:, c:c + LANES], ones, preferred_element_type=F32)
             for c in range(0, y.shape[1], LANES)], axis=1)
        return y * lax.rsqrt(ssum * (1.0 / HEAD_DIM) + RMS_EPS) * g

    def silu(g):
        return g * (1.0 / (1.0 + jnp.exp(-g)))

    def store_key_blocks_t(dst_ref, v):
        vt = v.T.astype(BF16)
        for c in range(dst_ref.shape[1]):
            dst_ref[0, c] = vt[:, c * ts:(c + 1) * ts]

    c = 0
    _store_head_masked(qa_ref, head_norm(proj(c, d_a), qg_ref[...])); c += d_a
    ka_ref[0] = head_norm(proj(c, d_a), kg_ref[...]).astype(BF16); c += d_a
    store_key_blocks_t(vat_ref, proj(c, d_a)); c += d_a
    ga_ref[0] = silu(proj(c, d_a)); c += d_a
    _store_head_masked(qi_ref, proj(c, d_qi)); c += d_qi
    ki_ref[0] = proj(c, LANES).astype(BF16); c += LANES
    wit_ref[0] = proj(c, LANES).T[:IDX_HEADS, :]; c += LANES
    _store_head_masked(qb_ref, proj(c, d_b)); c += d_b
    kb_ref[0] = proj(c, d_b).astype(BF16); c += d_b
    store_key_blocks_t(vbt_ref, proj(c, d_b)); c += d_b
    gb_ref[0] = silu(proj(c, d_b))


def _inproj(x, gain, w_all, qg, kg, gsum, d_a, d_b, d_qi):
    b, l, d = x.shape
    tm = TOKEN_TILE
    ts = ATT_TILE
    nb = l // ts
    row = lambda width: pl.BlockSpec((1, tm, width), lambda i, j: (i, j, 0))
    const = lambda shape: pl.BlockSpec(shape, lambda i, j: (0, 0))
    vt_spec = lambda ch: pl.BlockSpec((1, tm // ts, ch, ts), lambda i, j: (i, j, 0, 0))
    tok = lambda width, dt: jax.ShapeDtypeStruct((b, l, width), dt)
    heads = lambda width: jax.ShapeDtypeStruct((b, width // HEAD_DIM, LANES, l), BF16)
    heads_spec = lambda width: pl.BlockSpec((1, width // HEAD_DIM, LANES, tm), lambda i, j: (i, 0, 0, j))
    out_shapes = [
        heads(d_a), tok(d_a, BF16), jax.ShapeDtypeStruct((b, nb, d_a, ts), BF16), tok(d_a, F32),
        heads(d_qi), tok(LANES, BF16), jax.ShapeDtypeStruct((b, IDX_HEADS, l), F32),
        heads(d_b), tok(d_b, BF16), jax.ShapeDtypeStruct((b, nb, d_b, ts), BF16), tok(d_b, F32),
    ]
    out_specs = [heads_spec(d_a), row(d_a), vt_spec(d_a), row(d_a), heads_spec(d_qi), row(LANES),
                 pl.BlockSpec((1, IDX_HEADS, tm), lambda i, j: (i, 0, j)),
                 heads_spec(d_b), row(d_b), vt_spec(d_b), row(d_b)]
    return pl.pallas_call(
        functools.partial(_inproj_kernel, d_a=d_a, d_b=d_b, d_qi=d_qi),
        grid=(b, l // tm),
        in_specs=[row(d), const(gain.shape),
                  pl.BlockSpec(w_all.shape, lambda i, j: (0, 0), pipeline_mode=pl.Buffered(1)),
                  const(qg.shape), const(kg.shape), const(gsum.shape)],
        out_specs=out_specs,
        out_shape=out_shapes,
        compiler_params=_params("arbitrary", "arbitrary"),
        name="inproj",
    )(x, gain, w_all, qg, kg, gsum)


def _t5_large_thresholds():
    max_exact = NUM_BUCKETS // 2
    d = np.arange(max_exact, 2 * MAX_DISTANCE + 1)
    large = max_exact + (np.log(d.astype(np.float32) / np.float32(max_exact))
                         / np.float32(math.log(MAX_DISTANCE / max_exact))
                         * np.float32(NUM_BUCKETS - max_exact)).astype(np.int32)
    large = np.minimum(large, NUM_BUCKETS - 1)
    return [int(d[np.argmax(large >= k)]) for k in range(max_exact + 1, NUM_BUCKETS)]


def _t5_bucket(dist):
    max_exact = NUM_BUCKETS // 2
    d = jnp.maximum(dist, 0)
    large = jnp.full(d.shape, max_exact, I32)
    for first in _t5_large_thresholds():
        large = large + (d >= first).astype(I32)
    return jnp.where(d < max_exact, d, large)


def _t5_table_kernel(rb_ref, tab_ref, *, n_heads):
    ts = tab_ref.shape[2]
    rows = 16
    far = _t5_large_thresholds()[-1]
    t_rel = lax.broadcasted_iota(I32, (rows, ts), 1)
    s_rel = lax.broadcasted_iota(I32, (rows, ts), 0)

    for i in range(3):
        n_far = max(0, min(ts, i * ts - far + 1)) // rows

        def fill(c, _, i=i):
            r0 = pl.multiple_of(c * rows, rows)
            for h in range(n_heads):
                tab_ref[i, h, pl.ds(r0, rows), :] = jnp.full((rows, ts), rb_ref[NUM_BUCKETS - 1, h] * LOG2E, F32)
            return 0

        def compute(c, _, i=i):
            r0 = pl.multiple_of(c * rows, rows)
            bucket = _t5_bucket(t_rel - (r0 + s_rel) + i * ts)
            acc = [jnp.zeros((rows, ts), F32) for _ in range(n_heads)]
            for j in range(NUM_BUCKETS):
                hit = bucket == j
                acc = [jnp.where(hit, rb_ref[j, h] * LOG2E, acc[h]) for h in range(n_heads)]
            for h in range(n_heads):
                tab_ref[i, h, pl.ds(r0, rows), :] = acc[h]
            return 0

        lax.fori_loop(0, n_far, fill, 0)
        lax.fori_loop(n_far, ts // rows, compute, 0)


def _t5_table(rel_bias, ts):
    n_heads = rel_bias.shape[1]
    return pl.pallas_call(
        functools.partial(_t5_table_kernel, n_heads=n_heads),
        in_specs=[pl.BlockSpec(memory_space=pltpu.SMEM)],
        out_specs=pl.BlockSpec(memory_space=pltpu.VMEM),
        out_shape=jax.ShapeDtypeStruct((3, n_heads, ts, ts), F32),
        compiler_params=pltpu.CompilerParams(vmem_limit_bytes=VMEM_LIMIT_BYTES),
        name="t5_table",
    )(rel_bias)


def _dsa_kernel(ki_ref, qin_ref, witn_ref, ka_ref, vat_ref, qam_ref, ga_ref, tab_ref, o_ref,
                sc_ref, mm_ref, madd_ref, m_ref, l_ref, acc_ref, lg_ref, bmax_ref, *, topk, n_heads):
    tq = o_ref.shape[1]
    ts = tq
    q_blk = pl.program_id(1)
    nblk = q_blk + 1
    cur = q_blk % 2
    nxt = 1 - cur
    s_rel = lax.broadcasted_iota(I32, (ts, tq), 0)
    t_rel = lax.broadcasted_iota(I32, (ts, tq), 1)
    t_idx = q_blk * tq + t_rel

    def block_start(j):
        return pl.multiple_of(j * ts, ts)

    all_selected = nblk * tq <= topk
    n_sel = jnp.where(all_selected, 0, nblk)

    @pl.when(all_selected)
    def _():
        mm_ref[cur] = jnp.zeros(mm_ref.shape[1:], F32)

        def body(j, _):
            s0 = block_start(j)
            madd_ref[pl.ds(s0, ts), :] = jnp.where(s0 + s_rel > t_idx, NEG_BIG, 0.0)
            return 0

        lax.fori_loop(0, nblk, body, 0)

    vmin = jnp.min(mm_ref[cur, 0:8, :], axis=0, keepdims=True)
    vmax = jnp.max(mm_ref[cur, 8:16, :], axis=0, keepdims=True)

    def count(pred_fn):
        def body(j, c):
            s0 = block_start(j)
            hit = pred_fn(sc_ref[cur, pl.ds(s0, ts), :], s0).reshape(ts // 8, 8, tq)
            c = list(c)
            for r in range(ts // 8):
                c[r % 4] = jnp.where(hit[r], c[r % 4] + 1, c[r % 4])
            return tuple(c)
        z = jnp.zeros((8, tq), I32)
        c8 = lax.fori_loop(0, n_sel, body, (z, z, z, z))
        return jnp.sum(c8[0] + c8[1] + c8[2] + c8[3], axis=0, keepdims=True)

    def midpoint(lo, hi):
        return 0.5 * lo + 0.5 * hi

    def searching(lo, hi, cnt_lo):
        mid = midpoint(lo, hi)
        return (cnt_lo > topk) & (mid > lo) & (mid < hi)

    def any_lane(flag):
        return jnp.max(flag.astype(I32)) > 0

    n_causal = t_idx[0:1, :] + 1
    lo0 = jnp.where(n_causal > topk, vmin, -jnp.inf)
    hi0 = vmax + (jnp.abs(vmax) * 2.0 ** -20 + 1e-30)

    def bisect_cond(carry):
        it, _, _, _, live = carry
        return live & (it < BISECT_CAP)

    def bisect_step(_, carry):
        lo, hi, cnt_lo = carry
        go = searching(lo, hi, cnt_lo)
        mid = midpoint(lo, hi)
        cnt = count(lambda blk, s0: blk >= mid)
        take = go & (cnt >= topk)
        return (jnp.where(take, mid, lo), jnp.where(go & (cnt < topk), mid, hi),
                jnp.where(take, cnt, cnt_lo))

    def bisect_body(carry):
        it, lo, hi, cnt_lo, _ = carry
        lo, hi, cnt_lo = lax.fori_loop(0, BISECT_GROUP, bisect_step, (lo, hi, cnt_lo))
        return it + BISECT_GROUP, lo, hi, cnt_lo, any_lane(searching(lo, hi, cnt_lo))

    n_blind = jnp.where(all_selected, 0, BISECT_BLIND)
    lo, hi, cnt_lo = lax.fori_loop(0, n_blind, bisect_step, (lo0, hi0, n_causal))
    _, thr, _, cnt_thr, _ = lax.while_loop(
        bisect_cond, bisect_body,
        (jnp.int32(BISECT_BLIND), lo, hi, cnt_lo,
         jnp.logical_not(all_selected) & any_lane(searching(lo, hi, cnt_lo))))

    tie = jnp.logical_not(all_selected) & any_lane(cnt_thr > topk)

    @pl.when(jnp.logical_not(tie))
    def _():
        def body(j, _):
            s0 = block_start(j)
            sel = (sc_ref[cur, pl.ds(s0, ts), :] >= thr) & (s0 + s_rel <= t_idx)
            madd_ref[pl.ds(s0, ts), :] = jnp.where(sel, 0.0, NEG_BIG)
            return 0

        lax.fori_loop(0, n_sel, body, 0)

    @pl.when(tie)
    def _():
        need = topk - count(lambda blk, s0: blk > thr)
        n_bits = sc_ref.shape[1].bit_length()

        def jbody(i, bound):
            cand = bound + jnp.left_shift(jnp.int32(1), n_bits - 1 - i)
            cnt = count(lambda blk, s0: (blk == thr) & (s0 + s_rel < cand))
            return jnp.where(cnt <= need, cand, bound)

        bound = lax.fori_loop(0, n_bits, jbody, jnp.zeros((1, tq), I32))

        def body(j, _):
            s0 = block_start(j)
            blk = sc_ref[cur, pl.ds(s0, ts), :]
            sel = ((blk > thr) | ((blk == thr) & (s0 + s_rel < bound))) & (s0 + s_rel <= t_idx)
            madd_ref[pl.ds(s0, ts), :] = jnp.where(sel, 0.0, NEG_BIG)
            return 0

        lax.fori_loop(0, n_sel, body, 0)

    m_ref[...] = jnp.full(m_ref.shape, NEG_BIG, F32)
    l_ref[...] = jnp.zeros(l_ref.shape, F32)
    acc_ref[...] = jnp.zeros(acc_ref.shape, F32)

    def attend(j):
        s0 = block_start(j)
        madd = madd_ref[pl.ds(s0, ts), :]
        near = jnp.minimum(q_blk - j, 2)
        for h in range(n_heads):
            lg = jnp.dot(ka_ref[0, pl.ds(s0, ts), _pair(h)], qam_ref[0, h],
                         preferred_element_type=F32)
            lg = lg + tab_ref[near, h] + madd
            lg_ref[h] = lg
            bmax_ref[h] = jnp.max(lg, axis=0, keepdims=True)
        for h in range(n_heads):
            rows = slice(h * HEAD_DIM, (h + 1) * HEAD_DIM)
            m_old = m_ref[h]
            m_new = jnp.maximum(m_old, bmax_ref[h])
            alpha = jnp.exp2(m_old - m_new)
            p = jnp.exp2(lg_ref[h] - m_new)
            l_ref[h] = alpha * l_ref[h] + jnp.sum(p, axis=0, keepdims=True)
            pv = jnp.dot(vat_ref[0, j, rows, :], p.astype(BF16), preferred_element_type=F32)
            acc_ref[rows, :] = alpha * acc_ref[rows, :] + pv
            m_ref[h] = m_new

    def index_next(j, carry, diagonal):
        vmin8, vmax8 = carry
        s0 = block_start(j)
        kblk = ki_ref[0, pl.ds(s0, ts), :]
        score = jnp.zeros((ts, tq), F32)
        for h in range(IDX_HEADS):
            dots = jnp.dot(kblk, qin_ref[0, h], preferred_element_type=F32)
            score = score + jnp.maximum(dots, 0.0) * witn_ref[0, h:h + 1, :]
        if diagonal:
            future = s_rel > t_rel
            low = jnp.where(future, jnp.inf, score)
            score = jnp.where(future, -jnp.inf, score)
        else:
            low = score
        sc_ref[nxt, pl.ds(s0, ts), :] = score
        return (jnp.minimum(vmin8, low.reshape(ts // 8, 8, tq).min(axis=0)),
                jnp.maximum(vmax8, score.reshape(ts // 8, 8, tq).max(axis=0)))

    has_next = q_blk + 1 < pl.num_programs(1)

    @pl.when(has_next)
    def _():
        def body(j, carry):
            attend(j)
            return index_next(j, carry, diagonal=False)

        carry = lax.fori_loop(0, nblk, body, (jnp.full((8, tq), jnp.inf, F32),
                                              jnp.full((8, tq), -jnp.inf, F32)))
        vmin8, vmax8 = index_next(nblk, carry, diagonal=True)
        mm_ref[nxt, 0:8, :] = vmin8
        mm_ref[nxt, 8:16, :] = vmax8

    @pl.when(jnp.logical_not(has_next))
    def _():
        def body(j, _):
            attend(j)
            return 0

        lax.fori_loop(0, nblk, body, 0)

    for h in range(n_heads):
        rows = slice(h * HEAD_DIM, (h + 1) * HEAD_DIM)
        acc_ref[rows, :] = acc_ref[rows, :] * (1.0 / l_ref[h])
    o_ref[0] = (acc_ref[...].T * ga_ref[0]).astype(BF16)


def _dsa(ki, qim, wit, ka, vat, qam, ga, tab, topk):
    b, l, d_a = ka.shape
    n_heads = d_a // HEAD_DIM
    tq = ATT_TILE
    nq = l // tq
    assert tq <= topk, "the first query tile must not need indexer scores"
    next_tile = lambda j: jnp.minimum(j + 1, nq - 1)
    kernel = functools.partial(_dsa_kernel, topk=topk, n_heads=n_heads)
    return pl.pallas_call(
        kernel,
        grid=(b, nq),
        in_specs=[
            pl.BlockSpec((1, l, LANES), lambda i, j: (i, 0, 0)),
            pl.BlockSpec((1, IDX_HEADS, LANES, tq), lambda i, j: (i, 0, 0, next_tile(j))),
            pl.BlockSpec((1, IDX_HEADS, tq), lambda i, j: (i, 0, next_tile(j))),
            pl.BlockSpec((1, l, d_a), lambda i, j: (i, 0, 0)),
            pl.BlockSpec((1, nq, d_a, tq), lambda i, j: (i, 0, 0, 0)),
            pl.BlockSpec((1, n_heads, LANES, tq), lambda i, j: (i, 0, 0, j)),
            pl.BlockSpec((1, tq, d_a), lambda i, j: (i, j, 0)),
            pl.BlockSpec(tab.shape, lambda i, j: (0, 0, 0, 0), pipeline_mode=pl.Buffered(1)),
        ],
        out_specs=pl.BlockSpec((1, tq, d_a), lambda i, j: (i, j, 0)),
        out_shape=jax.ShapeDtypeStruct((b, l, d_a), BF16),
        scratch_shapes=[
            pltpu.VMEM((2, l, tq), F32),
            pltpu.VMEM((2, 16, tq), F32),
            pltpu.VMEM((l, tq), F32),
            pltpu.VMEM((n_heads, 1, tq), F32),
            pltpu.VMEM((n_heads, 1, tq), F32),
            pltpu.VMEM((d_a, tq), F32),
            pltpu.VMEM((n_heads, tq, tq), F32),
            pltpu.VMEM((n_heads, 1, tq), F32),
        ],
        compiler_params=_params("arbitrary", "arbitrary"),
        name="dsa",
    )(ki, qim, wit, ka, vat, qam, ga, tab)


def _stick_kernel(kb_ref, vbt_ref, qbm_ref, gb_ref, u_ref, x_ref, oa_ref, wa_ref, wb_ref, y_ref,
                  below_ref, acc_ref, z_ref, lb_ref, *, n_heads):
    tq = y_ref.shape[1]
    ts = tq
    q_blk = pl.program_id(1)
    s_rel = lax.broadcasted_iota(I32, (ts, tq), 0)
    t_rel = lax.broadcasted_iota(I32, (ts, tq), 1)

    below_ref[...] = jnp.zeros(below_ref.shape, F32)
    acc_ref[...] = jnp.zeros(acc_ref.shape, F32)

    def block(j, diagonal):
        s0 = pl.multiple_of(j * ts, ts)
        strict = s_rel < t_rel

        def z_dot(h):
            return jnp.dot(kb_ref[0, pl.ds(s0, ts), _pair(h)], qbm_ref[0, h],
                           preferred_element_type=F32)

        def terms(z2):
            log_beta = jnp.minimum(z2, 0.0) - jnp.log2(1.0 + jnp.exp2(-jnp.abs(z2)))
            log_om = log_beta - z2
            if diagonal:
                log_om = jnp.where(strict, log_om, 0.0)
            suffix = jnp.dot(u_ref[...], log_om.astype(BF16), preferred_element_type=F32)
            return log_beta, log_om[0:1, :], suffix

        def finish(h, log_a, om_total):
            rows = slice(h * HEAD_DIM, (h + 1) * HEAD_DIM)
            below = below_ref[h]
            a = jnp.exp2(log_a)
            if diagonal:
                a = jnp.where(strict, a, 0.0)
            pv = jnp.dot(vbt_ref[0, j, rows, :], a.astype(BF16), preferred_element_type=F32)
            acc_ref[rows, :] += pv * jnp.exp2(below)
            below = below + om_total
            below_ref[h] = below
            return below

        for h in range(n_heads):
            z_ref[h] = z_dot(h)
        om0 = []
        for h in range(n_heads):
            log_beta, om_row, suffix = terms(z_ref[h])
            lb_ref[h] = log_beta + suffix
            om0.append(om_row + suffix[0:1, :])
        worst = None
        for h in range(n_heads):
            below = finish(h, lb_ref[h], om0[h])
            worst = below if worst is None else jnp.maximum(worst, below)
        return jnp.max(worst)

    top = block(q_blk, diagonal=True)

    def cond(carry):
        i, top = carry
        return (i <= q_blk) & (top >= EXP2_UNDERFLOW)

    def body(carry):
        i, _ = carry
        return i + 1, block(q_blk - i, diagonal=False)

    lax.while_loop(cond, body, (jnp.int32(1), top))

    ob = (acc_ref[...].T * gb_ref[0]).astype(BF16)
    y_ref[0] = (x_ref[0]
                + jnp.dot(oa_ref[0], wa_ref[...], preferred_element_type=F32)
                + jnp.dot(ob, wb_ref[...], preferred_element_type=F32))


def _stick(kb, vbt, qbm, gb, u2, x, oa, wa, wb):
    b, l, d_b = kb.shape
    d = x.shape[2]
    n_heads = d_b // HEAD_DIM
    tq = ATT_TILE
    nq = l // tq
    return pl.pallas_call(
        functools.partial(_stick_kernel, n_heads=n_heads),
        grid=(b, nq),
        in_specs=[
            pl.BlockSpec((1, l, d_b), lambda i, j: (i, 0, 0)),
            pl.BlockSpec((1, nq, d_b, tq), lambda i, j: (i, 0, 0, 0)),
            pl.BlockSpec((1, n_heads, LANES, tq), lambda i, j: (i, 0, 0, j)),
            pl.BlockSpec((1, tq, d_b), lambda i, j: (i, j, 0)),
            pl.BlockSpec(u2.shape, lambda i, j: (0, 0)),
            pl.BlockSpec((1, tq, d), lambda i, j: (i, j, 0)),
            pl.BlockSpec((1, tq, oa.shape[2]), lambda i, j: (i, j, 0)),
            pl.BlockSpec(wa.shape, lambda i, j: (0, 0)),
            pl.BlockSpec(wb.shape, lambda i, j: (0, 0)),
        ],
        out_specs=pl.BlockSpec((1, tq, d), lambda i, j: (i, j, 0)),
        out_shape=jax.ShapeDtypeStruct((b, l, d), F32),
        scratch_shapes=[
            pltpu.VMEM((n_heads, 1, tq), F32),
            pltpu.VMEM((d_b, tq), F32),
            pltpu.VMEM((n_heads, tq, tq), F32),
            pltpu.VMEM((n_heads, tq, tq), F32),
        ],
        compiler_params=_params("arbitrary", "arbitrary"),
        name="stick",
    )(kb, vbt, qbm, gb, u2, x, oa, wa, wb)


def kernel(x, norm_gain, w_in, q_norm_gain, k_norm_gain, rel_bias, w_out):
    b, l, d = x.shape
    depth = w_in.shape[0]
    d_a = d // 2
    d_b = d - d_a
    h_a = d_a // HEAD_DIM
    d_qi = IDX_HEADS * IDX_DIM
    topk = min(TOPK_MAX, l // 4)
    ts = ATT_TILE
    scale = HEAD_DIM ** -0.5 * LOG2E
    idx_scale = (IDX_HEADS * IDX_DIM) ** -0.5

    lane = jnp.arange(LANES)
    gsum = (lane[:, None] // HEAD_DIM == lane[None, :] // HEAD_DIM).astype(BF16)
    u2 = (jnp.arange(ts)[None, :] > jnp.arange(ts)[:, None]).astype(BF16)
    tab = _t5_table(rel_bias.astype(F32), ts)

    for layer in range(depth):
        w_all = _pack_w_in(jnp.swapaxes(w_in[layer], 0, 1), d_a, d_b, d_qi, scale, idx_scale)
        qg = jnp.tile(q_norm_gain[layer] * scale, h_a)[None, :]
        kg = jnp.tile(k_norm_gain[layer], h_a)[None, :]

        (qam, ka, vat, ga, qim, ki, wit, qbm, kb, vbt, gb) = _inproj(
            x, norm_gain[layer][None, :], w_all, qg, kg, gsum, d_a, d_b, d_qi)
        oa = _dsa(ki, qim, wit, ka, vat, qam, ga, tab, topk)
        w_o = w_out[layer].astype(BF16)
        x = _stick(kb, vbt, qbm, gb, u2, x, oa, w_o[:d_a], w_o[d_a:])
    return x
```

```python
import functools
import math

import jax
import jax.numpy as jnp
import numpy as np
from jax import lax
from jax.experimental import pallas as pl
from jax.experimental.pallas import tpu as pltpu

F32 = jnp.float32
BF16 = jnp.bfloat16
I32 = jnp.int32

HEAD_DIM = 64
IDX_HEADS = 16
IDX_DIM = 64
TOPK_MAX = 256
NUM_BUCKETS = 32
MAX_DISTANCE = 128
RMS_EPS = 1e-6

LANES = 128
NEG_BIG = -1e30
EXP2_UNDERFLOW = -150.0
LOG2E = math.log2(math.e)
BISECT_CAP = 300
BISECT_BLIND = 20
BISECT_GROUP = 2

VMEM_LIMIT_BYTES = 56 * 1024 * 1024
TOKEN_TILE = 512
ATT_TILE = 256


def _params(*sem):
    return pltpu.CompilerParams(dimension_semantics=sem, vmem_limit_bytes=VMEM_LIMIT_BYTES)


def _pair(h):
    return slice((h // 2) * LANES, (h // 2 + 1) * LANES)


def _store_head_masked(dst_ref, src):
    rows = src.shape[0]
    low = lax.broadcasted_iota(I32, (rows, LANES), 1) < HEAD_DIM
    for h in range(dst_ref.shape[1]):
        keep = low if h % 2 == 0 else jnp.logical_not(low)
        dst_ref[0, h] = jnp.where(keep, src[:, _pair(h)], 0.0).T.astype(BF16)


def _pack_w_in_kernel(wt_ref, o_ref, *, d_a, d_b, d_qi, scale, idx_scale):
    o_ki = 4 * d_a + d_qi
    o_wi = o_ki + IDX_DIM
    o_b = o_wi + IDX_HEADS
    d = wt_ref.shape[1]

    def put(dst, rows):
        o_ref[:, dst:dst + LANES] = rows.T.astype(BF16)

    for c in range(0, o_ki, LANES):
        put(c, wt_ref[c:c + LANES, :])
    k_idx = wt_ref[o_ki:o_wi, :]
    put(o_ki, jnp.concatenate([k_idx, k_idx], axis=0))
    w_idx = wt_ref[o_wi:o_b, :] * idx_scale
    put(o_ki + LANES, jnp.concatenate([w_idx, jnp.zeros((LANES - IDX_HEADS, d), F32)], axis=0))
    for c in range(0, 4 * d_b, LANES):
        rows = wt_ref[o_b + c:o_b + c + LANES, :]
        put(o_ki + 2 * LANES + c, rows * scale if c < d_b else rows)


def _pack_w_in(w_t, d_a, d_b, d_qi, scale, idx_scale):
    p, d = w_t.shape
    width = p + 2 * LANES - IDX_DIM - IDX_HEADS
    return pl.pallas_call(
        functools.partial(_pack_w_in_kernel, d_a=d_a, d_b=d_b, d_qi=d_qi, scale=scale,
                          idx_scale=idx_scale),
        out_shape=jax.ShapeDtypeStruct((d, width), BF16),
        compiler_params=pltpu.CompilerParams(vmem_limit_bytes=VMEM_LIMIT_BYTES),
        name="pack_w_in",
    )(w_t)


def _inproj_kernel(x_ref, gain_ref, w_ref, qg_ref, kg_ref, gsum_ref,
                   qa_ref, ka_ref, vat_ref, ga_ref, qi_ref, ki_ref, wit_ref,
                   qb_ref, kb_ref, vbt_ref, gb_ref, *, d_a, d_b, d_qi):
    x = x_ref[0]
    ms = jnp.mean(x * x, axis=-1, keepdims=True)
    h = (x * lax.rsqrt(ms + RMS_EPS) * gain_ref[...]).astype(BF16)
    ts = vat_ref.shape[3]

    def proj(c0, width):
        return jnp.dot(h, w_ref[:, c0:c0 + width], preferred_element_type=F32)

    def head_norm(y, g):
        sq = (y * y).astype(BF16)
        ones = gsum_ref[...]
        ssum = jnp.concatenate(
            [jnp.dot(sq[:, c:c + LANES], ones, preferred_element_type=F32)
             for c in range(0, y.shape[1], LANES)], axis=1)
        return y * lax.rsqrt(ssum * (1.0 / HEAD_DIM) + RMS_EPS) * g

    def silu(g):
        return g * (1.0 / (1.0 + jnp.exp(-g)))

    def store_key_blocks_t(dst_ref, v):
        vt = v.T.astype(BF16)
        for c in range(dst_ref.shape[1]):
            dst_ref[0, c] = vt[:, c * ts:(c + 1) * ts]

    c = 0
    _store_head_masked(qa_ref, head_norm(proj(c, d_a), qg_ref[...])); c += d_a
    ka_ref[0] = head_norm(proj(c, d_a), kg_ref[...]).astype(BF16); c += d_a
    store_key_blocks_t(vat_ref, proj(c, d_a)); c += d_a
    ga_ref[0] = silu(proj(c, d_a)); c += d_a
    _store_head_masked(qi_ref, proj(c, d_qi)); c += d_qi
    ki_ref[0] = proj(c, LANES).astype(BF16); c += LANES
    wit_ref[0] = proj(c, LANES).T[:IDX_HEADS, :]; c += LANES
    _store_head_masked(qb_ref, proj(c, d_b)); c += d_b
    kb_ref[0] = proj(c, d_b).astype(BF16); c += d_b
    store_key_blocks_t(vbt_ref, proj(c, d_b)); c += d_b
    gb_ref[0] = silu(proj(c, d_b))


def _inproj(x, gain, w_all, qg, kg, gsum, d_a, d_b, d_qi):
    b, l, d = x.shape
    tm = TOKEN_TILE
    ts = ATT_TILE
    nb = l // ts
    row = lambda width: pl.BlockSpec((1, tm, width), lambda i, j: (i, j, 0))
    const = lambda shape: pl.BlockSpec(shape, lambda i, j: (0, 0))
    vt_spec = lambda ch: pl.BlockSpec((1, tm // ts, ch, ts), lambda i, j: (i, j, 0, 0))
    tok = lambda width, dt: jax.ShapeDtypeStruct((b, l, width), dt)
    heads = lambda width: jax.ShapeDtypeStruct((b, width // HEAD_DIM, LANES, l), BF16)
    heads_spec = lambda width: pl.BlockSpec((1, width // HEAD_DIM, LANES, tm), lambda i, j: (i, 0, 0, j))
    out_shapes = [
        heads(d_a), tok(d_a, BF16), jax.ShapeDtypeStruct((b, nb, d_a, ts), BF16), tok(d_a, F32),
        heads(d_qi), tok(LANES, BF16), jax.ShapeDtypeStruct((b, IDX_HEADS, l), F32),
        heads(d_b), tok(d_b, BF16), jax.ShapeDtypeStruct((b, nb, d_b, ts), BF16), tok(d_b, F32),
    ]
    out_specs = [heads_spec(d_a), row(d_a), vt_spec(d_a), row(d_a), heads_spec(d_qi), row(LANES),
                 pl.BlockSpec((1, IDX_HEADS, tm), lambda i, j: (i, 0, j)),
                 heads_spec(d_b), row(d_b), vt_spec(d_b), row(d_b)]
    return pl.pallas_call(
        functools.partial(_inproj_kernel, d_a=d_a, d_b=d_b, d_qi=d_qi),
        grid=(b, l // tm),
        in_specs=[row(d), const(gain.shape),
                  pl.BlockSpec(w_all.shape, lambda i, j: (0, 0), pipeline_mode=pl.Buffered(1)),
                  const(qg.shape), const(kg.shape), const(gsum.shape)],
        out_specs=out_specs,
        out_shape=out_shapes,
        compiler_params=_params("arbitrary", "arbitrary"),
        name="inproj",
    )(x, gain, w_all, qg, kg, gsum)


def _t5_large_thresholds():
    max_exact = NUM_BUCKETS // 2
    d = np.arange(max_exact, 2 * MAX_DISTANCE + 1)
    large = max_exact + (np.log(d.astype(np.float32) / np.float32(max_exact))
                         / np.float32(math.log(MAX_DISTANCE / max_exact))
                         * np.float32(NUM_BUCKETS - max_exact)).astype(np.int32)
    large = np.minimum(large, NUM_BUCKETS - 1)
    return [int(d[np.argmax(large >= k)]) for k in range(max_exact + 1, NUM_BUCKETS)]


def _t5_bucket(dist):
    max_exact = NUM_BUCKETS // 2
    d = jnp.maximum(dist, 0)
    large = jnp.full(d.shape, max_exact, I32)
    for first in _t5_large_thresholds():
        large = large + (d >= first).astype(I32)
    return jnp.where(d < max_exact, d, large)


def _t5_table_kernel(rb_ref, tab_ref, *, n_heads):
    ts = tab_ref.shape[2]
    rows = 16
    far = _t5_large_thresholds()[-1]
    t_rel = lax.broadcasted_iota(I32, (rows, ts), 1)
    s_rel = lax.broadcasted_iota(I32, (rows, ts), 0)

    for i in range(3):
        n_far = max(0, min(ts, i * ts - far + 1)) // rows

        def fill(c, _, i=i):
            r0 = pl.multiple_of(c * rows, rows)
            for h in range(n_heads):
                tab_ref[i, h, pl.ds(r0, rows), :] = jnp.full((rows, ts), rb_ref[NUM_BUCKETS - 1, h] * LOG2E, F32)
            return 0

        def compute(c, _, i=i):
            r0 = pl.multiple_of(c * rows, rows)
            bucket = _t5_bucket(t_rel - (r0 + s_rel) + i * ts)
            acc = [jnp.zeros((rows, ts), F32) for _ in range(n_heads)]
            for j in range(NUM_BUCKETS):
                hit = bucket == j
                acc = [jnp.where(hit, rb_ref[j, h] * LOG2E, acc[h]) for h in range(n_heads)]
            for h in range(n_heads):
                tab_ref[i, h, pl.ds(r0, rows), :] = acc[h]
            return 0

        lax.fori_loop(0, n_far, fill, 0)
        lax.fori_loop(n_far, ts // rows, compute, 0)


def _t5_table(rel_bias, ts):
    n_heads = rel_bias.shape[1]
    return pl.pallas_call(
        functools.partial(_t5_table_kernel, n_heads=n_heads),
        in_specs=[pl.BlockSpec(memory_space=pltpu.SMEM)],
        out_specs=pl.BlockSpec(memory_space=pltpu.VMEM),
        out_shape=jax.ShapeDtypeStruct((3, n_heads, ts, ts), F32),
        compiler_params=pltpu.CompilerParams(vmem_limit_bytes=VMEM_LIMIT_BYTES),
        name="t5_table",
    )(rel_bias)


def _dsa_kernel(ki_ref, qin_ref, witn_ref, ka_ref, vat_ref, qam_ref, ga_ref, tab_ref, o_ref,
                sc_ref, mm_ref, madd_ref, m_ref, l_ref, acc_ref, lg_ref, bmax_ref, *, topk, n_heads):
    tq = o_ref.shape[1]
    ts = tq
    q_blk = pl.program_id(1)
    nblk = q_blk + 1
    cur = q_blk % 2
    nxt = 1 - cur
    s_rel = lax.broadcasted_iota(I32, (ts, tq), 0)
    t_rel = lax.broadcasted_iota(I32, (ts, tq), 1)
    t_idx = q_blk * tq + t_rel

    def block_start(j):
        return pl.multiple_of(j * ts, ts)

    all_selected = nblk * tq <= topk
    n_sel = jnp.where(all_selected, 0, nblk)

    @pl.when(all_selected)
    def _():
        mm_ref[cur] = jnp.zeros(mm_ref.shape[1:], F32)

        def body(j, _):
            s0 = block_start(j)
            madd_ref[pl.ds(s0, ts), :] = jnp.where(s0 + s_rel > t_idx, NEG_BIG, 0.0)
            return 0

        lax.fori_loop(0, nblk, body, 0)

    vmin = jnp.min(mm_ref[cur, 0:8, :], axis=0, keepdims=True)
    vmax = jnp.max(mm_ref[cur, 8:16, :], axis=0, keepdims=True)

    def count(pred_fn):
        def body(j, c):
            s0 = block_start(j)
            hit = pred_fn(sc_ref[cur, pl.ds(s0, ts), :], s0).reshape(ts // 8, 8, tq)
            c = list(c)
            for r in range(ts // 8):
                c[r % 4] = jnp.where(hit[r], c[r % 4] + 1, c[r % 4])
            return tuple(c)
        z = jnp.zeros((8, tq), I32)
        c8 = lax.fori_loop(0, n_sel, body, (z, z, z, z))
        return jnp.sum(c8[0] + c8[1] + c8[2] + c8[3], axis=0, keepdims=True)

    def midpoint(lo, hi):
        return 0.5 * lo + 0.5 * hi

    def searching(lo, hi, cnt_lo):
        mid = midpoint(lo, hi)
        return (cnt_lo > topk) & (mid > lo) & (mid < hi)

    def status(lo, hi, cnt_lo):
        flags = 2 * searching(lo, hi, cnt_lo).astype(I32) + (cnt_lo > topk).astype(I32)
        return jnp.max(flags)

    n_causal = t_idx[0:1, :] + 1
    lo0 = jnp.where(n_causal > topk, vmin, -jnp.inf)
    hi0 = vmax + (jnp.abs(vmax) * 2.0 ** -20 + 1e-30)

    def bisect_cond(carry):
        it, _, _, _, state = carry
        return (state >= 2) & (it < BISECT_CAP)

    def bisect_step(_, carry):
        lo, hi, cnt_lo = carry
        go = searching(lo, hi, cnt_lo)
        mid = midpoint(lo, hi)
        cnt = count(lambda blk, s0: blk >= mid)
        take = go & (cnt >= topk)
        return (jnp.where(take, mid, lo), jnp.where(go & (cnt < topk), mid, hi),
                jnp.where(take, cnt, cnt_lo))

    def bisect_body(carry):
        it, lo, hi, cnt_lo, _ = carry
        lo, hi, cnt_lo = lax.fori_loop(0, BISECT_GROUP, bisect_step, (lo, hi, cnt_lo))
        return it + BISECT_GROUP, lo, hi, cnt_lo, status(lo, hi, cnt_lo)

    n_blind = jnp.where(all_selected, 0, BISECT_BLIND)
    lo, hi, cnt_lo = lax.fori_loop(0, n_blind, bisect_step, (lo0, hi0, n_causal))
    _, thr, hi, cnt_thr, state = lax.while_loop(
        bisect_cond, bisect_body,
        (jnp.int32(BISECT_BLIND), lo, hi, cnt_lo,
         jnp.where(all_selected, 0, status(lo, hi, cnt_lo))))

    tie = state >= 1

    @pl.when(jnp.logical_not(tie))
    def _():
        def body(j, _):
            s0 = block_start(j)
            sel = (sc_ref[cur, pl.ds(s0, ts), :] >= thr) & (s0 + s_rel <= t_idx)
            madd_ref[pl.ds(s0, ts), :] = jnp.where(sel, 0.0, NEG_BIG)
            return 0

        lax.fori_loop(0, n_sel, body, 0)

    @pl.when(tie)
    def _():
        need = topk - count(lambda blk, s0: blk > thr)
        n_bits = sc_ref.shape[1].bit_length()

        def jbody(i, bound):
            cand = bound + jnp.left_shift(jnp.int32(1), n_bits - 1 - i)
            cnt = count(lambda blk, s0: (blk == thr) & (s0 + s_rel < cand))
            return jnp.where(cnt <= need, cand, bound)

        bound = lax.fori_loop(0, n_bits, jbody, jnp.zeros((1, tq), I32))

        def body(j, _):
            s0 = block_start(j)
            blk = sc_ref[cur, pl.ds(s0, ts), :]
            sel = ((blk > thr) | ((blk == thr) & (s0 + s_rel < bound))) & (s0 + s_rel <= t_idx)
            madd_ref[pl.ds(s0, ts), :] = jnp.where(sel, 0.0, NEG_BIG)
            return 0

        lax.fori_loop(0, n_sel, body, 0)

    m_ref[...] = jnp.full(m_ref.shape, NEG_BIG, F32)
    l_ref[...] = jnp.zeros(l_ref.shape, F32)
    acc_ref[...] = jnp.zeros(acc_ref.shape, F32)

    def attend(j):
        s0 = block_start(j)
        madd = madd_ref[pl.ds(s0, ts), :]
        near = jnp.minimum(q_blk - j, 2)
        for h in range(n_heads):
            lg = jnp.dot(ka_ref[0, pl.ds(s0, ts), _pair(h)], qam_ref[0, h],
                         preferred_element_type=F32)
            lg = lg + tab_ref[near, h] + madd
            lg_ref[h] = lg
            bmax_ref[h] = jnp.max(lg, axis=0, keepdims=True)
        for h in range(n_heads):
            rows = slice(h * HEAD_DIM, (h + 1) * HEAD_DIM)
            m_old = m_ref[h]
            m_new = jnp.maximum(m_old, bmax_ref[h])
            alpha = jnp.exp2(m_old - m_new)
            p = jnp.exp2(lg_ref[h] - m_new)
            l_ref[h] = alpha * l_ref[h] + jnp.sum(p, axis=0, keepdims=True)
            pv = jnp.dot(vat_ref[0, j, rows, :], p.astype(BF16), preferred_element_type=F32)
            acc_ref[rows, :] = alpha * acc_ref[rows, :] + pv
            m_ref[h] = m_new

    def index_next(j, carry, diagonal):
        vmin8, vmax8 = carry
        s0 = block_start(j)
        kblk = ki_ref[0, pl.ds(s0, ts), :]
        score = jnp.zeros((ts, tq), F32)
        for h in range(IDX_HEADS):
            dots = jnp.dot(kblk, qin_ref[0, h], preferred_element_type=F32)
            score = score + jnp.maximum(dots, 0.0) * witn_ref[0, h:h + 1, :]
        if diagonal:
            future = s_rel > t_rel
            low = jnp.where(future, jnp.inf, score)
            score = jnp.where(future, -jnp.inf, score)
        else:
            low = score
        sc_ref[nxt, pl.ds(s0, ts), :] = score
        return (jnp.minimum(vmin8, low.reshape(ts // 8, 8, tq).min(axis=0)),
                jnp.maximum(vmax8, score.reshape(ts // 8, 8, tq).max(axis=0)))

    has_next = q_blk + 1 < pl.num_programs(1)

    @pl.when(has_next)
    def _():
        def pair(i, carry):
            for j in (2 * i, 2 * i + 1):
                attend(j)
                carry = index_next(j, carry, diagonal=False)
            return carry

        def single(j, carry):
            attend(j)
            return index_next(j, carry, diagonal=False)

        carry = lax.fori_loop(0, nblk // 2, pair, (jnp.full((8, tq), jnp.inf, F32),
                                                   jnp.full((8, tq), -jnp.inf, F32)))
        carry = lax.fori_loop(2 * (nblk // 2), nblk, single, carry)
        vmin8, vmax8 = index_next(nblk, carry, diagonal=True)
        mm_ref[nxt, 0:8, :] = vmin8
        mm_ref[nxt, 8:16, :] = vmax8

    @pl.when(jnp.logical_not(has_next))
    def _():
        def body(j, _):
            attend(j)
            return 0

        lax.fori_loop(0, nblk, body, 0)

    for h in range(n_heads):
        rows = slice(h * HEAD_DIM, (h + 1) * HEAD_DIM)
        acc_ref[rows, :] = acc_ref[rows, :] * (1.0 / l_ref[h])
    o_ref[0] = (acc_ref[...].T * ga_ref[0]).astype(BF16)


def _dsa(ki, qim, wit, ka, vat, qam, ga, tab, topk):
    b, l, d_a = ka.shape
    n_heads = d_a // HEAD_DIM
    tq = ATT_TILE
    nq = l // tq
    assert tq <= topk, "the first query tile must not need indexer scores"
    next_tile = lambda j: jnp.minimum(j + 1, nq - 1)
    kernel = functools.partial(_dsa_kernel, topk=topk, n_heads=n_heads)
    return pl.pallas_call(
        kernel,
        grid=(b, nq),
        in_specs=[
            pl.BlockSpec((1, l, LANES), lambda i, j: (i, 0, 0)),
            pl.BlockSpec((1, IDX_HEADS, LANES, tq), lambda i, j: (i, 0, 0, next_tile(j))),
            pl.BlockSpec((1, IDX_HEADS, tq), lambda i, j: (i, 0, next_tile(j))),
            pl.BlockSpec((1, l, d_a), lambda i, j: (i, 0, 0)),
            pl.BlockSpec((1, nq, d_a, tq), lambda i, j: (i, 0, 0, 0)),
            pl.BlockSpec((1, n_heads, LANES, tq), lambda i, j: (i, 0, 0, j)),
            pl.BlockSpec((1, tq, d_a), lambda i, j: (i, j, 0)),
            pl.BlockSpec(tab.shape, lambda i, j: (0, 0, 0, 0), pipeline_mode=pl.Buffered(1)),
        ],
        out_specs=pl.BlockSpec((1, tq, d_a), lambda i, j: (i, j, 0)),
        out_shape=jax.ShapeDtypeStruct((b, l, d_a), BF16),
        scratch_shapes=[
            pltpu.VMEM((2, l, tq), F32),
            pltpu.VMEM((2, 16, tq), F32),
            pltpu.VMEM((l, tq), F32),
            pltpu.VMEM((n_heads, 1, tq), F32),
            pltpu.VMEM((n_heads, 1, tq), F32),
            pltpu.VMEM((d_a, tq), F32),
            pltpu.VMEM((n_heads, tq, tq), F32),
            pltpu.VMEM((n_heads, 1, tq), F32),
        ],
        compiler_params=_params("arbitrary", "arbitrary"),
        name="dsa",
    )(ki, qim, wit, ka, vat, qam, ga, tab)


def _stick_kernel(kb_ref, vbt_ref, qbm_ref, gb_ref, u_ref, x_ref, oa_ref, wa_ref, wb_ref, y_ref,
                  below_ref, acc_ref, z_ref, lb_ref, *, n_heads):
    tq = y_ref.shape[1]
    ts = tq
    q_blk = pl.program_id(1)
    s_rel = lax.broadcasted_iota(I32, (ts, tq), 0)
    t_rel = lax.broadcasted_iota(I32, (ts, tq), 1)

    below_ref[...] = jnp.zeros(below_ref.shape, F32)
    acc_ref[...] = jnp.zeros(acc_ref.shape, F32)

    def block(j, diagonal):
        s0 = pl.multiple_of(j * ts, ts)
        strict = s_rel < t_rel

        def z_dot(h):
            return jnp.dot(kb_ref[0, pl.ds(s0, ts), _pair(h)], qbm_ref[0, h],
                           preferred_element_type=F32)

        def terms(z2):
            log_beta = jnp.minimum(z2, 0.0) - jnp.log2(1.0 + jnp.exp2(-jnp.abs(z2)))
            log_om = log_beta - z2
            if diagonal:
                log_om = jnp.where(strict, log_om, 0.0)
            suffix = jnp.dot(u_ref[...], log_om.astype(BF16), preferred_element_type=F32)
            return log_beta, log_om[0:1, :], suffix

        def finish(h, log_a, below):
            rows = slice(h * HEAD_DIM, (h + 1) * HEAD_DIM)
            a = jnp.exp2(log_a)
            if diagonal:
                a = jnp.where(strict, a, 0.0)
            pv = jnp.dot(vbt_ref[0, j, rows, :], a.astype(BF16), preferred_element_type=F32)
            acc_ref[rows, :] += pv * jnp.exp2(below)

        for h in range(n_heads):
            z_ref[h] = z_dot(h)
        below, worst = [], None
        for h in range(n_heads):
            log_beta, om_row, suffix = terms(z_ref[h])
            lb_ref[h] = log_beta + suffix
            below.append(below_ref[h])
            after = below[h] + om_row + suffix[0:1, :]
            below_ref[h] = after
            worst = after if worst is None else jnp.maximum(worst, after)
        top = jnp.max(worst)
        for h in range(n_heads):
            finish(h, lb_ref[h], below[h])
        return top

    top = block(q_blk, diagonal=True)

    def cond(carry):
        i, top = carry
        return (i <= q_blk) & (top >= EXP2_UNDERFLOW)

    def body(carry):
        i, _ = carry
        return i + 1, block(q_blk - i, diagonal=False)

    lax.while_loop(cond, body, (jnp.int32(1), top))

    ob = (acc_ref[...].T * gb_ref[0]).astype(BF16)
    y_ref[0] = (x_ref[0]
                + jnp.dot(oa_ref[0], wa_ref[...], preferred_element_type=F32)
                + jnp.dot(ob, wb_ref[...], preferred_element_type=F32))


def _stick(kb, vbt, qbm, gb, u2, x, oa, wa, wb):
    b, l, d_b = kb.shape
    d = x.shape[2]
    n_heads = d_b // HEAD_DIM
    tq = ATT_TILE
    nq = l // tq
    return pl.pallas_call(
        functools.partial(_stick_kernel, n_heads=n_heads),
        grid=(b, nq),
        in_specs=[
            pl.BlockSpec((1, l, d_b), lambda i, j: (i, 0, 0)),
            pl.BlockSpec((1, nq, d_b, tq), lambda i, j: (i, 0, 0, 0)),
            pl.BlockSpec((1, n_heads, LANES, tq), lambda i, j: (i, 0, 0, j)),
            pl.BlockSpec((1, tq, d_b), lambda i, j: (i, j, 0)),
            pl.BlockSpec(u2.shape, lambda i, j: (0, 0)),
            pl.BlockSpec((1, tq, d), lambda i, j: (i, j, 0)),
            pl.BlockSpec((1, tq, oa.shape[2]), lambda i, j: (i, j, 0)),
            pl.BlockSpec(wa.shape, lambda i, j: (0, 0)),
            pl.BlockSpec(wb.shape, lambda i, j: (0, 0)),
        ],
        out_specs=pl.BlockSpec((1, tq, d), lambda i, j: (i, j, 0)),
        out_shape=jax.ShapeDtypeStruct((b, l, d), F32),
        scratch_shapes=[
            pltpu.VMEM((n_heads, 1, tq), F32),
            pltpu.VMEM((d_b, tq), F32),
            pltpu.VMEM((n_heads, tq, tq), F32),
            pltpu.VMEM((n_heads, tq, tq), F32),
        ],
        compiler_params=_params("arbitrary", "arbitrary"),
        name="stick",
    )(kb, vbt, qbm, gb, u2, x, oa, wa, wb)


def kernel(x, norm_gain, w_in, q_norm_gain, k_norm_gain, rel_bias, w_out):
    b, l, d = x.shape
    depth = w_in.shape[0]
    d_a = d // 2
    d_b = d - d_a
    h_a = d_a // HEAD_DIM
    d_qi = IDX_HEADS * IDX_DIM
    topk = min(TOPK_MAX, l // 4)
    ts = ATT_TILE
    scale = HEAD_DIM ** -0.5 * LOG2E
    idx_scale = (IDX_HEADS * IDX_DIM) ** -0.5

    lane = jnp.arange(LANES)
    gsum = (lane[:, None] // HEAD_DIM == lane[None, :] // HEAD_DIM).astype(BF16)
    u2 = (jnp.arange(ts)[None, :] > jnp.arange(ts)[:, None]).astype(BF16)
    tab = _t5_table(rel_bias.astype(F32), ts)

    for layer in range(depth):
        w_all = _pack_w_in(jnp.swapaxes(w_in[layer], 0, 1), d_a, d_b, d_qi, scale, idx_scale)
        qg = jnp.tile(q_norm_gain[layer] * scale, h_a)[None, :]
        kg = jnp.tile(k_norm_gain[layer], h_a)[None, :]

        (qam, ka, vat, ga, qim, ki, wit, qbm, kb, vbt, gb) = _inproj(
            x, norm_gain[layer][None, :], w_all, qg, kg, gsum, d_a, d_b, d_qi)
        oa = _dsa(ki, qim, wit, ka, vat, qam, ga, tab, topk)
        w_o = w_out[layer].astype(BF16)
        x = _stick(kb, vbt, qbm, gb, u2, x, oa, w_o[:d_a], w_o[d_a:])
    return x
```

```python
import functools
import math

import jax
import jax.numpy as jnp
import numpy as np
from jax import lax
from jax.experimental import pallas as pl
from jax.experimental.pallas import tpu as pltpu

F32 = jnp.float32
BF16 = jnp.bfloat16
I32 = jnp.int32

HEAD_DIM = 64
IDX_HEADS = 16
IDX_DIM = 64
TOPK_MAX = 256
NUM_BUCKETS = 32
MAX_DISTANCE = 128
RMS_EPS = 1e-6

LANES = 128
NEG_BIG = -1e30
EXP2_UNDERFLOW = -150.0
LOG2E = math.log2(math.e)
BISECT_CAP = 300
BISECT_BLIND = 20
BISECT_GROUP = 2

VMEM_LIMIT_BYTES = 56 * 1024 * 1024
TOKEN_TILE = 512
ATT_TILE = 256


def _params(*sem):
    return pltpu.CompilerParams(dimension_semantics=sem, vmem_limit_bytes=VMEM_LIMIT_BYTES)


def _pair(h):
    return slice((h // 2) * LANES, (h // 2 + 1) * LANES)


def _store_head_masked(dst_ref, src):
    rows = src.shape[0]
    low = lax.broadcasted_iota(I32, (rows, LANES), 1) < HEAD_DIM
    for h in range(dst_ref.shape[1]):
        keep = low if h % 2 == 0 else jnp.logical_not(low)
        dst_ref[0, h] = jnp.where(keep, src[:, _pair(h)], 0.0).T.astype(BF16)


def _pack_w_in_kernel(wt_ref, o_ref, *, d_a, d_b, d_qi, scale, idx_scale):
    o_ki = 4 * d_a + d_qi
    o_wi = o_ki + IDX_DIM
    o_b = o_wi + IDX_HEADS
    d = wt_ref.shape[1]

    def put(dst, rows):
        o_ref[:, dst:dst + LANES] = rows.T.astype(BF16)

    for c in range(0, o_ki, LANES):
        put(c, wt_ref[c:c + LANES, :])
    k_idx = wt_ref[o_ki:o_wi, :]
    put(o_ki, jnp.concatenate([k_idx, k_idx], axis=0))
    w_idx = wt_ref[o_wi:o_b, :] * idx_scale
    put(o_ki + LANES, jnp.concatenate([w_idx, jnp.zeros((LANES - IDX_HEADS, d), F32)], axis=0))
    for c in range(0, 4 * d_b, LANES):
        rows = wt_ref[o_b + c:o_b + c + LANES, :]
        put(o_ki + 2 * LANES + c, rows * scale if c < d_b else rows)


def _pack_w_in(w_t, d_a, d_b, d_qi, scale, idx_scale):
    p, d = w_t.shape
    width = p + 2 * LANES - IDX_DIM - IDX_HEADS
    return pl.pallas_call(
        functools.partial(_pack_w_in_kernel, d_a=d_a, d_b=d_b, d_qi=d_qi, scale=scale,
                          idx_scale=idx_scale),
        out_shape=jax.ShapeDtypeStruct((d, width), BF16),
        compiler_params=pltpu.CompilerParams(vmem_limit_bytes=VMEM_LIMIT_BYTES),
        name="pack_w_in",
    )(w_t)


def _inproj_kernel(x_ref, gain_ref, w_ref, qg_ref, kg_ref, gsum_ref,
                   qa_ref, ka_ref, vat_ref, ga_ref, qi_ref, ki_ref, wit_ref,
                   qb_ref, kb_ref, vbt_ref, gb_ref, *, d_a, d_b, d_qi):
    x = x_ref[0]
    ms = jnp.mean(x * x, axis=-1, keepdims=True)
    h = (x * lax.rsqrt(ms + RMS_EPS) * gain_ref[...]).astype(BF16)
    ts = vat_ref.shape[3]

    def proj(c0, width):
        return jnp.dot(h, w_ref[:, c0:c0 + width], preferred_element_type=F32)

    def head_norm(y, g):
        sq = (y * y).astype(BF16)
        ones = gsum_ref[...]
        ssum = jnp.concatenate(
            [jnp.dot(sq[:, c:c + LANES], ones, preferred_element_type=F32)
             for c in range(0, y.shape[1], LANES)], axis=1)
        return y * lax.rsqrt(ssum * (1.0 / HEAD_DIM) + RMS_EPS) * g

    def silu(g):
        return g * (1.0 / (1.0 + jnp.exp(-g)))

    def store_key_blocks_t(dst_ref, v):
        vt = v.T.astype(BF16)
        for c in range(dst_ref.shape[1]):
            dst_ref[0, c] = vt[:, c * ts:(c + 1) * ts]

    c = 0
    _store_head_masked(qa_ref, head_norm(proj(c, d_a), qg_ref[...])); c += d_a
    ka_ref[0] = head_norm(proj(c, d_a), kg_ref[...]).astype(BF16); c += d_a
    store_key_blocks_t(vat_ref, proj(c, d_a)); c += d_a
    ga_ref[0] = silu(proj(c, d_a)); c += d_a
    _store_head_masked(qi_ref, proj(c, d_qi)); c += d_qi
    ki_ref[0] = proj(c, LANES).astype(BF16); c += LANES
    wit_ref[0] = proj(c, LANES).T[:IDX_HEADS, :]; c += LANES
    _store_head_masked(qb_ref, proj(c, d_b)); c += d_b
    kb_ref[0] = proj(c, d_b).astype(BF16); c += d_b
    store_key_blocks_t(vbt_ref, proj(c, d_b)); c += d_b
    gb_ref[0] = silu(proj(c, d_b))


def _inproj(x, gain, w_all, qg, kg, gsum, d_a, d_b, d_qi):
    b, l, d = x.shape
    tm = TOKEN_TILE
    ts = ATT_TILE
    nb = l // ts
    row = lambda width: pl.BlockSpec((1, tm, width), lambda i, j: (i, j, 0))
    const = lambda shape: pl.BlockSpec(shape, lambda i, j: (0, 0))
    vt_spec = lambda ch: pl.BlockSpec((1, tm // ts, ch, ts), lambda i, j: (i, j, 0, 0))
    tok = lambda width, dt: jax.ShapeDtypeStruct((b, l, width), dt)
    heads = lambda width: jax.ShapeDtypeStruct((b, width // HEAD_DIM, LANES, l), BF16)
    heads_spec = lambda width: pl.BlockSpec((1, width // HEAD_DIM, LANES, tm), lambda i, j: (i, 0, 0, j))
    out_shapes = [
        heads(d_a), tok(d_a, BF16), jax.ShapeDtypeStruct((b, nb, d_a, ts), BF16), tok(d_a, F32),
        heads(d_qi), tok(LANES, BF16), jax.ShapeDtypeStruct((b, IDX_HEADS, l), F32),
        heads(d_b), tok(d_b, BF16), jax.ShapeDtypeStruct((b, nb, d_b, ts), BF16), tok(d_b, F32),
    ]
    out_specs = [heads_spec(d_a), row(d_a), vt_spec(d_a), row(d_a), heads_spec(d_qi), row(LANES),
                 pl.BlockSpec((1, IDX_HEADS, tm), lambda i, j: (i, 0, j)),
                 heads_spec(d_b), row(d_b), vt_spec(d_b), row(d_b)]
    return pl.pallas_call(
        functools.partial(_inproj_kernel, d_a=d_a, d_b=d_b, d_qi=d_qi),
        grid=(b, l // tm),
        in_specs=[row(d), const(gain.shape),
                  pl.BlockSpec(w_all.shape, lambda i, j: (0, 0), pipeline_mode=pl.Buffered(1)),
                  const(qg.shape), const(kg.shape), const(gsum.shape)],
        out_specs=out_specs,
        out_shape=out_shapes,
        compiler_params=_params("arbitrary", "arbitrary"),
        name="inproj",
    )(x, gain, w_all, qg, kg, gsum)


def _t5_large_thresholds():
    max_exact = NUM_BUCKETS // 2
    d = np.arange(max_exact, 2 * MAX_DISTANCE + 1)
    large = max_exact + (np.log(d.astype(np.float32) / np.float32(max_exact))
                         / np.float32(math.log(MAX_DISTANCE / max_exact))
                         * np.float32(NUM_BUCKETS - max_exact)).astype(np.int32)
    large = np.minimum(large, NUM_BUCKETS - 1)
    return [int(d[np.argmax(large >= k)]) for k in range(max_exact + 1, NUM_BUCKETS)]


def _t5_bucket(dist):
    max_exact = NUM_BUCKETS // 2
    d = jnp.maximum(dist, 0)
    large = jnp.full(d.shape, max_exact, I32)
    for first in _t5_large_thresholds():
        large = large + (d >= first).astype(I32)
    return jnp.where(d < max_exact, d, large)


def _t5_table_kernel(rb_ref, tab_ref, *, n_heads):
    ts = tab_ref.shape[2]
    rows = 16
    far = _t5_large_thresholds()[-1]
    t_rel = lax.broadcasted_iota(I32, (rows, ts), 1)
    s_rel = lax.broadcasted_iota(I32, (rows, ts), 0)

    for i in range(3):
        n_far = max(0, min(ts, i * ts - far + 1)) // rows

        def fill(c, _, i=i):
            r0 = pl.multiple_of(c * rows, rows)
            for h in range(n_heads):
                tab_ref[i, h, pl.ds(r0, rows), :] = jnp.full((rows, ts), rb_ref[NUM_BUCKETS - 1, h] * LOG2E, F32)
            return 0

        def compute(c, _, i=i):
            r0 = pl.multiple_of(c * rows, rows)
            bucket = _t5_bucket(t_rel - (r0 + s_rel) + i * ts)
            acc = [jnp.zeros((rows, ts), F32) for _ in range(n_heads)]
            for j in range(NUM_BUCKETS):
                hit = bucket == j
                acc = [jnp.where(hit, rb_ref[j, h] * LOG2E, acc[h]) for h in range(n_heads)]
            for h in range(n_heads):
                tab_ref[i, h, pl.ds(r0, rows), :] = acc[h]
            return 0

        lax.fori_loop(0, n_far, fill, 0)
        lax.fori_loop(n_far, ts // rows, compute, 0)


def _t5_table(rel_bias, ts):
    n_heads = rel_bias.shape[1]
    return pl.pallas_call(
        functools.partial(_t5_table_kernel, n_heads=n_heads),
        in_specs=[pl.BlockSpec(memory_space=pltpu.SMEM)],
        out_specs=pl.BlockSpec(memory_space=pltpu.VMEM),
        out_shape=jax.ShapeDtypeStruct((3, n_heads, ts, ts), F32),
        compiler_params=pltpu.CompilerParams(vmem_limit_bytes=VMEM_LIMIT_BYTES),
        name="t5_table",
    )(rel_bias)


def _dsa_kernel(ki_ref, qin_ref, witn_ref, ka_ref, vat_ref, qam_ref, ga_ref, tab_ref, o_ref,
                sc_ref, mm_ref, madd_ref, m_ref, l_ref, acc_ref, lg_ref, bmax_ref, *, topk, n_heads):
    tq = o_ref.shape[1]
    ts = tq
    q_blk = pl.program_id(1)
    nblk = q_blk + 1
    cur = q_blk % 2
    nxt = 1 - cur
    s_rel = lax.broadcasted_iota(I32, (ts, tq), 0)
    t_rel = lax.broadcasted_iota(I32, (ts, tq), 1)
    t_idx = q_blk * tq + t_rel

    def block_start(j):
        return pl.multiple_of(j * ts, ts)

    all_selected = nblk * tq <= topk
    n_sel = jnp.where(all_selected, 0, nblk)

    @pl.when(all_selected)
    def _():
        mm_ref[cur] = jnp.zeros(mm_ref.shape[1:], F32)

        def body(j, _):
            s0 = block_start(j)
            madd_ref[pl.ds(s0, ts), :] = jnp.where(s0 + s_rel > t_idx, NEG_BIG, 0.0)
            return 0

        lax.fori_loop(0, nblk, body, 0)

    vmin = jnp.min(mm_ref[cur, 0:8, :], axis=0, keepdims=True)
    vmax = jnp.max(mm_ref[cur, 8:16, :], axis=0, keepdims=True)

    def count(pred_fn):
        def body(j, c):
            s0 = block_start(j)
            hit = pred_fn(sc_ref[cur, pl.ds(s0, ts), :], s0).reshape(ts // 8, 8, tq)
            c = list(c)
            for r in range(ts // 8):
                c[r % 4] = jnp.where(hit[r], c[r % 4] + 1, c[r % 4])
            return tuple(c)
        z = jnp.zeros((8, tq), I32)
        c8 = lax.fori_loop(0, n_sel, body, (z, z, z, z))
        return jnp.sum(c8[0] + c8[1] + c8[2] + c8[3], axis=0, keepdims=True)

    def midpoint(lo, hi):
        return 0.5 * lo + 0.5 * hi

    def searching(lo, hi, cnt_lo):
        mid = midpoint(lo, hi)
        return (cnt_lo > topk) & (mid > lo) & (mid < hi)

    def status(lo, hi, cnt_lo):
        flags = 2 * searching(lo, hi, cnt_lo).astype(I32) + (cnt_lo > topk).astype(I32)
        return jnp.max(flags)

    n_causal = t_idx[0:1, :] + 1
    lo0 = jnp.where(n_causal > topk, vmin, -jnp.inf)
    hi0 = vmax + (jnp.abs(vmax) * 2.0 ** -20 + 1e-30)

    def bisect_cond(carry):
        it, _, _, _, state = carry
        return (state >= 2) & (it < BISECT_CAP)

    def bisect_step(_, carry):
        lo, hi, cnt_lo = carry
        go = searching(lo, hi, cnt_lo)
        mid = midpoint(lo, hi)
        cnt = count(lambda blk, s0: blk >= mid)
        take = go & (cnt >= topk)
        return (jnp.where(take, mid, lo), jnp.where(go & (cnt < topk), mid, hi),
                jnp.where(take, cnt, cnt_lo))

    def bisect_body(carry):
        it, lo, hi, cnt_lo, _ = carry
        lo, hi, cnt_lo = lax.fori_loop(0, BISECT_GROUP, bisect_step, (lo, hi, cnt_lo))
        return it + BISECT_GROUP, lo, hi, cnt_lo, status(lo, hi, cnt_lo)

    n_blind = jnp.where(all_selected, 0, BISECT_BLIND)
    lo, hi, cnt_lo = lax.fori_loop(0, n_blind, bisect_step, (lo0, hi0, n_causal))
    _, thr, hi, cnt_thr, state = lax.while_loop(
        bisect_cond, bisect_body,
        (jnp.int32(BISECT_BLIND), lo, hi, cnt_lo,
         jnp.where(all_selected, 0, status(lo, hi, cnt_lo))))

    tie = state >= 1

    @pl.when(jnp.logical_not(tie))
    def _():
        def body(j, _):
            s0 = block_start(j)
            sel = (sc_ref[cur, pl.ds(s0, ts), :] >= thr) & (s0 + s_rel <= t_idx)
            madd_ref[pl.ds(s0, ts), :] = jnp.where(sel, 0.0, NEG_BIG)
            return 0

        lax.fori_loop(0, n_sel, body, 0)

    @pl.when(tie)
    def _():
        need = topk - count(lambda blk, s0: blk > thr)
        n_bits = sc_ref.shape[1].bit_length()

        def jbody(i, bound):
            cand = bound + jnp.left_shift(jnp.int32(1), n_bits - 1 - i)
            cnt = count(lambda blk, s0: (blk == thr) & (s0 + s_rel < cand))
            return jnp.where(cnt <= need, cand, bound)

        bound = lax.fori_loop(0, n_bits, jbody, jnp.zeros((1, tq), I32))

        def body(j, _):
            s0 = block_start(j)
            blk = sc_ref[cur, pl.ds(s0, ts), :]
            sel = ((blk > thr) | ((blk == thr) & (s0 + s_rel < bound))) & (s0 + s_rel <= t_idx)
            madd_ref[pl.ds(s0, ts), :] = jnp.where(sel, 0.0, NEG_BIG)
            return 0

        lax.fori_loop(0, n_sel, body, 0)

    m_ref[...] = jnp.full(m_ref.shape, NEG_BIG, F32)
    l_ref[...] = jnp.zeros(l_ref.shape, F32)
    acc_ref[...] = jnp.zeros(acc_ref.shape, F32)

    def attend(j):
        s0 = block_start(j)
        madd = madd_ref[pl.ds(s0, ts), :]
        near = jnp.minimum(q_blk - j, 2)
        for h in range(n_heads):
            lg = jnp.dot(ka_ref[0, pl.ds(s0, ts), _pair(h)], qam_ref[0, h],
                         preferred_element_type=F32)
            lg = lg + tab_ref[near, h] + madd
            lg_ref[h] = lg
            bmax_ref[h] = jnp.max(lg, axis=0, keepdims=True)
        for h in range(n_heads):
            rows = slice(h * HEAD_DIM, (h + 1) * HEAD_DIM)
            m_old = m_ref[h]
            m_new = jnp.maximum(m_old, bmax_ref[h])
            alpha = jnp.exp2(m_old - m_new)
            p = jnp.exp2(lg_ref[h] - m_new)
            l_ref[h] = alpha * l_ref[h] + jnp.sum(p, axis=0, keepdims=True)
            pv = jnp.dot(vat_ref[0, j, rows, :], p.astype(BF16), preferred_element_type=F32)
            acc_ref[rows, :] = alpha * acc_ref[rows, :] + pv
            m_ref[h] = m_new

    def index_next(j, carry, diagonal):
        vmin8, vmax8 = carry
        s0 = block_start(j)
        kblk = ki_ref[0, pl.ds(s0, ts), :]
        score = jnp.zeros((ts, tq), F32)
        for h in range(IDX_HEADS):
            dots = jnp.dot(kblk, qin_ref[0, h], preferred_element_type=F32)
            score = score + jnp.maximum(dots, 0.0) * witn_ref[0, h:h + 1, :]
        if diagonal:
            future = s_rel > t_rel
            low = jnp.where(future, jnp.inf, score)
            score = jnp.where(future, -jnp.inf, score)
        else:
            low = score
        sc_ref[nxt, pl.ds(s0, ts), :] = score
        return (jnp.minimum(vmin8, low.reshape(ts // 8, 8, tq).min(axis=0)),
                jnp.maximum(vmax8, score.reshape(ts // 8, 8, tq).max(axis=0)))

    def write_output():
        for h in range(n_heads):
            rows = slice(h * HEAD_DIM, (h + 1) * HEAD_DIM)
            acc_ref[rows, :] = acc_ref[rows, :] * (1.0 / l_ref[h])
        o_ref[0] = (acc_ref[...].T * ga_ref[0]).astype(BF16)

    has_next = q_blk + 1 < pl.num_programs(1)

    @pl.when(has_next)
    def _():
        def pair(i, carry):
            for j in (2 * i, 2 * i + 1):
                attend(j)
                carry = index_next(j, carry, diagonal=False)
            return carry

        def single(j, carry):
            attend(j)
            return index_next(j, carry, diagonal=False)

        carry = lax.fori_loop(0, nblk // 2, pair, (jnp.full((8, tq), jnp.inf, F32),
                                                   jnp.full((8, tq), -jnp.inf, F32)))
        carry = lax.fori_loop(2 * (nblk // 2), nblk, single, carry)
        vmin8, vmax8 = index_next(nblk, carry, diagonal=True)
        mm_ref[nxt, 0:8, :] = vmin8
        mm_ref[nxt, 8:16, :] = vmax8
        write_output()

    @pl.when(jnp.logical_not(has_next))
    def _():
        def body(j, _):
            attend(j)
            return 0

        lax.fori_loop(0, nblk, body, 0)
        write_output()


def _dsa(ki, qim, wit, ka, vat, qam, ga, tab, topk):
    b, l, d_a = ka.shape
    n_heads = d_a // HEAD_DIM
    tq = ATT_TILE
    nq = l // tq
    assert tq <= topk, "the first query tile must not need indexer scores"
    next_tile = lambda j: jnp.minimum(j + 1, nq - 1)
    kernel = functools.partial(_dsa_kernel, topk=topk, n_heads=n_heads)
    return pl.pallas_call(
        kernel,
        grid=(b, nq),
        in_specs=[
            pl.BlockSpec((1, l, LANES), lambda i, j: (i, 0, 0)),
            pl.BlockSpec((1, IDX_HEADS, LANES, tq), lambda i, j: (i, 0, 0, next_tile(j))),
            pl.BlockSpec((1, IDX_HEADS, tq), lambda i, j: (i, 0, next_tile(j))),
            pl.BlockSpec((1, l, d_a), lambda i, j: (i, 0, 0)),
            pl.BlockSpec((1, nq, d_a, tq), lambda i, j: (i, 0, 0, 0)),
            pl.BlockSpec((1, n_heads, LANES, tq), lambda i, j: (i, 0, 0, j)),
            pl.BlockSpec((1, tq, d_a), lambda i, j: (i, j, 0)),
            pl.BlockSpec(tab.shape, lambda i, j: (0, 0, 0, 0), pipeline_mode=pl.Buffered(1)),
        ],
        out_specs=pl.BlockSpec((1, tq, d_a), lambda i, j: (i, j, 0)),
        out_shape=jax.ShapeDtypeStruct((b, l, d_a), BF16),
        scratch_shapes=[
            pltpu.VMEM((2, l, tq), F32),
            pltpu.VMEM((2, 16, tq), F32),
            pltpu.VMEM((l, tq), F32),
            pltpu.VMEM((n_heads, 1, tq), F32),
            pltpu.VMEM((n_heads, 1, tq), F32),
            pltpu.VMEM((d_a, tq), F32),
            pltpu.VMEM((n_heads, tq, tq), F32),
            pltpu.VMEM((n_heads, 1, tq), F32),
        ],
        compiler_params=_params("arbitrary", "arbitrary"),
        name="dsa",
    )(ki, qim, wit, ka, vat, qam, ga, tab)


def _stick_kernel(kb_ref, vbt_ref, qbm_ref, gb_ref, u_ref, x_ref, oa_ref, wa_ref, wb_ref, y_ref,
                  below_ref, acc_ref, z_ref, lb_ref, *, n_heads):
    tq = y_ref.shape[1]
    ts = tq
    q_blk = pl.program_id(1)
    s_rel = lax.broadcasted_iota(I32, (ts, tq), 0)
    t_rel = lax.broadcasted_iota(I32, (ts, tq), 1)

    below_ref[...] = jnp.zeros(below_ref.shape, F32)
    acc_ref[...] = jnp.zeros(acc_ref.shape, F32)

    def block(j, diagonal):
        s0 = pl.multiple_of(j * ts, ts)
        strict = s_rel < t_rel

        def z_dot(h):
            return jnp.dot(kb_ref[0, pl.ds(s0, ts), _pair(h)], qbm_ref[0, h],
                           preferred_element_type=F32)

        def terms(z2):
            log_beta = jnp.minimum(z2, 0.0) - jnp.log2(1.0 + jnp.exp2(-jnp.abs(z2)))
            log_om = log_beta - z2
            if diagonal:
                log_om = jnp.where(strict, log_om, 0.0)
            suffix = jnp.dot(u_ref[...], log_om.astype(BF16), preferred_element_type=F32)
            return log_beta, log_om[0:1, :], suffix

        def finish(h, log_a, below):
            rows = slice(h * HEAD_DIM, (h + 1) * HEAD_DIM)
            a = jnp.exp2(log_a)
            if diagonal:
                a = jnp.where(strict, a, 0.0)
            pv = jnp.dot(vbt_ref[0, j, rows, :], a.astype(BF16), preferred_element_type=F32)
            acc_ref[rows, :] += pv * jnp.exp2(below)

        for h in range(n_heads):
            z_ref[h] = z_dot(h)
        below, worst = [], None
        for h in range(n_heads):
            log_beta, om_row, suffix = terms(z_ref[h])
            lb_ref[h] = log_beta + suffix
            below.append(below_ref[h])
            after = below[h] + om_row + suffix[0:1, :]
            below_ref[h] = after
            worst = after if worst is None else jnp.maximum(worst, after)
        top = jnp.max(worst)
        for h in range(n_heads):
            finish(h, lb_ref[h], below[h])
        return top

    top = block(q_blk, diagonal=True)

    def cond(carry):
        i, top = carry
        return (i <= q_blk) & (top >= EXP2_UNDERFLOW)

    def body(carry):
        i, _ = carry
        return i + 1, block(q_blk - i, diagonal=False)

    lax.while_loop(cond, body, (jnp.int32(1), top))

    ob = (acc_ref[...].T * gb_ref[0]).astype(BF16)
    y_ref[0] = (x_ref[0]
                + jnp.dot(oa_ref[0], wa_ref[...], preferred_element_type=F32)
                + jnp.dot(ob, wb_ref[...], preferred_element_type=F32))


def _stick(kb, vbt, qbm, gb, u2, x, oa, wa, wb):
    b, l, d_b = kb.shape
    d = x.shape[2]
    n_heads = d_b // HEAD_DIM
    tq = ATT_TILE
    nq = l // tq
    return pl.pallas_call(
        functools.partial(_stick_kernel, n_heads=n_heads),
        grid=(b, nq),
        in_specs=[
            pl.BlockSpec((1, l, d_b), lambda i, j: (i, 0, 0)),
            pl.BlockSpec((1, nq, d_b, tq), lambda i, j: (i, 0, 0, 0)),
            pl.BlockSpec((1, n_heads, LANES, tq), lambda i, j: (i, 0, 0, j)),
            pl.BlockSpec((1, tq, d_b), lambda i, j: (i, j, 0)),
            pl.BlockSpec(u2.shape, lambda i, j: (0, 0)),
            pl.BlockSpec((1, tq, d), lambda i, j: (i, j, 0)),
            pl.BlockSpec((1, tq, oa.shape[2]), lambda i, j: (i, j, 0)),
            pl.BlockSpec(wa.shape, lambda i, j: (0, 0)),
            pl.BlockSpec(wb.shape, lambda i, j: (0, 0)),
        ],
        out_specs=pl.BlockSpec((1, tq, d), lambda i, j: (i, j, 0)),
        out_shape=jax.ShapeDtypeStruct((b, l, d), F32),
        scratch_shapes=[
            pltpu.VMEM((n_heads, 1, tq), F32),
            pltpu.VMEM((d_b, tq), F32),
            pltpu.VMEM((n_heads, tq, tq), F32),
            pltpu.VMEM((n_heads, tq, tq), F32),
        ],
        compiler_params=_params("arbitrary", "arbitrary"),
        name="stick",
    )(kb, vbt, qbm, gb, u2, x, oa, wa, wb)


def kernel(x, norm_gain, w_in, q_norm_gain, k_norm_gain, rel_bias, w_out):
    b, l, d = x.shape
    depth = w_in.shape[0]
    d_a = d // 2
    d_b = d - d_a
    h_a = d_a // HEAD_DIM
    d_qi = IDX_HEADS * IDX_DIM
    topk = min(TOPK_MAX, l // 4)
    ts = ATT_TILE
    scale = HEAD_DIM ** -0.5 * LOG2E
    idx_scale = (IDX_HEADS * IDX_DIM) ** -0.5

    lane = jnp.arange(LANES)
    gsum = (lane[:, None] // HEAD_DIM == lane[None, :] // HEAD_DIM).astype(BF16)
    u2 = (jnp.arange(ts)[None, :] > jnp.arange(ts)[:, None]).astype(BF16)
    tab = _t5_table(rel_bias.astype(F32), ts)

    for layer in range(depth):
        w_all = _pack_w_in(jnp.swapaxes(w_in[layer], 0, 1), d_a, d_b, d_qi, scale, idx_scale)
        qg = jnp.tile(q_norm_gain[layer] * scale, h_a)[None, :]
        kg = jnp.tile(k_norm_gain[layer], h_a)[None, :]

        (qam, ka, vat, ga, qim, ki, wit, qbm, kb, vbt, gb) = _inproj(
            x, norm_gain[layer][None, :], w_all, qg, kg, gsum, d_a, d_b, d_qi)
        oa = _dsa(ki, qim, wit, ka, vat, qam, ga, tab, topk)
        w_o = w_out[layer].astype(BF16)
        x = _stick(kb, vbt, qbm, gb, u2, x, oa, w_o[:d_a], w_o[d_a:])
    return x
```

```python
import functools
import math

import jax
import jax.numpy as jnp
import numpy as np
from jax import lax
from jax.experimental import pallas as pl
from jax.experimental.pallas import tpu as pltpu

F32 = jnp.float32
BF16 = jnp.bfloat16
I32 = jnp.int32

HEAD_DIM = 64
IDX_HEADS = 16
IDX_DIM = 64
TOPK_MAX = 256
NUM_BUCKETS = 32
MAX_DISTANCE = 128
RMS_EPS = 1e-6

LANES = 128
SUBLANES = 8
NEG_BIG = -1e30
EXP2_UNDERFLOW = -150.0
LOG2E = math.log2(math.e)
BISECT_CAP = 300
BISECT_BLIND = 20
BISECT_GROUP = 2

VMEM_LIMIT_BYTES = 56 * 1024 * 1024
TOKEN_TILE = 512
ATT_TILE = 256


def _params(*sem):
    return pltpu.CompilerParams(dimension_semantics=sem, vmem_limit_bytes=VMEM_LIMIT_BYTES)


def _pair(h):
    return slice((h // 2) * LANES, (h // 2 + 1) * LANES)


def _store_head_masked(dst_ref, src):
    rows = src.shape[0]
    low = lax.broadcasted_iota(I32, (rows, LANES), 1) < HEAD_DIM
    for h in range(dst_ref.shape[1]):
        keep = low if h % 2 == 0 else jnp.logical_not(low)
        dst_ref[0, h] = jnp.where(keep, src[:, _pair(h)], 0.0).T.astype(BF16)


def _pack_w_in_kernel(wt_ref, o_ref, *, d_a, d_b, d_qi, scale, idx_scale):
    o_ki = 4 * d_a + d_qi
    o_wi = o_ki + IDX_DIM
    o_b = o_wi + IDX_HEADS
    d = wt_ref.shape[1]

    def put(dst, rows):
        o_ref[:, dst:dst + LANES] = rows.T.astype(BF16)

    for c in range(0, o_ki, LANES):
        put(c, wt_ref[c:c + LANES, :])
    k_idx = wt_ref[o_ki:o_wi, :]
    put(o_ki, jnp.concatenate([k_idx, k_idx], axis=0))
    w_idx = wt_ref[o_wi:o_b, :] * idx_scale
    put(o_ki + LANES, jnp.concatenate([w_idx, jnp.zeros((LANES - IDX_HEADS, d), F32)], axis=0))
    for c in range(0, 4 * d_b, LANES):
        rows = wt_ref[o_b + c:o_b + c + LANES, :]
        put(o_ki + 2 * LANES + c, rows * scale if c < d_b else rows)


def _pack_w_in(w_t, d_a, d_b, d_qi, scale, idx_scale):
    p, d = w_t.shape
    width = p + 2 * LANES - IDX_DIM - IDX_HEADS
    return pl.pallas_call(
        functools.partial(_pack_w_in_kernel, d_a=d_a, d_b=d_b, d_qi=d_qi, scale=scale,
                          idx_scale=idx_scale),
        out_shape=jax.ShapeDtypeStruct((d, width), BF16),
        compiler_params=pltpu.CompilerParams(vmem_limit_bytes=VMEM_LIMIT_BYTES),
        name="pack_w_in",
    )(w_t)


def _inproj_kernel(x_ref, gain_ref, w_ref, qg_ref, kg_ref, gsum_ref,
                   qa_ref, ka_ref, vat_ref, ga_ref, qi_ref, ki_ref, wit_ref,
                   qb_ref, kb_ref, vbt_ref, gb_ref, *, d_a, d_b, d_qi):
    x = x_ref[0]
    ms = jnp.mean(x * x, axis=-1, keepdims=True)
    h = (x * lax.rsqrt(ms + RMS_EPS) * gain_ref[...]).astype(BF16)
    ts = vat_ref.shape[3]

    def proj(c0, width):
        return jnp.dot(h, w_ref[:, c0:c0 + width], preferred_element_type=F32)

    def head_norm(y, g):
        sq = (y * y).astype(BF16)
        ones = gsum_ref[...]
        ssum = jnp.concatenate(
            [jnp.dot(sq[:, c:c + LANES], ones, preferred_element_type=F32)
             for c in range(0, y.shape[1], LANES)], axis=1)
        return y * lax.rsqrt(ssum * (1.0 / HEAD_DIM) + RMS_EPS) * g

    def silu(g):
        return g * (1.0 / (1.0 + jnp.exp(-g)))

    def store_key_blocks_t(dst_ref, v):
        vt = v.T.astype(BF16)
        for c in range(dst_ref.shape[1]):
            dst_ref[0, c] = vt[:, c * ts:(c + 1) * ts]

    c = 0
    _store_head_masked(qa_ref, head_norm(proj(c, d_a), qg_ref[...])); c += d_a
    ka_ref[0] = head_norm(proj(c, d_a), kg_ref[...]).astype(BF16); c += d_a
    store_key_blocks_t(vat_ref, proj(c, d_a)); c += d_a
    ga_ref[0] = silu(proj(c, d_a)); c += d_a
    _store_head_masked(qi_ref, proj(c, d_qi)); c += d_qi
    ki_ref[0] = proj(c, LANES).astype(BF16); c += LANES
    wit_ref[0] = proj(c, LANES).T[:IDX_HEADS, :]; c += LANES
    _store_head_masked(qb_ref, proj(c, d_b)); c += d_b
    kb_ref[0] = proj(c, d_b).astype(BF16); c += d_b
    store_key_blocks_t(vbt_ref, proj(c, d_b)); c += d_b
    gb_ref[0] = silu(proj(c, d_b))


def _inproj(x, gain, w_all, qg, kg, gsum, d_a, d_b, d_qi):
    b, l, d = x.shape
    tm = TOKEN_TILE
    ts = ATT_TILE
    nb = l // ts
    row = lambda width: pl.BlockSpec((1, tm, width), lambda i, j: (i, j, 0))
    const = lambda shape: pl.BlockSpec(shape, lambda i, j: (0, 0))
    vt_spec = lambda ch: pl.BlockSpec((1, tm // ts, ch, ts), lambda i, j: (i, j, 0, 0))
    tok = lambda width, dt: jax.ShapeDtypeStruct((b, l, width), dt)
    heads = lambda width: jax.ShapeDtypeStruct((b, width // HEAD_DIM, LANES, l), BF16)
    heads_spec = lambda width: pl.BlockSpec((1, width // HEAD_DIM, LANES, tm), lambda i, j: (i, 0, 0, j))
    out_shapes = [
        heads(d_a), tok(d_a, BF16), jax.ShapeDtypeStruct((b, nb, d_a, ts), BF16), tok(d_a, F32),
        heads(d_qi), tok(LANES, BF16), jax.ShapeDtypeStruct((b, IDX_HEADS, l), F32),
        heads(d_b), tok(d_b, BF16), jax.ShapeDtypeStruct((b, nb, d_b, ts), BF16), tok(d_b, F32),
    ]
    out_specs = [heads_spec(d_a), row(d_a), vt_spec(d_a), row(d_a), heads_spec(d_qi), row(LANES),
                 pl.BlockSpec((1, IDX_HEADS, tm), lambda i, j: (i, 0, j)),
                 heads_spec(d_b), row(d_b), vt_spec(d_b), row(d_b)]
    return pl.pallas_call(
        functools.partial(_inproj_kernel, d_a=d_a, d_b=d_b, d_qi=d_qi),
        grid=(b, l // tm),
        in_specs=[row(d), const(gain.shape),
                  pl.BlockSpec(w_all.shape, lambda i, j: (0, 0), pipeline_mode=pl.Buffered(1)),
                  const(qg.shape), const(kg.shape), const(gsum.shape)],
        out_specs=out_specs,
        out_shape=out_shapes,
        compiler_params=_params("arbitrary", "arbitrary"),
        name="inproj",
    )(x, gain, w_all, qg, kg, gsum)


def _t5_large_thresholds():
    max_exact = NUM_BUCKETS // 2
    d = np.arange(max_exact, 2 * MAX_DISTANCE + 1)
    large = max_exact + (np.log(d.astype(np.float32) / np.float32(max_exact))
                         / np.float32(math.log(MAX_DISTANCE / max_exact))
                         * np.float32(NUM_BUCKETS - max_exact)).astype(np.int32)
    large = np.minimum(large, NUM_BUCKETS - 1)
    return [int(d[np.argmax(large >= k)]) for k in range(max_exact + 1, NUM_BUCKETS)]


def _t5_bucket(dist):
    max_exact = NUM_BUCKETS // 2
    d = jnp.maximum(dist, 0)
    large = jnp.full(d.shape, max_exact, I32)
    for first in _t5_large_thresholds():
        large = large + (d >= first).astype(I32)
    return jnp.where(d < max_exact, d, large)


def _t5_table_kernel(rb_ref, tab_ref, *, n_heads):
    ts = tab_ref.shape[2]
    rows = 16
    far = _t5_large_thresholds()[-1]
    t_rel = lax.broadcasted_iota(I32, (rows, ts), 1)
    s_rel = lax.broadcasted_iota(I32, (rows, ts), 0)

    for i in range(3):
        n_far = max(0, min(ts, i * ts - far + 1)) // rows

        def fill(c, _, i=i):
            r0 = pl.multiple_of(c * rows, rows)
            for h in range(n_heads):
                tab_ref[i, h, pl.ds(r0, rows), :] = jnp.full((rows, ts), rb_ref[NUM_BUCKETS - 1, h] * LOG2E, F32)
            return 0

        def compute(c, _, i=i):
            r0 = pl.multiple_of(c * rows, rows)
            bucket = _t5_bucket(t_rel - (r0 + s_rel) + i * ts)
            acc = [jnp.zeros((rows, ts), F32) for _ in range(n_heads)]
            for j in range(NUM_BUCKETS):
                hit = bucket == j
                acc = [jnp.where(hit, rb_ref[j, h] * LOG2E, acc[h]) for h in range(n_heads)]
            for h in range(n_heads):
                tab_ref[i, h, pl.ds(r0, rows), :] = acc[h]
            return 0

        lax.fori_loop(0, n_far, fill, 0)
        lax.fori_loop(n_far, ts // rows, compute, 0)


def _t5_table(rel_bias, ts):
    n_heads = rel_bias.shape[1]
    return pl.pallas_call(
        functools.partial(_t5_table_kernel, n_heads=n_heads),
        in_specs=[pl.BlockSpec(memory_space=pltpu.SMEM)],
        out_specs=pl.BlockSpec(memory_space=pltpu.VMEM),
        out_shape=jax.ShapeDtypeStruct((3, n_heads, ts, ts), F32),
        compiler_params=pltpu.CompilerParams(vmem_limit_bytes=VMEM_LIMIT_BYTES),
        name="t5_table",
    )(rel_bias)


def _dsa_kernel(ki_ref, qin_ref, witn_ref, ka_ref, vat_ref, qam_ref, ga_ref, tab_ref, o_ref,
                sc_ref, mm_ref, madd_ref, m_ref, l_ref, acc_ref, lg_ref, bmax_ref, *, topk, n_heads):
    tq = o_ref.shape[1]
    ts = tq
    q_blk = pl.program_id(1)
    nblk = q_blk + 1
    cur = q_blk % 2
    nxt = 1 - cur
    s_rel = lax.broadcasted_iota(I32, (ts, tq), 0)
    t_rel = lax.broadcasted_iota(I32, (ts, tq), 1)
    t_idx = q_blk * tq + t_rel

    def block_start(j):
        return pl.multiple_of(j * ts, ts)

    all_selected = nblk * tq <= topk
    n_sel = jnp.where(all_selected, 0, nblk)

    @pl.when(all_selected)
    def _():
        mm_ref[cur] = jnp.zeros(mm_ref.shape[1:], F32)

        def body(j, _):
            s0 = block_start(j)
            madd_ref[pl.ds(s0, ts), :] = jnp.where(s0 + s_rel > t_idx, NEG_BIG, 0.0)
            return 0

        lax.fori_loop(0, nblk, body, 0)

    vmin = jnp.min(mm_ref[cur, 0], axis=0, keepdims=True)
    vmax = jnp.max(mm_ref[cur, 1], axis=0, keepdims=True)

    def count(pred_fn):
        def body(j, c):
            s0 = block_start(j)
            hit = pred_fn(sc_ref[cur, pl.ds(s0, ts), :], s0).reshape(ts // SUBLANES, SUBLANES, tq)
            c = list(c)
            for r in range(ts // SUBLANES):
                c[r % 4] = jnp.where(hit[r], c[r % 4] + 1, c[r % 4])
            return tuple(c)
        z = jnp.zeros((SUBLANES, tq), I32)
        c8 = lax.fori_loop(0, n_sel, body, (z, z, z, z))
        return jnp.sum(c8[0] + c8[1] + c8[2] + c8[3], axis=0, keepdims=True)

    def midpoint(lo, hi):
        return 0.5 * lo + 0.5 * hi

    def searching(lo, hi, cnt_lo):
        mid = midpoint(lo, hi)
        return (cnt_lo > topk) & (mid > lo) & (mid < hi)

    def status(lo, hi, cnt_lo):
        flags = 2 * searching(lo, hi, cnt_lo).astype(I32) + (cnt_lo > topk).astype(I32)
        return jnp.max(flags)

    n_causal = t_idx[0:1, :] + 1
    lo0 = jnp.where(n_causal > topk, vmin, -jnp.inf)
    hi0 = vmax + (jnp.abs(vmax) * 2.0 ** -20 + 1e-30)

    def bisect_cond(carry):
        it, _, _, _, state = carry
        return (state >= 2) & (it < BISECT_CAP)

    def bisect_step(_, carry):
        lo, hi, cnt_lo = carry
        go = searching(lo, hi, cnt_lo)
        mid = midpoint(lo, hi)
        cnt = count(lambda blk, s0: blk >= mid)
        take = go & (cnt >= topk)
        return (jnp.where(take, mid, lo), jnp.where(go & (cnt < topk), mid, hi),
                jnp.where(take, cnt, cnt_lo))

    def bisect_body(carry):
        it, lo, hi, cnt_lo, _ = carry
        lo, hi, cnt_lo = lax.fori_loop(0, BISECT_GROUP, bisect_step, (lo, hi, cnt_lo))
        return it + BISECT_GROUP, lo, hi, cnt_lo, status(lo, hi, cnt_lo)

    n_blind = jnp.where(all_selected, 0, BISECT_BLIND)
    lo, hi, cnt_lo = lax.fori_loop(0, n_blind, bisect_step, (lo0, hi0, n_causal))
    _, thr, hi, cnt_thr, state = lax.while_loop(
        bisect_cond, bisect_body,
        (jnp.int32(BISECT_BLIND), lo, hi, cnt_lo,
         jnp.where(all_selected, 0, status(lo, hi, cnt_lo))))

    tie = state >= 1

    @pl.when(jnp.logical_not(tie))
    def _():
        def body(j, _):
            s0 = block_start(j)
            sel = (sc_ref[cur, pl.ds(s0, ts), :] >= thr) & (s0 + s_rel <= t_idx)
            madd_ref[pl.ds(s0, ts), :] = jnp.where(sel, 0.0, NEG_BIG)
            return 0

        lax.fori_loop(0, n_sel, body, 0)

    @pl.when(tie)
    def _():
        need = topk - count(lambda blk, s0: blk > thr)
        n_bits = sc_ref.shape[1].bit_length()

        def jbody(i, bound):
            cand = bound + jnp.left_shift(jnp.int32(1), n_bits - 1 - i)
            cnt = count(lambda blk, s0: (blk == thr) & (s0 + s_rel < cand))
            return jnp.where(cnt <= need, cand, bound)

        bound = lax.fori_loop(0, n_bits, jbody, jnp.zeros((1, tq), I32))

        def body(j, _):
            s0 = block_start(j)
            blk = sc_ref[cur, pl.ds(s0, ts), :]
            sel = ((blk > thr) | ((blk == thr) & (s0 + s_rel < bound))) & (s0 + s_rel <= t_idx)
            madd_ref[pl.ds(s0, ts), :] = jnp.where(sel, 0.0, NEG_BIG)
            return 0

        lax.fori_loop(0, n_sel, body, 0)

    m_ref[...] = jnp.full(m_ref.shape, NEG_BIG, F32)
    l_ref[...] = jnp.zeros(l_ref.shape, F32)
    acc_ref[...] = jnp.zeros(acc_ref.shape, F32)

    def attend(j):
        s0 = block_start(j)
        madd = madd_ref[pl.ds(s0, ts), :]
        near = jnp.minimum(q_blk - j, 2)
        for h in range(n_heads):
            lg = jnp.dot(ka_ref[0, pl.ds(s0, ts), _pair(h)], qam_ref[0, h],
                         preferred_element_type=F32)
            lg = lg + tab_ref[near, h] + madd
            lg_ref[h] = lg
            bmax_ref[h] = jnp.max(lg, axis=0, keepdims=True)
        for h in range(n_heads):
            rows = slice(h * HEAD_DIM, (h + 1) * HEAD_DIM)
            m_old = m_ref[h]
            m_new = jnp.maximum(m_old, bmax_ref[h])
            alpha = jnp.exp2(m_old - m_new)
            p = jnp.exp2(lg_ref[h] - m_new)
            l_ref[h] = alpha * l_ref[h] + jnp.sum(p, axis=0, keepdims=True)
            pv = jnp.dot(vat_ref[0, j, rows, :], p.astype(BF16), preferred_element_type=F32)
            acc_ref[rows, :] = alpha * acc_ref[rows, :] + pv
            m_ref[h] = m_new

    def index_next(j, carry, diagonal):
        vmin8, vmax8 = carry
        s0 = block_start(j)
        kblk = ki_ref[0, pl.ds(s0, ts), :]
        score = jnp.zeros((ts, tq), F32)
        for h in range(IDX_HEADS):
            dots = jnp.dot(kblk, qin_ref[0, h], preferred_element_type=F32)
            score = score + jnp.maximum(dots, 0.0) * witn_ref[0, h:h + 1, :]
        if diagonal:
            future = s_rel > t_rel
            low = jnp.where(future, jnp.inf, score)
            score = jnp.where(future, -jnp.inf, score)
        else:
            low = score
        sc_ref[nxt, pl.ds(s0, ts), :] = score
        groups = (ts // SUBLANES, SUBLANES, tq)
        return (jnp.minimum(vmin8, low.reshape(groups).min(axis=0)),
                jnp.maximum(vmax8, score.reshape(groups).max(axis=0)))

    def write_output():
        for h in range(n_heads):
            rows = slice(h * HEAD_DIM, (h + 1) * HEAD_DIM)
            acc_ref[rows, :] = acc_ref[rows, :] * (1.0 / l_ref[h])
        o_ref[0] = (acc_ref[...].T * ga_ref[0]).astype(BF16)

    has_next = q_blk + 1 < pl.num_programs(1)

    @pl.when(has_next)
    def _():
        def pair(i, carry):
            for j in (2 * i, 2 * i + 1):
                attend(j)
                carry = index_next(j, carry, diagonal=False)
            return carry

        def single(j, carry):
            attend(j)
            return index_next(j, carry, diagonal=False)

        carry = lax.fori_loop(0, nblk // 2, pair, (jnp.full((SUBLANES, tq), jnp.inf, F32),
                                                   jnp.full((SUBLANES, tq), -jnp.inf, F32)))
        carry = lax.fori_loop(2 * (nblk // 2), nblk, single, carry)
        vmin8, vmax8 = index_next(nblk, carry, diagonal=True)
        mm_ref[nxt, 0] = vmin8
        mm_ref[nxt, 1] = vmax8
        write_output()

    @pl.when(jnp.logical_not(has_next))
    def _():
        def body(j, _):
            attend(j)
            return 0

        lax.fori_loop(0, nblk, body, 0)
        write_output()


def _dsa(ki, qim, wit, ka, vat, qam, ga, tab, topk):
    b, l, d_a = ka.shape
    n_heads = d_a // HEAD_DIM
    tq = ATT_TILE
    nq = l // tq
    assert tq <= topk, "the first query tile must not need indexer scores"
    next_tile = lambda j: jnp.minimum(j + 1, nq - 1)
    kernel = functools.partial(_dsa_kernel, topk=topk, n_heads=n_heads)
    return pl.pallas_call(
        kernel,
        grid=(b, nq),
        in_specs=[
            pl.BlockSpec((1, l, LANES), lambda i, j: (i, 0, 0)),
            pl.BlockSpec((1, IDX_HEADS, LANES, tq), lambda i, j: (i, 0, 0, next_tile(j))),
            pl.BlockSpec((1, IDX_HEADS, tq), lambda i, j: (i, 0, next_tile(j))),
            pl.BlockSpec((1, l, d_a), lambda i, j: (i, 0, 0)),
            pl.BlockSpec((1, nq, d_a, tq), lambda i, j: (i, 0, 0, 0)),
            pl.BlockSpec((1, n_heads, LANES, tq), lambda i, j: (i, 0, 0, j)),
            pl.BlockSpec((1, tq, d_a), lambda i, j: (i, j, 0)),
            pl.BlockSpec(tab.shape, lambda i, j: (0, 0, 0, 0), pipeline_mode=pl.Buffered(1)),
        ],
        out_specs=pl.BlockSpec((1, tq, d_a), lambda i, j: (i, j, 0)),
        out_shape=jax.ShapeDtypeStruct((b, l, d_a), BF16),
        scratch_shapes=[
            pltpu.VMEM((2, l, tq), F32),
            pltpu.VMEM((2, 2, SUBLANES, tq), F32),
            pltpu.VMEM((l, tq), F32),
            pltpu.VMEM((n_heads, 1, tq), F32),
            pltpu.VMEM((n_heads, 1, tq), F32),
            pltpu.VMEM((d_a, tq), F32),
            pltpu.VMEM((n_heads, tq, tq), F32),
            pltpu.VMEM((n_heads, 1, tq), F32),
        ],
        compiler_params=_params("arbitrary", "arbitrary"),
        name="dsa",
    )(ki, qim, wit, ka, vat, qam, ga, tab)


def _stick_kernel(kb_ref, vbt_ref, qbm_ref, gb_ref, u_ref, x_ref, oa_ref, wa_ref, wb_ref, y_ref,
                  below_ref, acc_ref, z_ref, lb_ref, *, n_heads):
    tq = y_ref.shape[1]
    ts = tq
    q_blk = pl.program_id(1)
    s_rel = lax.broadcasted_iota(I32, (ts, tq), 0)
    t_rel = lax.broadcasted_iota(I32, (ts, tq), 1)

    below_ref[...] = jnp.zeros(below_ref.shape, F32)
    acc_ref[...] = jnp.zeros(acc_ref.shape, F32)

    def block(j, diagonal):
        s0 = pl.multiple_of(j * ts, ts)
        strict = s_rel < t_rel

        def z_dot(h):
            return jnp.dot(kb_ref[0, pl.ds(s0, ts), _pair(h)], qbm_ref[0, h],
                           preferred_element_type=F32)

        def terms(z2):
            log_beta = jnp.minimum(z2, 0.0) - jnp.log2(1.0 + jnp.exp2(-jnp.abs(z2)))
            log_om = log_beta - z2
            if diagonal:
                log_om = jnp.where(strict, log_om, 0.0)
            suffix = jnp.dot(u_ref[...], log_om.astype(BF16), preferred_element_type=F32)
            return log_beta, log_om[0:1, :], suffix

        def finish(h, log_a, below):
            rows = slice(h * HEAD_DIM, (h + 1) * HEAD_DIM)
            a = jnp.exp2(log_a)
            if diagonal:
                a = jnp.where(strict, a, 0.0)
            pv = jnp.dot(vbt_ref[0, j, rows, :], a.astype(BF16), preferred_element_type=F32)
            acc_ref[rows, :] += pv * jnp.exp2(below)

        for h in range(n_heads):
            z_ref[h] = z_dot(h)
        below, worst = [], None
        for h in range(n_heads):
            log_beta, om_row, suffix = terms(z_ref[h])
            lb_ref[h] = log_beta + suffix
            below.append(below_ref[h])
            after = below[h] + om_row + suffix[0:1, :]
            below_ref[h] = after
            worst = after if worst is None else jnp.maximum(worst, after)
        top = jnp.max(worst)
        for h in range(n_heads):
            finish(h, lb_ref[h], below[h])
        return top

    top = block(q_blk, diagonal=True)

    def cond(carry):
        i, top = carry
        return (i <= q_blk) & (top >= EXP2_UNDERFLOW)

    def body(carry):
        i, _ = carry
        return i + 1, block(q_blk - i, diagonal=False)

    lax.while_loop(cond, body, (jnp.int32(1), top))

    ob = (acc_ref[...].T * gb_ref[0]).astype(BF16)
    y_ref[0] = (x_ref[0]
                + jnp.dot(oa_ref[0], wa_ref[...], preferred_element_type=F32)
                + jnp.dot(ob, wb_ref[...], preferred_element_type=F32))


def _stick(kb, vbt, qbm, gb, u2, x, oa, wa, wb):
    b, l, d_b = kb.shape
    d = x.shape[2]
    n_heads = d_b // HEAD_DIM
    tq = ATT_TILE
    nq = l // tq
    return pl.pallas_call(
        functools.partial(_stick_kernel, n_heads=n_heads),
        grid=(b, nq),
        in_specs=[
            pl.BlockSpec((1, l, d_b), lambda i, j: (i, 0, 0)),
            pl.BlockSpec((1, nq, d_b, tq), lambda i, j: (i, 0, 0, 0)),
            pl.BlockSpec((1, n_heads, LANES, tq), lambda i, j: (i, 0, 0, j)),
            pl.BlockSpec((1, tq, d_b), lambda i, j: (i, j, 0)),
            pl.BlockSpec(u2.shape, lambda i, j: (0, 0)),
            pl.BlockSpec((1, tq, d), lambda i, j: (i, j, 0)),
            pl.BlockSpec((1, tq, oa.shape[2]), lambda i, j: (i, j, 0)),
            pl.BlockSpec(wa.shape, lambda i, j: (0, 0)),
            pl.BlockSpec(wb.shape, lambda i, j: (0, 0)),
        ],
        out_specs=pl.BlockSpec((1, tq, d), lambda i, j: (i, j, 0)),
        out_shape=jax.ShapeDtypeStruct((b, l, d), F32),
        scratch_shapes=[
            pltpu.VMEM((n_heads, 1, tq), F32),
            pltpu.VMEM((d_b, tq), F32),
            pltpu.VMEM((n_heads, tq, tq), F32),
            pltpu.VMEM((n_heads, tq, tq), F32),
        ],
        compiler_params=_params("arbitrary", "arbitrary"),
        name="stick",
    )(kb, vbt, qbm, gb, u2, x, oa, wa, wb)


def kernel(x, norm_gain, w_in, q_norm_gain, k_norm_gain, rel_bias, w_out):
    b, l, d = x.shape
    depth = w_in.shape[0]
    d_a = d // 2
    d_b = d - d_a
    h_a = d_a // HEAD_DIM
    d_qi = IDX_HEADS * IDX_DIM
    topk = min(TOPK_MAX, l // 4)
    ts = ATT_TILE
    scale = HEAD_DIM ** -0.5 * LOG2E
    idx_scale = (IDX_HEADS * IDX_DIM) ** -0.5

    lane = jnp.arange(LANES)
    gsum = (lane[:, None] // HEAD_DIM == lane[None, :] // HEAD_DIM).astype(BF16)
    u2 = (jnp.arange(ts)[None, :] > jnp.arange(ts)[:, None]).astype(BF16)
    tab = _t5_table(rel_bias.astype(F32), ts)

    for layer in range(depth):
        w_all = _pack_w_in(jnp.swapaxes(w_in[layer], 0, 1), d_a, d_b, d_qi, scale, idx_scale)
        qg = jnp.tile(q_norm_gain[layer] * scale, h_a)[None, :]
        kg = jnp.tile(k_norm_gain[layer], h_a)[None, :]

        (qam, ka, vat, ga, qim, ki, wit, qbm, kb, vbt, gb) = _inproj(
            x, norm_gain[layer][None, :], w_all, qg, kg, gsum, d_a, d_b, d_qi)
        oa = _dsa(ki, qim, wit, ka, vat, qam, ga, tab, topk)
        w_o = w_out[layer].astype(BF16)
        x = _stick(kb, vbt, qbm, gb, u2, x, oa, w_o[:d_a], w_o[d_a:])
    return x
```

```python
import functools
import math

import jax
import jax.numpy as jnp
import numpy as np
from jax import lax
from jax.experimental import pallas as pl
from jax.experimental.pallas import tpu as pltpu

F32 = jnp.float32
BF16 = jnp.bfloat16
I32 = jnp.int32

HEAD_DIM = 64
IDX_HEADS = 16
IDX_DIM = 64
TOPK_MAX = 256
NUM_BUCKETS = 32
MAX_DISTANCE = 128
RMS_EPS = 1e-6

LANES = 128
SUBLANES = 8
NEG_BIG = -1e30
EXP2_UNDERFLOW = -150.0
LOG2E = math.log2(math.e)
BISECT_CAP = 300
BISECT_BLIND = 20
BISECT_GROUP = 2

VMEM_LIMIT_BYTES = 56 * 1024 * 1024
TOKEN_TILE = 512
ATT_TILE = 256


def _params(*sem):
    return pltpu.CompilerParams(dimension_semantics=sem, vmem_limit_bytes=VMEM_LIMIT_BYTES)


def _pair(h):
    return slice((h // 2) * LANES, (h // 2 + 1) * LANES)


def _store_head_masked(dst_ref, src):
    rows = src.shape[0]
    low = lax.broadcasted_iota(I32, (rows, LANES), 1) < HEAD_DIM
    for h in range(dst_ref.shape[1]):
        keep = low if h % 2 == 0 else jnp.logical_not(low)
        dst_ref[0, h] = jnp.where(keep, src[:, _pair(h)], 0.0).T.astype(BF16)


def _pack_w_in_kernel(wt_ref, o_ref, *, d_a, d_b, d_qi, scale, idx_scale):
    o_ki = 4 * d_a + d_qi
    o_wi = o_ki + IDX_DIM
    o_b = o_wi + IDX_HEADS
    d = wt_ref.shape[1]

    def put(dst, rows):
        o_ref[:, dst:dst + LANES] = rows.T.astype(BF16)

    for c in range(0, o_ki, LANES):
        put(c, wt_ref[c:c + LANES, :])
    k_idx = wt_ref[o_ki:o_wi, :]
    put(o_ki, jnp.concatenate([k_idx, k_idx], axis=0))
    w_idx = wt_ref[o_wi:o_b, :] * idx_scale
    put(o_ki + LANES, jnp.concatenate([w_idx, jnp.zeros((LANES - IDX_HEADS, d), F32)], axis=0))
    for c in range(0, 4 * d_b, LANES):
        rows = wt_ref[o_b + c:o_b + c + LANES, :]
        put(o_ki + 2 * LANES + c, rows * scale if c < d_b else rows)


def _pack_w_in(w_t, d_a, d_b, d_qi, scale, idx_scale):
    p, d = w_t.shape
    width = p + 2 * LANES - IDX_DIM - IDX_HEADS
    return pl.pallas_call(
        functools.partial(_pack_w_in_kernel, d_a=d_a, d_b=d_b, d_qi=d_qi, scale=scale,
                          idx_scale=idx_scale),
        out_shape=jax.ShapeDtypeStruct((d, width), BF16),
        compiler_params=pltpu.CompilerParams(vmem_limit_bytes=VMEM_LIMIT_BYTES),
        name="pack_w_in",
    )(w_t)


def _inproj_kernel(x_ref, gain_ref, w_ref, qg_ref, kg_ref, gsum_ref,
                   qa_ref, ka_ref, vat_ref, ga_ref, qi_ref, ki_ref, wit_ref,
                   qb_ref, kb_ref, vbt_ref, gb_ref, *, d_a, d_b, d_qi):
    x = x_ref[0]
    ms = jnp.mean(x * x, axis=-1, keepdims=True)
    h = (x * lax.rsqrt(ms + RMS_EPS) * gain_ref[...]).astype(BF16)
    ts = vat_ref.shape[3]

    def proj(c0, width):
        return jnp.dot(h, w_ref[:, c0:c0 + width], preferred_element_type=F32)

    def head_norm(y, g):
        sq = (y * y).astype(BF16)
        ones = gsum_ref[...]
        ssum = jnp.concatenate(
            [jnp.dot(sq[:, c:c + LANES], ones, preferred_element_type=F32)
             for c in range(0, y.shape[1], LANES)], axis=1)
        return y * lax.rsqrt(ssum * (1.0 / HEAD_DIM) + RMS_EPS) * g

    def silu(g):
        return g * (1.0 / (1.0 + jnp.exp(-g)))

    def store_key_blocks_t(dst_ref, v):
        vt = v.T.astype(BF16)
        for c in range(dst_ref.shape[1]):
            dst_ref[0, c] = vt[:, c * ts:(c + 1) * ts]

    c = 0
    _store_head_masked(qa_ref, head_norm(proj(c, d_a), qg_ref[...])); c += d_a
    ka_ref[0] = head_norm(proj(c, d_a), kg_ref[...]).astype(BF16); c += d_a
    store_key_blocks_t(vat_ref, proj(c, d_a)); c += d_a
    ga_ref[0] = silu(proj(c, d_a)); c += d_a
    _store_head_masked(qi_ref, proj(c, d_qi)); c += d_qi
    ki_ref[0] = proj(c, LANES).astype(BF16); c += LANES
    wit_ref[0] = proj(c, LANES).T[:IDX_HEADS, :]; c += LANES
    _store_head_masked(qb_ref, proj(c, d_b)); c += d_b
    kb_ref[0] = proj(c, d_b).astype(BF16); c += d_b
    store_key_blocks_t(vbt_ref, proj(c, d_b)); c += d_b
    gb_ref[0] = silu(proj(c, d_b))


def _inproj(x, gain, w_all, qg, kg, gsum, d_a, d_b, d_qi):
    b, l, d = x.shape
    tm = TOKEN_TILE
    ts = ATT_TILE
    nb = l // ts
    row = lambda width: pl.BlockSpec((1, tm, width), lambda i, j: (i, j, 0))
    const = lambda shape: pl.BlockSpec(shape, lambda i, j: (0, 0))
    vt_spec = lambda ch: pl.BlockSpec((1, tm // ts, ch, ts), lambda i, j: (i, j, 0, 0))
    tok = lambda width, dt: jax.ShapeDtypeStruct((b, l, width), dt)
    heads = lambda width: jax.ShapeDtypeStruct((b, width // HEAD_DIM, LANES, l), BF16)
    heads_spec = lambda width: pl.BlockSpec((1, width // HEAD_DIM, LANES, tm), lambda i, j: (i, 0, 0, j))
    out_shapes = [
        heads(d_a), tok(d_a, BF16), jax.ShapeDtypeStruct((b, nb, d_a, ts), BF16), tok(d_a, F32),
        heads(d_qi), tok(LANES, BF16), jax.ShapeDtypeStruct((b, IDX_HEADS, l), F32),
        heads(d_b), tok(d_b, BF16), jax.ShapeDtypeStruct((b, nb, d_b, ts), BF16), tok(d_b, F32),
    ]
    out_specs = [heads_spec(d_a), row(d_a), vt_spec(d_a), row(d_a), heads_spec(d_qi), row(LANES),
                 pl.BlockSpec((1, IDX_HEADS, tm), lambda i, j: (i, 0, j)),
                 heads_spec(d_b), row(d_b), vt_spec(d_b), row(d_b)]
    return pl.pallas_call(
        functools.partial(_inproj_kernel, d_a=d_a, d_b=d_b, d_qi=d_qi),
        grid=(b, l // tm),
        in_specs=[row(d), const(gain.shape),
                  pl.BlockSpec(w_all.shape, lambda i, j: (0, 0), pipeline_mode=pl.Buffered(1)),
                  const(qg.shape), const(kg.shape), const(gsum.shape)],
        out_specs=out_specs,
        out_shape=out_shapes,
        compiler_params=_params("arbitrary", "arbitrary"),
        name="inproj",
    )(x, gain, w_all, qg, kg, gsum)


def _t5_large_thresholds():
    max_exact = NUM_BUCKETS // 2
    d = np.arange(max_exact, 2 * MAX_DISTANCE + 1)
    large = max_exact + (np.log(d.astype(np.float32) / np.float32(max_exact))
                         / np.float32(math.log(MAX_DISTANCE / max_exact))
                         * np.float32(NUM_BUCKETS - max_exact)).astype(np.int32)
    large = np.minimum(large, NUM_BUCKETS - 1)
    return [int(d[np.argmax(large >= k)]) for k in range(max_exact + 1, NUM_BUCKETS)]


def _t5_bucket(dist):
    max_exact = NUM_BUCKETS // 2
    d = jnp.maximum(dist, 0)
    large = jnp.full(d.shape, max_exact, I32)
    for first in _t5_large_thresholds():
        large = large + (d >= first).astype(I32)
    return jnp.where(d < max_exact, d, large)


def _t5_table_kernel(rb_ref, tab_ref, *, n_heads):
    ts = tab_ref.shape[2]
    rows = 16
    far = _t5_large_thresholds()[-1]
    t_rel = lax.broadcasted_iota(I32, (rows, ts), 1)
    s_rel = lax.broadcasted_iota(I32, (rows, ts), 0)

    for i in range(3):
        n_far = max(0, min(ts, i * ts - far + 1)) // rows

        def fill(c, _, i=i):
            r0 = pl.multiple_of(c * rows, rows)
            for h in range(n_heads):
                tab_ref[i, h, pl.ds(r0, rows), :] = jnp.full((rows, ts), rb_ref[NUM_BUCKETS - 1, h] * LOG2E, F32)
            return 0

        def compute(c, _, i=i):
            r0 = pl.multiple_of(c * rows, rows)
            bucket = _t5_bucket(t_rel - (r0 + s_rel) + i * ts)
            acc = [jnp.zeros((rows, ts), F32) for _ in range(n_heads)]
            for j in range(NUM_BUCKETS):
                hit = bucket == j
                acc = [jnp.where(hit, rb_ref[j, h] * LOG2E, acc[h]) for h in range(n_heads)]
            for h in range(n_heads):
                tab_ref[i, h, pl.ds(r0, rows), :] = acc[h]
            return 0

        lax.fori_loop(0, n_far, fill, 0)
        lax.fori_loop(n_far, ts // rows, compute, 0)


def _t5_table(rel_bias, ts):
    n_heads = rel_bias.shape[1]
    return pl.pallas_call(
        functools.partial(_t5_table_kernel, n_heads=n_heads),
        in_specs=[pl.BlockSpec(memory_space=pltpu.SMEM)],
        out_specs=pl.BlockSpec(memory_space=pltpu.VMEM),
        out_shape=jax.ShapeDtypeStruct((3, n_heads, ts, ts), F32),
        compiler_params=pltpu.CompilerParams(vmem_limit_bytes=VMEM_LIMIT_BYTES),
        name="t5_table",
    )(rel_bias)


def _dsa_kernel(ki_ref, qin_ref, witn_ref, ka_ref, vat_ref, qam_ref, ga_ref, tab_ref, o_ref,
                sc_ref, mm_ref, madd_ref, m_ref, l_ref, acc_ref, lg_ref, bmax_ref, *, topk, n_heads):
    tq = o_ref.shape[1]
    ts = tq
    q_blk = pl.program_id(1)
    nblk = q_blk + 1
    cur = q_blk % 2
    nxt = 1 - cur
    s_rel = lax.broadcasted_iota(I32, (ts, tq), 0)
    t_rel = lax.broadcasted_iota(I32, (ts, tq), 1)
    t_idx = q_blk * tq + t_rel

    def block_start(j):
        return pl.multiple_of(j * ts, ts)

    all_selected = nblk * tq <= topk
    n_sel = jnp.where(all_selected, 0, nblk)

    @pl.when(all_selected)
    def _():
        mm_ref[cur] = jnp.zeros(mm_ref.shape[1:], F32)

        def body(j, _):
            s0 = block_start(j)
            madd_ref[pl.ds(s0, ts), :] = jnp.where(s0 + s_rel > t_idx, NEG_BIG, 0.0)
            return 0

        lax.fori_loop(0, nblk, body, 0)

    vmin = jnp.min(mm_ref[cur, 0], axis=0, keepdims=True)
    vmax = jnp.max(mm_ref[cur, 1], axis=0, keepdims=True)

    def count(pred_fn):
        def body(j, c):
            s0 = block_start(j)
            hit = pred_fn(sc_ref[cur, pl.ds(s0, ts), :], s0).reshape(ts // SUBLANES, SUBLANES, tq)
            c = list(c)
            for r in range(ts // SUBLANES):
                c[r % 4] = jnp.where(hit[r], c[r % 4] + 1, c[r % 4])
            return tuple(c)
        z = jnp.zeros((SUBLANES, tq), I32)
        c8 = lax.fori_loop(0, n_sel, body, (z, z, z, z))
        return jnp.sum(c8[0] + c8[1] + c8[2] + c8[3], axis=0, keepdims=True)

    def midpoint(lo, hi):
        return 0.5 * lo + 0.5 * hi

    def searching(lo, hi, cnt_lo):
        mid = midpoint(lo, hi)
        return (cnt_lo > topk) & (mid > lo) & (mid < hi)

    def status(lo, hi, cnt_lo):
        flags = 2 * searching(lo, hi, cnt_lo).astype(I32) + (cnt_lo > topk).astype(I32)
        return jnp.max(flags)

    n_causal = t_idx[0:1, :] + 1
    lo0 = jnp.where(n_causal > topk, vmin, -jnp.inf)
    hi0 = vmax + (jnp.abs(vmax) * 2.0 ** -20 + 1e-30)

    def bisect_cond(carry):
        it, _, _, _, state = carry
        return (state >= 2) & (it < BISECT_CAP)

    def bisect_step(_, carry):
        lo, hi, cnt_lo = carry
        go = searching(lo, hi, cnt_lo)
        mid = midpoint(lo, hi)
        cnt = count(lambda blk, s0: blk >= mid)
        take = go & (cnt >= topk)
        return (jnp.where(take, mid, lo), jnp.where(go & (cnt < topk), mid, hi),
                jnp.where(take, cnt, cnt_lo))

    def bisect_body(carry):
        it, lo, hi, cnt_lo, _ = carry
        lo, hi, cnt_lo = lax.fori_loop(0, BISECT_GROUP, bisect_step, (lo, hi, cnt_lo))
        return it + BISECT_GROUP, lo, hi, cnt_lo, status(lo, hi, cnt_lo)

    n_blind = jnp.where(all_selected, 0, BISECT_BLIND)
    lo, hi, cnt_lo = lax.fori_loop(0, n_blind, bisect_step, (lo0, hi0, n_causal))
    _, thr, hi, cnt_thr, state = lax.while_loop(
        bisect_cond, bisect_body,
        (jnp.int32(BISECT_BLIND), lo, hi, cnt_lo,
         jnp.where(all_selected, 0, status(lo, hi, cnt_lo))))

    tie = state >= 1

    @pl.when(jnp.logical_not(tie))
    def _():
        def body(j, _):
            s0 = block_start(j)
            sel = (sc_ref[cur, pl.ds(s0, ts), :] >= thr) & (s0 + s_rel <= t_idx)
            madd_ref[pl.ds(s0, ts), :] = jnp.where(sel, 0.0, NEG_BIG)
            return 0

        lax.fori_loop(0, n_sel, body, 0)

    @pl.when(tie)
    def _():
        need = topk - count(lambda blk, s0: blk > thr)
        n_bits = sc_ref.shape[1].bit_length()

        def jbody(i, bound):
            cand = bound + jnp.left_shift(jnp.int32(1), n_bits - 1 - i)
            cnt = count(lambda blk, s0: (blk == thr) & (s0 + s_rel < cand))
            return jnp.where(cnt <= need, cand, bound)

        bound = lax.fori_loop(0, n_bits, jbody, jnp.zeros((1, tq), I32))

        def body(j, _):
            s0 = block_start(j)
            blk = sc_ref[cur, pl.ds(s0, ts), :]
            sel = ((blk > thr) | ((blk == thr) & (s0 + s_rel < bound))) & (s0 + s_rel <= t_idx)
            madd_ref[pl.ds(s0, ts), :] = jnp.where(sel, 0.0, NEG_BIG)
            return 0

        lax.fori_loop(0, n_sel, body, 0)

    m_ref[...] = jnp.full(m_ref.shape, NEG_BIG, F32)
    l_ref[...] = jnp.zeros(l_ref.shape, F32)
    acc_ref[...] = jnp.zeros(acc_ref.shape, F32)

    def attend(j):
        s0 = block_start(j)
        madd = madd_ref[pl.ds(s0, ts), :]
        near = jnp.minimum(q_blk - j, 2)
        for h in range(n_heads):
            lg = jnp.dot(ka_ref[0, pl.ds(s0, ts), _pair(h)], qam_ref[0, h],
                         preferred_element_type=F32)
            lg = lg + tab_ref[near, h] + madd
            lg_ref[h] = lg
            bmax_ref[h] = jnp.max(lg, axis=0, keepdims=True)
        for h in range(n_heads):
            rows = slice(h * HEAD_DIM, (h + 1) * HEAD_DIM)
            m_old = m_ref[h]
            m_new = jnp.maximum(m_old, bmax_ref[h])
            alpha = jnp.exp2(m_old - m_new)
            p = jnp.exp2(lg_ref[h] - m_new)
            l_ref[h] = alpha * l_ref[h] + jnp.sum(p, axis=0, keepdims=True)
            pv = jnp.dot(vat_ref[0, j, rows, :], p.astype(BF16), preferred_element_type=F32)
            acc_ref[rows, :] = alpha * acc_ref[rows, :] + pv
            m_ref[h] = m_new

    def index_next(j, carry, diagonal):
        vmin8, vmax8 = carry
        s0 = block_start(j)
        kblk = ki_ref[0, pl.ds(s0, ts), :]
        score = jnp.zeros((ts, tq), F32)
        for h in range(IDX_HEADS):
            dots = jnp.dot(kblk, qin_ref[0, h], preferred_element_type=F32)
            score = score + jnp.maximum(dots, 0.0) * witn_ref[0, h:h + 1, :]
        if diagonal:
            future = s_rel > t_rel
            low = jnp.where(future, jnp.inf, score)
            score = jnp.where(future, -jnp.inf, score)
        else:
            low = score
        sc_ref[nxt, pl.ds(s0, ts), :] = score
        groups = (ts // SUBLANES, SUBLANES, tq)
        return (jnp.minimum(vmin8, low.reshape(groups).min(axis=0)),
                jnp.maximum(vmax8, score.reshape(groups).max(axis=0)))

    def write_output():
        for h in range(n_heads):
            rows = slice(h * HEAD_DIM, (h + 1) * HEAD_DIM)
            acc_ref[rows, :] = acc_ref[rows, :] * (1.0 / l_ref[h])
        o_ref[0] = (acc_ref[...].T * ga_ref[0]).astype(BF16)

    has_next = q_blk + 1 < pl.num_programs(1)

    @pl.when(has_next)
    def _():
        def pair(i, carry):
            for j in (2 * i, 2 * i + 1):
                attend(j)
                carry = index_next(j, carry, diagonal=False)
            return carry

        def single(j, carry):
            attend(j)
            return index_next(j, carry, diagonal=False)

        carry = lax.fori_loop(0, nblk // 2, pair, (jnp.full((SUBLANES, tq), jnp.inf, F32),
                                                   jnp.full((SUBLANES, tq), -jnp.inf, F32)))
        carry = lax.fori_loop(2 * (nblk // 2), nblk, single, carry)
        vmin8, vmax8 = index_next(nblk, carry, diagonal=True)
        mm_ref[nxt, 0] = vmin8
        mm_ref[nxt, 1] = vmax8
        write_output()

    @pl.when(jnp.logical_not(has_next))
    def _():
        def body(j, _):
            attend(j)
            return 0

        lax.fori_loop(0, nblk, body, 0)
        write_output()


def _dsa(ki, qim, wit, ka, vat, qam, ga, tab, topk):
    b, l, d_a = ka.shape
    n_heads = d_a // HEAD_DIM
    tq = ATT_TILE
    nq = l // tq
    assert tq <= topk, "the first query tile must not need indexer scores"
    next_tile = lambda j: jnp.minimum(j + 1, nq - 1)
    kernel = functools.partial(_dsa_kernel, topk=topk, n_heads=n_heads)
    return pl.pallas_call(
        kernel,
        grid=(b, nq),
        in_specs=[
            pl.BlockSpec((1, l, LANES), lambda i, j: (i, 0, 0)),
            pl.BlockSpec((1, IDX_HEADS, LANES, tq), lambda i, j: (i, 0, 0, next_tile(j))),
            pl.BlockSpec((1, IDX_HEADS, tq), lambda i, j: (i, 0, next_tile(j))),
            pl.BlockSpec((1, l, d_a), lambda i, j: (i, 0, 0)),
            pl.BlockSpec((1, nq, d_a, tq), lambda i, j: (i, 0, 0, 0)),
            pl.BlockSpec((1, n_heads, LANES, tq), lambda i, j: (i, 0, 0, j)),
            pl.BlockSpec((1, tq, d_a), lambda i, j: (i, j, 0)),
            pl.BlockSpec(tab.shape, lambda i, j: (0, 0, 0, 0), pipeline_mode=pl.Buffered(1)),
        ],
        out_specs=pl.BlockSpec((1, tq, d_a), lambda i, j: (i, j, 0)),
        out_shape=jax.ShapeDtypeStruct((b, l, d_a), BF16),
        scratch_shapes=[
            pltpu.VMEM((2, l, tq), F32),
            pltpu.VMEM((2, 2, SUBLANES, tq), F32),
            pltpu.VMEM((l, tq), F32),
            pltpu.VMEM((n_heads, 1, tq), F32),
            pltpu.VMEM((n_heads, 1, tq), F32),
            pltpu.VMEM((d_a, tq), F32),
            pltpu.VMEM((n_heads, tq, tq), F32),
            pltpu.VMEM((n_heads, 1, tq), F32),
        ],
        compiler_params=_params("arbitrary", "arbitrary"),
        name="dsa",
    )(ki, qim, wit, ka, vat, qam, ga, tab)


def _stick_kernel(kb_ref, vbt_ref, qbm_ref, gb_ref, u_ref, x_ref, oa_ref, wa_ref, wb_ref, y_ref,
                  below_ref, acc_ref, z_ref, lb_ref, top_ref, *, n_heads):
    tq = y_ref.shape[1]
    ts = tq
    q_blk = pl.program_id(1)
    s_rel = lax.broadcasted_iota(I32, (ts, tq), 0)
    t_rel = lax.broadcasted_iota(I32, (ts, tq), 1)

    below_ref[...] = jnp.zeros(below_ref.shape, F32)
    acc_ref[...] = jnp.zeros(acc_ref.shape, F32)

    def block(j, diagonal):
        s0 = pl.multiple_of(j * ts, ts)
        strict = s_rel < t_rel

        def z_dot(h):
            return jnp.dot(kb_ref[0, pl.ds(s0, ts), _pair(h)], qbm_ref[0, h],
                           preferred_element_type=F32)

        def terms(z2):
            log_beta = jnp.minimum(z2, 0.0) - jnp.log2(1.0 + jnp.exp2(-jnp.abs(z2)))
            log_om = log_beta - z2
            if diagonal:
                log_om = jnp.where(strict, log_om, 0.0)
            suffix = jnp.dot(u_ref[...], log_om.astype(BF16), preferred_element_type=F32)
            return log_beta, log_om[0:1, :], suffix

        def finish(h, log_a, below):
            rows = slice(h * HEAD_DIM, (h + 1) * HEAD_DIM)
            a = jnp.exp2(log_a)
            if diagonal:
                a = jnp.where(strict, a, 0.0)
            pv = jnp.dot(vbt_ref[0, j, rows, :], a.astype(BF16), preferred_element_type=F32)
            acc_ref[rows, :] += pv * jnp.exp2(below)

        for h in range(n_heads):
            z_ref[h] = z_dot(h)
        below, worst = [], None
        for h in range(n_heads):
            log_beta, om_row, suffix = terms(z_ref[h])
            lb_ref[h] = log_beta + suffix
            below.append(below_ref[h])
            after = below[h] + om_row + suffix[0:1, :]
            below_ref[h] = after
            worst = after if worst is None else jnp.maximum(worst, after)
        top = jnp.max(worst)
        for h in range(n_heads):
            finish(h, lb_ref[h], below[h])
        return top

    @pl.when(q_blk == 0)
    def _():
        top_ref[0] = block(q_blk, diagonal=True)

    @pl.when(q_blk > 0)
    def _():
        block(q_blk, diagonal=True)
        top_ref[0] = block(q_blk - 1, diagonal=False)

    def cond(carry):
        i, top = carry
        return (i <= q_blk) & (top >= EXP2_UNDERFLOW)

    def body(carry):
        i, _ = carry
        return i + 1, block(q_blk - i, diagonal=False)

    lax.while_loop(cond, body, (jnp.int32(2), top_ref[0]))

    ob = (acc_ref[...].T * gb_ref[0]).astype(BF16)
    y_ref[0] = (x_ref[0]
                + jnp.dot(oa_ref[0], wa_ref[...], preferred_element_type=F32)
                + jnp.dot(ob, wb_ref[...], preferred_element_type=F32))


def _stick(kb, vbt, qbm, gb, u2, x, oa, wa, wb):
    b, l, d_b = kb.shape
    d = x.shape[2]
    n_heads = d_b // HEAD_DIM
    tq = ATT_TILE
    nq = l // tq
    return pl.pallas_call(
        functools.partial(_stick_kernel, n_heads=n_heads),
        grid=(b, nq),
        in_specs=[
            pl.BlockSpec((1, l, d_b), lambda i, j: (i, 0, 0)),
            pl.BlockSpec((1, nq, d_b, tq), lambda i, j: (i, 0, 0, 0)),
            pl.BlockSpec((1, n_heads, LANES, tq), lambda i, j: (i, 0, 0, j)),
            pl.BlockSpec((1, tq, d_b), lambda i, j: (i, j, 0)),
            pl.BlockSpec(u2.shape, lambda i, j: (0, 0)),
            pl.BlockSpec((1, tq, d), lambda i, j: (i, j, 0)),
            pl.BlockSpec((1, tq, oa.shape[2]), lambda i, j: (i, j, 0)),
            pl.BlockSpec(wa.shape, lambda i, j: (0, 0)),
            pl.BlockSpec(wb.shape, lambda i, j: (0, 0)),
        ],
        out_specs=pl.BlockSpec((1, tq, d), lambda i, j: (i, j, 0)),
        out_shape=jax.ShapeDtypeStruct((b, l, d), F32),
        scratch_shapes=[
            pltpu.VMEM((n_heads, 1, tq), F32),
            pltpu.VMEM((d_b, tq), F32),
            pltpu.VMEM((n_heads, tq, tq), F32),
            pltpu.VMEM((n_heads, tq, tq), F32),
            pltpu.SMEM((1,), F32),
        ],
        compiler_params=_params("arbitrary", "arbitrary"),
        name="stick",
    )(kb, vbt, qbm, gb, u2, x, oa, wa, wb)


def kernel(x, norm_gain, w_in, q_norm_gain, k_norm_gain, rel_bias, w_out):
    b, l, d = x.shape
    depth = w_in.shape[0]
    d_a = d // 2
    d_b = d - d_a
    h_a = d_a // HEAD_DIM
    d_qi = IDX_HEADS * IDX_DIM
    topk = min(TOPK_MAX, l // 4)
    ts = ATT_TILE
    scale = HEAD_DIM ** -0.5 * LOG2E
    idx_scale = (IDX_HEADS * IDX_DIM) ** -0.5

    lane = jnp.arange(LANES)
    gsum = (lane[:, None] // HEAD_DIM == lane[None, :] // HEAD_DIM).astype(BF16)
    u2 = (jnp.arange(ts)[None, :] > jnp.arange(ts)[:, None]).astype(BF16)
    tab = _t5_table(rel_bias.astype(F32), ts)

    for layer in range(depth):
        w_all = _pack_w_in(jnp.swapaxes(w_in[layer], 0, 1), d_a, d_b, d_qi, scale, idx_scale)
        qg = jnp.tile(q_norm_gain[layer] * scale, h_a)[None, :]
        kg = jnp.tile(k_norm_gain[layer], h_a)[None, :]

        (qam, ka, vat, ga, qim, ki, wit, qbm, kb, vbt, gb) = _inproj(
            x, norm_gain[layer][None, :], w_all, qg, kg, gsum, d_a, d_b, d_qi)
        oa = _dsa(ki, qim, wit, ka, vat, qam, ga, tab, topk)
        w_o = w_out[layer].astype(BF16)
        x = _stick(kb, vbt, qbm, gb, u2, x, oa, w_o[:d_a], w_o[d_a:])
    return x
```

```python
import functools
import math

import jax
import jax.numpy as jnp
import numpy as np
from jax import lax
from jax.experimental import pallas as pl
from jax.experimental.pallas import tpu as pltpu

F32 = jnp.float32
BF16 = jnp.bfloat16
I32 = jnp.int32

HEAD_DIM = 64
IDX_HEADS = 16
IDX_DIM = 64
TOPK_MAX = 256
NUM_BUCKETS = 32
MAX_DISTANCE = 128
RMS_EPS = 1e-6

LANES = 128
SUBLANES = 8
NEG_BIG = -1e30
EXP2_UNDERFLOW = -150.0
LOG2E = math.log2(math.e)
BISECT_CAP = 300
BISECT_BLIND = 20
BISECT_GROUP = 2

VMEM_LIMIT_BYTES = 56 * 1024 * 1024
TOKEN_TILE = 512
ATT_TILE = 256


def _params(*sem):
    return pltpu.CompilerParams(dimension_semantics=sem, vmem_limit_bytes=VMEM_LIMIT_BYTES)


def _pair(h):
    return slice((h // 2) * LANES, (h // 2 + 1) * LANES)


def _store_head_masked(dst_ref, src):
    rows = src.shape[0]
    low = lax.broadcasted_iota(I32, (rows, LANES), 1) < HEAD_DIM
    for h in range(dst_ref.shape[1]):
        keep = low if h % 2 == 0 else jnp.logical_not(low)
        dst_ref[0, h] = jnp.where(keep, src[:, _pair(h)], 0.0).T.astype(BF16)


def _pack_w_in_kernel(wt_ref, o_ref, *, d_a, d_b, d_qi, scale, idx_scale):
    o_ki = 4 * d_a + d_qi
    o_wi = o_ki + IDX_DIM
    o_b = o_wi + IDX_HEADS
    d = wt_ref.shape[1]

    def put(dst, rows):
        o_ref[:, dst:dst + LANES] = rows.T.astype(BF16)

    for c in range(0, o_ki, LANES):
        put(c, wt_ref[c:c + LANES, :])
    k_idx = wt_ref[o_ki:o_wi, :]
    put(o_ki, jnp.concatenate([k_idx, k_idx], axis=0))
    w_idx = wt_ref[o_wi:o_b, :] * idx_scale
    put(o_ki + LANES, jnp.concatenate([w_idx, jnp.zeros((LANES - IDX_HEADS, d), F32)], axis=0))
    for c in range(0, 4 * d_b, LANES):
        rows = wt_ref[o_b + c:o_b + c + LANES, :]
        put(o_ki + 2 * LANES + c, rows * scale if c < d_b else rows)


def _pack_w_in(w_t, d_a, d_b, d_qi, scale, idx_scale):
    p, d = w_t.shape
    width = p + 2 * LANES - IDX_DIM - IDX_HEADS
    return pl.pallas_call(
        functools.partial(_pack_w_in_kernel, d_a=d_a, d_b=d_b, d_qi=d_qi, scale=scale,
                          idx_scale=idx_scale),
        out_shape=jax.ShapeDtypeStruct((d, width), BF16),
        compiler_params=pltpu.CompilerParams(vmem_limit_bytes=VMEM_LIMIT_BYTES),
        name="pack_w_in",
    )(w_t)


def _inproj_kernel(x_ref, gain_ref, w_ref, qg_ref, kg_ref, gsum_ref,
                   qa_ref, ka_ref, vat_ref, ga_ref, qi_ref, ki_ref, wit_ref,
                   qb_ref, kb_ref, vbt_ref, gb_ref, *, d_a, d_b, d_qi):
    x = x_ref[0]
    ms = jnp.mean(x * x, axis=-1, keepdims=True)
    h = (x * lax.rsqrt(ms + RMS_EPS) * gain_ref[...]).astype(BF16)
    ts = vat_ref.shape[3]

    def proj(c0, width):
        return jnp.dot(h, w_ref[:, c0:c0 + width], preferred_element_type=F32)

    def head_norm(y, g):
        sq = (y * y).astype(BF16)
        ones = gsum_ref[...]
        ssum = jnp.concatenate(
            [jnp.dot(sq[:, c:c + LANES], ones, preferred_element_type=F32)
             for c in range(0, y.shape[1], LANES)], axis=1)
        return y * lax.rsqrt(ssum * (1.0 / HEAD_DIM) + RMS_EPS) * g

    def silu(g):
        return g * (1.0 / (1.0 + jnp.exp(-g)))

    def store_key_blocks_t(dst_ref, v):
        vt = v.T.astype(BF16)
        for c in range(dst_ref.shape[1]):
            dst_ref[0, c] = vt[:, c * ts:(c + 1) * ts]

    c = 0
    _store_head_masked(qa_ref, head_norm(proj(c, d_a), qg_ref[...])); c += d_a
    ka_ref[0] = head_norm(proj(c, d_a), kg_ref[...]).astype(BF16); c += d_a
    store_key_blocks_t(vat_ref, proj(c, d_a)); c += d_a
    ga_ref[0] = silu(proj(c, d_a)); c += d_a
    _store_head_masked(qi_ref, proj(c, d_qi)); c += d_qi
    ki_ref[0] = proj(c, LANES).astype(BF16); c += LANES
    wit_ref[0] = proj(c, LANES).T[:IDX_HEADS, :]; c += LANES
    _store_head_masked(qb_ref, proj(c, d_b)); c += d_b
    kb_ref[0] = proj(c, d_b).astype(BF16); c += d_b
    store_key_blocks_t(vbt_ref, proj(c, d_b)); c += d_b
    gb_ref[0] = silu(proj(c, d_b))


def _inproj(x, gain, w_all, qg, kg, gsum, d_a, d_b, d_qi):
    b, l, d = x.shape
    tm = TOKEN_TILE
    ts = ATT_TILE
    nb = l // ts
    row = lambda width: pl.BlockSpec((1, tm, width), lambda i, j: (i, j, 0))
    const = lambda shape: pl.BlockSpec(shape, lambda i, j: (0, 0))
    vt_spec = lambda ch: pl.BlockSpec((1, tm // ts, ch, ts), lambda i, j: (i, j, 0, 0))
    tok = lambda width, dt: jax.ShapeDtypeStruct((b, l, width), dt)
    heads = lambda width: jax.ShapeDtypeStruct((b, width // HEAD_DIM, LANES, l), BF16)
    heads_spec = lambda width: pl.BlockSpec((1, width // HEAD_DIM, LANES, tm), lambda i, j: (i, 0, 0, j))
    out_shapes = [
        heads(d_a), tok(d_a, BF16), jax.ShapeDtypeStruct((b, nb, d_a, ts), BF16), tok(d_a, F32),
        heads(d_qi), tok(LANES, BF16), jax.ShapeDtypeStruct((b, IDX_HEADS, l), F32),
        heads(d_b), tok(d_b, BF16), jax.ShapeDtypeStruct((b, nb, d_b, ts), BF16), tok(d_b, F32),
    ]
    out_specs = [heads_spec(d_a), row(d_a), vt_spec(d_a), row(d_a), heads_spec(d_qi), row(LANES),
                 pl.BlockSpec((1, IDX_HEADS, tm), lambda i, j: (i, 0, j)),
                 heads_spec(d_b), row(d_b), vt_spec(d_b), row(d_b)]
    return pl.pallas_call(
        functools.partial(_inproj_kernel, d_a=d_a, d_b=d_b, d_qi=d_qi),
        grid=(b, l // tm),
        in_specs=[row(d), const(gain.shape),
                  pl.BlockSpec(w_all.shape, lambda i, j: (0, 0), pipeline_mode=pl.Buffered(1)),
                  const(qg.shape), const(kg.shape), const(gsum.shape)],
        out_specs=out_specs,
        out_shape=out_shapes,
        compiler_params=_params("arbitrary", "arbitrary"),
        name="inproj",
    )(x, gain, w_all, qg, kg, gsum)


def _t5_large_thresholds():
    max_exact = NUM_BUCKETS // 2
    d = np.arange(max_exact, 2 * MAX_DISTANCE + 1)
    large = max_exact + (np.log(d.astype(np.float32) / np.float32(max_exact))
                         / np.float32(math.log(MAX_DISTANCE / max_exact))
                         * np.float32(NUM_BUCKETS - max_exact)).astype(np.int32)
    large = np.minimum(large, NUM_BUCKETS - 1)
    return [int(d[np.argmax(large >= k)]) for k in range(max_exact + 1, NUM_BUCKETS)]


def _t5_bucket(dist):
    max_exact = NUM_BUCKETS // 2
    d = jnp.maximum(dist, 0)
    large = jnp.full(d.shape, max_exact, I32)
    for first in _t5_large_thresholds():
        large = large + (d >= first).astype(I32)
    return jnp.where(d < max_exact, d, large)


def _t5_table_kernel(rb_ref, tab_ref, *, n_heads):
    ts = tab_ref.shape[2]
    rows = 16
    far = _t5_large_thresholds()[-1]
    t_rel = lax.broadcasted_iota(I32, (rows, ts), 1)
    s_rel = lax.broadcasted_iota(I32, (rows, ts), 0)

    for i in range(3):
        n_far = max(0, min(ts, i * ts - far + 1)) // rows

        def fill(c, _, i=i):
            r0 = pl.multiple_of(c * rows, rows)
            for h in range(n_heads):
                tab_ref[i, h, pl.ds(r0, rows), :] = jnp.full((rows, ts), rb_ref[NUM_BUCKETS - 1, h] * LOG2E, F32)
            return 0

        def compute(c, _, i=i):
            r0 = pl.multiple_of(c * rows, rows)
            bucket = _t5_bucket(t_rel - (r0 + s_rel) + i * ts)
            acc = [jnp.zeros((rows, ts), F32) for _ in range(n_heads)]
            for j in range(NUM_BUCKETS):
                hit = bucket == j
                acc = [jnp.where(hit, rb_ref[j, h] * LOG2E, acc[h]) for h in range(n_heads)]
            for h in range(n_heads):
                tab_ref[i, h, pl.ds(r0, rows), :] = acc[h]
            return 0

        lax.fori_loop(0, n_far, fill, 0)
        lax.fori_loop(n_far, ts // rows, compute, 0)


def _t5_table(rel_bias, ts):
    n_heads = rel_bias.shape[1]
    return pl.pallas_call(
        functools.partial(_t5_table_kernel, n_heads=n_heads),
        in_specs=[pl.BlockSpec(memory_space=pltpu.SMEM)],
        out_specs=pl.BlockSpec(memory_space=pltpu.VMEM),
        out_shape=jax.ShapeDtypeStruct((3, n_heads, ts, ts), F32),
        compiler_params=pltpu.CompilerParams(vmem_limit_bytes=VMEM_LIMIT_BYTES),
        name="t5_table",
    )(rel_bias)


def _dsa_kernel(ki_ref, qin_ref, witn_ref, ka_ref, vat_ref, qam_ref, ga_ref, tab_ref, o_ref,
                sc_ref, mm_ref, madd_ref, m_ref, l_ref, acc_ref, lg_ref, bmax_ref, *, topk, n_heads):
    tq = o_ref.shape[1]
    ts = tq
    q_blk = pl.program_id(1)
    nblk = q_blk + 1
    cur = q_blk % 2
    nxt = 1 - cur
    s_rel = lax.broadcasted_iota(I32, (ts, tq), 0)
    t_rel = lax.broadcasted_iota(I32, (ts, tq), 1)
    t_idx = q_blk * tq + t_rel

    def block_start(j):
        return pl.multiple_of(j * ts, ts)

    all_selected = nblk * tq <= topk
    n_sel = jnp.where(all_selected, 0, nblk)

    @pl.when(all_selected)
    def _():
        mm_ref[cur] = jnp.zeros(mm_ref.shape[1:], F32)

        def body(j, _):
            s0 = block_start(j)
            madd_ref[pl.ds(s0, ts), :] = jnp.where(s0 + s_rel > t_idx, NEG_BIG, 0.0)
            return 0

        lax.fori_loop(0, nblk, body, 0)

    vmin = jnp.min(mm_ref[cur, 0], axis=0, keepdims=True)
    vmax = jnp.max(mm_ref[cur, 1], axis=0, keepdims=True)

    def count(pred_fn):
        def body(j, c):
            s0 = block_start(j)
            hit = pred_fn(sc_ref[cur, pl.ds(s0, ts), :], s0).reshape(ts // SUBLANES, SUBLANES, tq)
            c = list(c)
            for r in range(ts // SUBLANES):
                c[r % 4] = jnp.where(hit[r], c[r % 4] + 1, c[r % 4])
            return tuple(c)
        z = jnp.zeros((SUBLANES, tq), I32)
        c8 = lax.fori_loop(0, n_sel, body, (z, z, z, z))
        return jnp.sum(c8[0] + c8[1] + c8[2] + c8[3], axis=0, keepdims=True)

    def midpoint(lo, hi):
        return 0.5 * lo + 0.5 * hi

    def searching(lo, hi, cnt_lo):
        mid = midpoint(lo, hi)
        return (cnt_lo > topk) & (mid > lo) & (mid < hi)

    def status(lo, hi, cnt_lo):
        flags = 2.0 * searching(lo, hi, cnt_lo).astype(F32) + (cnt_lo > topk).astype(F32)
        return jnp.max(flags)

    n_causal = t_idx[0:1, :] + 1
    lo0 = jnp.where(n_causal > topk, vmin, -jnp.inf)
    hi0 = vmax + (jnp.abs(vmax) * 2.0 ** -20 + 1e-30)

    def bisect_cond(carry):
        it, _, _, _, state = carry
        return (state >= 2) & (it < BISECT_CAP)

    def bisect_step(_, carry):
        lo, hi, cnt_lo = carry
        go = searching(lo, hi, cnt_lo)
        mid = midpoint(lo, hi)
        cnt = count(lambda blk, s0: blk >= mid)
        take = go & (cnt >= topk)
        return (jnp.where(take, mid, lo), jnp.where(go & (cnt < topk), mid, hi),
                jnp.where(take, cnt, cnt_lo))

    def bisect_body(carry):
        it, lo, hi, cnt_lo, _ = carry
        lo, hi, cnt_lo = lax.fori_loop(0, BISECT_GROUP, bisect_step, (lo, hi, cnt_lo))
        return it + BISECT_GROUP, lo, hi, cnt_lo, status(lo, hi, cnt_lo)

    n_blind = jnp.where(all_selected, 0, BISECT_BLIND)
    lo, hi, cnt_lo = lax.fori_loop(0, n_blind, bisect_step, (lo0, hi0, n_causal))
    _, thr, hi, cnt_thr, state = lax.while_loop(
        bisect_cond, bisect_body,
        (jnp.int32(BISECT_BLIND), lo, hi, cnt_lo,
         jnp.where(all_selected, 0, status(lo, hi, cnt_lo))))

    tie = state >= 1

    @pl.when(jnp.logical_not(tie))
    def _():
        def body(j, _):
            s0 = block_start(j)
            sel = (sc_ref[cur, pl.ds(s0, ts), :] >= thr) & (s0 + s_rel <= t_idx)
            madd_ref[pl.ds(s0, ts), :] = jnp.where(sel, 0.0, NEG_BIG)
            return 0

        lax.fori_loop(0, n_sel, body, 0)

    @pl.when(tie)
    def _():
        need = topk - count(lambda blk, s0: blk > thr)
        n_bits = sc_ref.shape[1].bit_length()

        def jbody(i, bound):
            cand = bound + jnp.left_shift(jnp.int32(1), n_bits - 1 - i)
            cnt = count(lambda blk, s0: (blk == thr) & (s0 + s_rel < cand))
            return jnp.where(cnt <= need, cand, bound)

        bound = lax.fori_loop(0, n_bits, jbody, jnp.zeros((1, tq), I32))

        def body(j, _):
            s0 = block_start(j)
            blk = sc_ref[cur, pl.ds(s0, ts), :]
            sel = ((blk > thr) | ((blk == thr) & (s0 + s_rel < bound))) & (s0 + s_rel <= t_idx)
            madd_ref[pl.ds(s0, ts), :] = jnp.where(sel, 0.0, NEG_BIG)
            return 0

        lax.fori_loop(0, n_sel, body, 0)

    m_ref[...] = jnp.full(m_ref.shape, NEG_BIG, F32)
    l_ref[...] = jnp.zeros(l_ref.shape, F32)
    acc_ref[...] = jnp.zeros(acc_ref.shape, F32)

    def attend(j):
        s0 = block_start(j)
        madd = madd_ref[pl.ds(s0, ts), :]
        near = jnp.minimum(q_blk - j, 2)
        for h in range(n_heads):
            lg = jnp.dot(ka_ref[0, pl.ds(s0, ts), _pair(h)], qam_ref[0, h],
                         preferred_element_type=F32)
            lg = lg + tab_ref[near, h] + madd
            lg_ref[h] = lg
            bmax_ref[h] = jnp.max(lg, axis=0, keepdims=True)
        for h in range(n_heads):
            rows = slice(h * HEAD_DIM, (h + 1) * HEAD_DIM)
            m_old = m_ref[h]
            m_new = jnp.maximum(m_old, bmax_ref[h])
            alpha = jnp.exp2(m_old - m_new)
            p = jnp.exp2(lg_ref[h] - m_new)
            l_ref[h] = alpha * l_ref[h] + jnp.sum(p, axis=0, keepdims=True)
            pv = jnp.dot(vat_ref[0, j, rows, :], p.astype(BF16), preferred_element_type=F32)
            acc_ref[rows, :] = alpha * acc_ref[rows, :] + pv
            m_ref[h] = m_new

    def index_next(j, carry, diagonal):
        vmin8, vmax8 = carry
        s0 = block_start(j)
        kblk = ki_ref[0, pl.ds(s0, ts), :]
        score = jnp.zeros((ts, tq), F32)
        for h in range(IDX_HEADS):
            dots = jnp.dot(kblk, qin_ref[0, h], preferred_element_type=F32)
            score = score + jnp.maximum(dots, 0.0) * witn_ref[0, h:h + 1, :]
        if diagonal:
            future = s_rel > t_rel
            low = jnp.where(future, jnp.inf, score)
            score = jnp.where(future, -jnp.inf, score)
        else:
            low = score
        sc_ref[nxt, pl.ds(s0, ts), :] = score
        groups = (ts // SUBLANES, SUBLANES, tq)
        return (jnp.minimum(vmin8, low.reshape(groups).min(axis=0)),
                jnp.maximum(vmax8, score.reshape(groups).max(axis=0)))

    def write_output():
        for h in range(n_heads):
            rows = slice(h * HEAD_DIM, (h + 1) * HEAD_DIM)
            acc_ref[rows, :] = acc_ref[rows, :] * (1.0 / l_ref[h])
        o_ref[0] = (acc_ref[...].T * ga_ref[0]).astype(BF16)

    has_next = q_blk + 1 < pl.num_programs(1)

    @pl.when(has_next)
    def _():
        def pair(i, carry):
            for j in (2 * i, 2 * i + 1):
                attend(j)
                carry = index_next(j, carry, diagonal=False)
            return carry

        def single(j, carry):
            attend(j)
            return index_next(j, carry, diagonal=False)

        carry = lax.fori_loop(0, nblk // 2, pair, (jnp.full((SUBLANES, tq), jnp.inf, F32),
                                                   jnp.full((SUBLANES, tq), -jnp.inf, F32)))
        carry = lax.fori_loop(2 * (nblk // 2), nblk, single, carry)
        vmin8, vmax8 = index_next(nblk, carry, diagonal=True)
        mm_ref[nxt, 0] = vmin8
        mm_ref[nxt, 1] = vmax8
        write_output()

    @pl.when(jnp.logical_not(has_next))
    def _():
        def body(j, _):
            attend(j)
            return 0

        lax.fori_loop(0, nblk, body, 0)
        write_output()


def _dsa(ki, qim, wit, ka, vat, qam, ga, tab, topk):
    b, l, d_a = ka.shape
    n_heads = d_a // HEAD_DIM
    tq = ATT_TILE
    nq = l // tq
    assert tq <= topk, "the first query tile must not need indexer scores"
    next_tile = lambda j: jnp.minimum(j + 1, nq - 1)
    kernel = functools.partial(_dsa_kernel, topk=topk, n_heads=n_heads)
    return pl.pallas_call(
        kernel,
        grid=(b, nq),
        in_specs=[
            pl.BlockSpec((1, l, LANES), lambda i, j: (i, 0, 0)),
            pl.BlockSpec((1, IDX_HEADS, LANES, tq), lambda i, j: (i, 0, 0, next_tile(j))),
            pl.BlockSpec((1, IDX_HEADS, tq), lambda i, j: (i, 0, next_tile(j))),
            pl.BlockSpec((1, l, d_a), lambda i, j: (i, 0, 0)),
            pl.BlockSpec((1, nq, d_a, tq), lambda i, j: (i, 0, 0, 0)),
            pl.BlockSpec((1, n_heads, LANES, tq), lambda i, j: (i, 0, 0, j)),
            pl.BlockSpec((1, tq, d_a), lambda i, j: (i, j, 0)),
            pl.BlockSpec(tab.shape, lambda i, j: (0, 0, 0, 0), pipeline_mode=pl.Buffered(1)),
        ],
        out_specs=pl.BlockSpec((1, tq, d_a), lambda i, j: (i, j, 0)),
        out_shape=jax.ShapeDtypeStruct((b, l, d_a), BF16),
        scratch_shapes=[
            pltpu.VMEM((2, l, tq), F32),
            pltpu.VMEM((2, 2, SUBLANES, tq), F32),
            pltpu.VMEM((l, tq), F32),
            pltpu.VMEM((n_heads, 1, tq), F32),
            pltpu.VMEM((n_heads, 1, tq), F32),
            pltpu.VMEM((d_a, tq), F32),
            pltpu.VMEM((n_heads, tq, tq), F32),
            pltpu.VMEM((n_heads, 1, tq), F32),
        ],
        compiler_params=_params("arbitrary", "arbitrary"),
        name="dsa",
    )(ki, qim, wit, ka, vat, qam, ga, tab)


def _stick_kernel(kb_ref, vbt_ref, qbm_ref, gb_ref, u_ref, x_ref, oa_ref, wa_ref, wb_ref, y_ref,
                  below_ref, acc_ref, z_ref, lb_ref, top_ref, *, n_heads):
    tq = y_ref.shape[1]
    ts = tq
    q_blk = pl.program_id(1)
    s_rel = lax.broadcasted_iota(I32, (ts, tq), 0)
    t_rel = lax.broadcasted_iota(I32, (ts, tq), 1)

    below_ref[...] = jnp.zeros(below_ref.shape, F32)
    acc_ref[...] = jnp.zeros(acc_ref.shape, F32)

    def block(j, diagonal):
        s0 = pl.multiple_of(j * ts, ts)
        strict = s_rel < t_rel

        def z_dot(h):
            return jnp.dot(kb_ref[0, pl.ds(s0, ts), _pair(h)], qbm_ref[0, h],
                           preferred_element_type=F32)

        def terms(z2):
            log_beta = jnp.minimum(z2, 0.0) - jnp.log2(1.0 + jnp.exp2(-jnp.abs(z2)))
            log_om = log_beta - z2
            if diagonal:
                log_om = jnp.where(strict, log_om, 0.0)
            suffix = jnp.dot(u_ref[...], log_om.astype(BF16), preferred_element_type=F32)
            return log_beta, log_om[0:1, :], suffix

        def finish(h, log_a, below):
            rows = slice(h * HEAD_DIM, (h + 1) * HEAD_DIM)
            a = jnp.exp2(log_a)
            if diagonal:
                a = jnp.where(strict, a, 0.0)
            pv = jnp.dot(vbt_ref[0, j, rows, :], a.astype(BF16), preferred_element_type=F32)
            acc_ref[rows, :] += pv * jnp.exp2(below)

        for h in range(n_heads):
            z_ref[h] = z_dot(h)
        below, worst = [], None
        for h in range(n_heads):
            log_beta, om_row, suffix = terms(z_ref[h])
            lb_ref[h] = log_beta + suffix
            below.append(below_ref[h])
            after = below[h] + om_row + suffix[0:1, :]
            below_ref[h] = after
            worst = after if worst is None else jnp.maximum(worst, after)
        top = jnp.max(worst)
        for h in range(n_heads):
            finish(h, lb_ref[h], below[h])
        return top

    @pl.when(q_blk == 0)
    def _():
        top_ref[0] = block(q_blk, diagonal=True)

    @pl.when(q_blk > 0)
    def _():
        block(q_blk, diagonal=True)
        top_ref[0] = block(q_blk - 1, diagonal=False)

    def cond(carry):
        i, top = carry
        return (i <= q_blk) & (top >= EXP2_UNDERFLOW)

    def body(carry):
        i, _ = carry
        return i + 1, block(q_blk - i, diagonal=False)

    lax.while_loop(cond, body, (jnp.int32(2), top_ref[0]))

    ob = (acc_ref[...].T * gb_ref[0]).astype(BF16)
    y_ref[0] = (x_ref[0]
                + jnp.dot(oa_ref[0], wa_ref[...], preferred_element_type=F32)
                + jnp.dot(ob, wb_ref[...], preferred_element_type=F32))


def _stick(kb, vbt, qbm, gb, u2, x, oa, wa, wb):
    b, l, d_b = kb.shape
    d = x.shape[2]
    n_heads = d_b // HEAD_DIM
    tq = ATT_TILE
    nq = l // tq
    return pl.pallas_call(
        functools.partial(_stick_kernel, n_heads=n_heads),
        grid=(b, nq),
        in_specs=[
            pl.BlockSpec((1, l, d_b), lambda i, j: (i, 0, 0)),
            pl.BlockSpec((1, nq, d_b, tq), lambda i, j: (i, 0, 0, 0)),
            pl.BlockSpec((1, n_heads, LANES, tq), lambda i, j: (i, 0, 0, j)),
            pl.BlockSpec((1, tq, d_b), lambda i, j: (i, j, 0)),
            pl.BlockSpec(u2.shape, lambda i, j: (0, 0)),
            pl.BlockSpec((1, tq, d), lambda i, j: (i, j, 0)),
            pl.BlockSpec((1, tq, oa.shape[2]), lambda i, j: (i, j, 0)),
            pl.BlockSpec(wa.shape, lambda i, j: (0, 0)),
            pl.BlockSpec(wb.shape, lambda i, j: (0, 0)),
        ],
        out_specs=pl.BlockSpec((1, tq, d), lambda i, j: (i, j, 0)),
        out_shape=jax.ShapeDtypeStruct((b, l, d), F32),
        scratch_shapes=[
            pltpu.VMEM((n_heads, 1, tq), F32),
            pltpu.VMEM((d_b, tq), F32),
            pltpu.VMEM((n_heads, tq, tq), F32),
            pltpu.VMEM((n_heads, tq, tq), F32),
            pltpu.SMEM((1,), F32),
        ],
        compiler_params=_params("arbitrary", "arbitrary"),
        name="stick",
    )(kb, vbt, qbm, gb, u2, x, oa, wa, wb)


def kernel(x, norm_gain, w_in, q_norm_gain, k_norm_gain, rel_bias, w_out):
    b, l, d = x.shape
    depth = w_in.shape[0]
    d_a = d // 2
    d_b = d - d_a
    h_a = d_a // HEAD_DIM
    d_qi = IDX_HEADS * IDX_DIM
    topk = min(TOPK_MAX, l // 4)
    ts = ATT_TILE
    scale = HEAD_DIM ** -0.5 * LOG2E
    idx_scale = (IDX_HEADS * IDX_DIM) ** -0.5

    lane = jnp.arange(LANES)
    gsum = (lane[:, None] // HEAD_DIM == lane[None, :] // HEAD_DIM).astype(BF16)
    u2 = (jnp.arange(ts)[None, :] > jnp.arange(ts)[:, None]).astype(BF16)
    tab = _t5_table(rel_bias.astype(F32), ts)

    for layer in range(depth):
        w_all = _pack_w_in(jnp.swapaxes(w_in[layer], 0, 1), d_a, d_b, d_qi, scale, idx_scale)
        qg = jnp.tile(q_norm_gain[layer] * scale, h_a)[None, :]
        kg = jnp.tile(k_norm_gain[layer], h_a)[None, :]

        (qam, ka, vat, ga, qim, ki, wit, qbm, kb, vbt, gb) = _inproj(
            x, norm_gain[layer][None, :], w_all, qg, kg, gsum, d_a, d_b, d_qi)
        oa = _dsa(ki, qim, wit, ka, vat, qam, ga, tab, topk)
        w_o = w_out[layer].astype(BF16)
        x = _stick(kb, vbt, qbm, gb, u2, x, oa, w_o[:d_a], w_o[d_a:])
    return x
```

```python
import functools
import math

import jax
import jax.numpy as jnp
import numpy as np
from jax import lax
from jax.experimental import pallas as pl
from jax.experimental.pallas import tpu as pltpu

F32 = jnp.float32
BF16 = jnp.bfloat16
I32 = jnp.int32

HEAD_DIM = 64
IDX_HEADS = 16
IDX_DIM = 64
TOPK_MAX = 256
NUM_BUCKETS = 32
MAX_DISTANCE = 128
RMS_EPS = 1e-6

LANES = 128
SUBLANES = 8
NEG_BIG = -1e30
EXP2_UNDERFLOW = -150.0
LOG2E = math.log2(math.e)
BISECT_CAP = 300
BISECT_BLIND = 20
BISECT_GROUP = 2

VMEM_LIMIT_BYTES = 56 * 1024 * 1024
TOKEN_TILE = 512
ATT_TILE = 256


def _params(*sem):
    return pltpu.CompilerParams(dimension_semantics=sem, vmem_limit_bytes=VMEM_LIMIT_BYTES)


def _pair(h):
    return slice((h // 2) * LANES, (h // 2 + 1) * LANES)


def _store_head_masked(dst_ref, src):
    rows = src.shape[0]
    low = lax.broadcasted_iota(I32, (rows, LANES), 1) < HEAD_DIM
    for h in range(dst_ref.shape[1]):
        keep = low if h % 2 == 0 else jnp.logical_not(low)
        dst_ref[0, h] = jnp.where(keep, src[:, _pair(h)], 0.0).T.astype(BF16)


def _pack_w_in_kernel(wt_ref, o_ref, *, d_a, d_b, d_qi, scale, idx_scale):
    o_ki = 4 * d_a + d_qi
    o_wi = o_ki + IDX_DIM
    o_b = o_wi + IDX_HEADS
    d = wt_ref.shape[1]

    def put(dst, rows):
        o_ref[:, dst:dst + LANES] = rows.T.astype(BF16)

    for c in range(0, o_ki, LANES):
        put(c, wt_ref[c:c + LANES, :])
    k_idx = wt_ref[o_ki:o_wi, :]
    put(o_ki, jnp.concatenate([k_idx, k_idx], axis=0))
    w_idx = wt_ref[o_wi:o_b, :] * idx_scale
    put(o_ki + LANES, jnp.concatenate([w_idx, jnp.zeros((LANES - IDX_HEADS, d), F32)], axis=0))
    for c in range(0, 4 * d_b, LANES):
        rows = wt_ref[o_b + c:o_b + c + LANES, :]
        put(o_ki + 2 * LANES + c, rows * scale if c < d_b else rows)


def _pack_w_in(w_t, d_a, d_b, d_qi, scale, idx_scale):
    p, d = w_t.shape
    width = p + 2 * LANES - IDX_DIM - IDX_HEADS
    return pl.pallas_call(
        functools.partial(_pack_w_in_kernel, d_a=d_a, d_b=d_b, d_qi=d_qi, scale=scale,
                          idx_scale=idx_scale),
        out_shape=jax.ShapeDtypeStruct((d, width), BF16),
        compiler_params=pltpu.CompilerParams(vmem_limit_bytes=VMEM_LIMIT_BYTES),
        name="pack_w_in",
    )(w_t)


def _inproj_kernel(x_ref, gain_ref, w_ref, qg_ref, kg_ref, gsum_ref,
                   qa_ref, ka_ref, vat_ref, ga_ref, qi_ref, ki_ref, wit_ref,
                   qb_ref, kb_ref, vbt_ref, gb_ref, *, d_a, d_b, d_qi):
    x = x_ref[0]
    ms = jnp.mean(x * x, axis=-1, keepdims=True)
    h = (x * lax.rsqrt(ms + RMS_EPS) * gain_ref[...]).astype(BF16)
    ts = vat_ref.shape[3]

    def proj(c0, width):
        return jnp.dot(h, w_ref[:, c0:c0 + width], preferred_element_type=F32)

    def head_norm(y, g):
        sq = (y * y).astype(BF16)
        ones = gsum_ref[...]
        ssum = jnp.concatenate(
            [jnp.dot(sq[:, c:c + LANES], ones, preferred_element_type=F32)
             for c in range(0, y.shape[1], LANES)], axis=1)
        return y * lax.rsqrt(ssum * (1.0 / HEAD_DIM) + RMS_EPS) * g

    def silu(g):
        return g * (1.0 / (1.0 + jnp.exp(-g)))

    def store_key_blocks_t(dst_ref, v):
        vt = v.T.astype(BF16)
        for c in range(dst_ref.shape[1]):
            dst_ref[0, c] = vt[:, c * ts:(c + 1) * ts]

    c = 0
    _store_head_masked(qa_ref, head_norm(proj(c, d_a), qg_ref[...])); c += d_a
    ka_ref[0] = head_norm(proj(c, d_a), kg_ref[...]).astype(BF16); c += d_a
    store_key_blocks_t(vat_ref, proj(c, d_a)); c += d_a
    ga_ref[0] = silu(proj(c, d_a)); c += d_a
    _store_head_masked(qi_ref, proj(c, d_qi)); c += d_qi
    ki_ref[0] = proj(c, LANES).astype(BF16); c += LANES
    wit_ref[0] = proj(c, LANES).T[:IDX_HEADS, :]; c += LANES
    _store_head_masked(qb_ref, proj(c, d_b)); c += d_b
    kb_ref[0] = proj(c, d_b).astype(BF16); c += d_b
    store_key_blocks_t(vbt_ref, proj(c, d_b)); c += d_b
    gb_ref[0] = silu(proj(c, d_b))


def _inproj(x, gain, w_all, qg, kg, gsum, d_a, d_b, d_qi):
    b, l, d = x.shape
    tm = TOKEN_TILE
    ts = ATT_TILE
    nb = l // ts
    row = lambda width: pl.BlockSpec((1, tm, width), lambda i, j: (i, j, 0))
    const = lambda shape: pl.BlockSpec(shape, lambda i, j: (0, 0))
    vt_spec = lambda ch: pl.BlockSpec((1, tm // ts, ch, ts), lambda i, j: (i, j, 0, 0))
    tok = lambda width, dt: jax.ShapeDtypeStruct((b, l, width), dt)
    heads = lambda width: jax.ShapeDtypeStruct((b, width // HEAD_DIM, LANES, l), BF16)
    heads_spec = lambda width: pl.BlockSpec((1, width // HEAD_DIM, LANES, tm), lambda i, j: (i, 0, 0, j))
    out_shapes = [
        heads(d_a), tok(d_a, BF16), jax.ShapeDtypeStruct((b, nb, d_a, ts), BF16), tok(d_a, F32),
        heads(d_qi), tok(LANES, BF16), jax.ShapeDtypeStruct((b, IDX_HEADS, l), F32),
        heads(d_b), tok(d_b, BF16), jax.ShapeDtypeStruct((b, nb, d_b, ts), BF16), tok(d_b, F32),
    ]
    out_specs = [heads_spec(d_a), row(d_a), vt_spec(d_a), row(d_a), heads_spec(d_qi), row(LANES),
                 pl.BlockSpec((1, IDX_HEADS, tm), lambda i, j: (i, 0, j)),
                 heads_spec(d_b), row(d_b), vt_spec(d_b), row(d_b)]
    return pl.pallas_call(
        functools.partial(_inproj_kernel, d_a=d_a, d_b=d_b, d_qi=d_qi),
        grid=(b, l // tm),
        in_specs=[row(d), const(gain.shape),
                  pl.BlockSpec(w_all.shape, lambda i, j: (0, 0), pipeline_mode=pl.Buffered(1)),
                  const(qg.shape), const(kg.shape), const(gsum.shape)],
        out_specs=out_specs,
        out_shape=out_shapes,
        compiler_params=_params("arbitrary", "arbitrary"),
        name="inproj",
    )(x, gain, w_all, qg, kg, gsum)


def _t5_large_thresholds():
    max_exact = NUM_BUCKETS // 2
    d = np.arange(max_exact, 2 * MAX_DISTANCE + 1)
    large = max_exact + (np.log(d.astype(np.float32) / np.float32(max_exact))
                         / np.float32(math.log(MAX_DISTANCE / max_exact))
                         * np.float32(NUM_BUCKETS - max_exact)).astype(np.int32)
    large = np.minimum(large, NUM_BUCKETS - 1)
    return [int(d[np.argmax(large >= k)]) for k in range(max_exact + 1, NUM_BUCKETS)]


def _t5_bucket(dist):
    max_exact = NUM_BUCKETS // 2
    d = jnp.maximum(dist, 0)
    large = jnp.full(d.shape, max_exact, I32)
    for first in _t5_large_thresholds():
        large = large + (d >= first).astype(I32)
    return jnp.where(d < max_exact, d, large)


def _t5_table_kernel(rb_ref, tab_ref, *, n_heads):
    ts = tab_ref.shape[2]
    rows = 16
    far = _t5_large_thresholds()[-1]
    t_rel = lax.broadcasted_iota(I32, (rows, ts), 1)
    s_rel = lax.broadcasted_iota(I32, (rows, ts), 0)

    for i in range(3):
        n_far = max(0, min(ts, i * ts - far + 1)) // rows

        def fill(c, _, i=i):
            r0 = pl.multiple_of(c * rows, rows)
            for h in range(n_heads):
                tab_ref[i, h, pl.ds(r0, rows), :] = jnp.full((rows, ts), rb_ref[NUM_BUCKETS - 1, h] * LOG2E, F32)
            return 0

        def compute(c, _, i=i):
            r0 = pl.multiple_of(c * rows, rows)
            bucket = _t5_bucket(t_rel - (r0 + s_rel) + i * ts)
            acc = [jnp.zeros((rows, ts), F32) for _ in range(n_heads)]
            for j in range(NUM_BUCKETS):
                hit = bucket == j
                acc = [jnp.where(hit, rb_ref[j, h] * LOG2E, acc[h]) for h in range(n_heads)]
            for h in range(n_heads):
                tab_ref[i, h, pl.ds(r0, rows), :] = acc[h]
            return 0

        lax.fori_loop(0, n_far, fill, 0)
        lax.fori_loop(n_far, ts // rows, compute, 0)


def _t5_table(rel_bias, ts):
    n_heads = rel_bias.shape[1]
    return pl.pallas_call(
        functools.partial(_t5_table_kernel, n_heads=n_heads),
        in_specs=[pl.BlockSpec(memory_space=pltpu.SMEM)],
        out_specs=pl.BlockSpec(memory_space=pltpu.VMEM),
        out_shape=jax.ShapeDtypeStruct((3, n_heads, ts, ts), F32),
        compiler_params=pltpu.CompilerParams(vmem_limit_bytes=VMEM_LIMIT_BYTES),
        name="t5_table",
    )(rel_bias)


def _dsa_kernel(ki_ref, qin_ref, witn_ref, ka_ref, vat_ref, qam_ref, ga_ref, tab_ref, o_ref,
                sc_ref, mm_ref, madd_ref, m_ref, l_ref, acc_ref, lg_ref, bmax_ref, *, topk, n_heads):
    tq = o_ref.shape[1]
    ts = tq
    q_blk = pl.program_id(1)
    nblk = q_blk + 1
    cur = q_blk % 2
    nxt = 1 - cur
    s_rel = lax.broadcasted_iota(I32, (ts, tq), 0)
    t_rel = lax.broadcasted_iota(I32, (ts, tq), 1)
    t_idx = q_blk * tq + t_rel

    def block_start(j):
        return pl.multiple_of(j * ts, ts)

    all_selected = nblk * tq <= topk
    n_sel = jnp.where(all_selected, 0, nblk)

    @pl.when(all_selected)
    def _():
        mm_ref[cur] = jnp.zeros(mm_ref.shape[1:], F32)

        def body(j, _):
            s0 = block_start(j)
            madd_ref[pl.ds(s0, ts), :] = jnp.where(s0 + s_rel > t_idx, NEG_BIG, 0.0)
            return 0

        lax.fori_loop(0, nblk, body, 0)

    vmin = jnp.min(mm_ref[cur, 0], axis=0, keepdims=True)
    vmax = jnp.max(mm_ref[cur, 1], axis=0, keepdims=True)

    def count(pred_fn):
        def body(j, c):
            s0 = block_start(j)
            hit = pred_fn(sc_ref[cur, pl.ds(s0, ts), :], s0).reshape(ts // SUBLANES, SUBLANES, tq)
            c = list(c)
            for r in range(ts // SUBLANES):
                c[r % 4] = jnp.where(hit[r], c[r % 4] + 1, c[r % 4])
            return tuple(c)
        z = jnp.zeros((SUBLANES, tq), I32)
        c8 = lax.fori_loop(0, n_sel, body, (z, z, z, z))
        return jnp.sum(c8[0] + c8[1] + c8[2] + c8[3], axis=0, keepdims=True)

    def midpoint(lo, hi):
        return 0.5 * lo + 0.5 * hi

    def searching(lo, hi, cnt_lo):
        mid = midpoint(lo, hi)
        return (cnt_lo > topk) & (mid > lo) & (mid < hi)

    def status(lo, hi, cnt_lo):
        flags = 2.0 * searching(lo, hi, cnt_lo).astype(F32) + (cnt_lo > topk).astype(F32)
        return jnp.max(flags)

    n_causal = t_idx[0:1, :] + 1
    lo0 = jnp.where(n_causal > topk, vmin, -jnp.inf)
    hi0 = vmax + (jnp.abs(vmax) * 2.0 ** -20 + 1e-30)

    def bisect_cond(carry):
        it, _, _, _, state = carry
        return (state >= 2) & (it < BISECT_CAP)

    def bisect_step(_, carry):
        lo, hi, cnt_lo = carry
        go = searching(lo, hi, cnt_lo)
        mid = midpoint(lo, hi)
        cnt = count(lambda blk, s0: blk >= mid)
        take = go & (cnt >= topk)
        return (jnp.where(take, mid, lo), jnp.where(go & (cnt < topk), mid, hi),
                jnp.where(take, cnt, cnt_lo))

    def bisect_body(carry):
        it, lo, hi, cnt_lo, _ = carry
        lo, hi, cnt_lo = lax.fori_loop(0, BISECT_GROUP, bisect_step, (lo, hi, cnt_lo))
        return it + BISECT_GROUP, lo, hi, cnt_lo, status(lo, hi, cnt_lo)

    n_blind = jnp.where(all_selected, 0, BISECT_BLIND)
    lo, hi, cnt_lo = lax.fori_loop(0, n_blind, bisect_step, (lo0, hi0, n_causal))
    _, thr, hi, cnt_thr, state = lax.while_loop(
        bisect_cond, bisect_body,
        (jnp.int32(BISECT_BLIND), lo, hi, cnt_lo,
         jnp.where(all_selected, 0, status(lo, hi, cnt_lo))))

    tie = state >= 1

    @pl.when(jnp.logical_not(tie))
    def _():
        def body(j, _):
            s0 = block_start(j)
            sel = (sc_ref[cur, pl.ds(s0, ts), :] >= thr) & (s0 + s_rel <= t_idx)
            madd_ref[pl.ds(s0, ts), :] = jnp.where(sel, 0.0, NEG_BIG)
            return 0

        lax.fori_loop(0, n_sel, body, 0)

    @pl.when(tie)
    def _():
        need = topk - count(lambda blk, s0: blk > thr)
        n_bits = sc_ref.shape[1].bit_length()

        def jbody(i, bound):
            cand = bound + jnp.left_shift(jnp.int32(1), n_bits - 1 - i)
            cnt = count(lambda blk, s0: (blk == thr) & (s0 + s_rel < cand))
            return jnp.where(cnt <= need, cand, bound)

        bound = lax.fori_loop(0, n_bits, jbody, jnp.zeros((1, tq), I32))

        def body(j, _):
            s0 = block_start(j)
            blk = sc_ref[cur, pl.ds(s0, ts), :]
            sel = ((blk > thr) | ((blk == thr) & (s0 + s_rel < bound))) & (s0 + s_rel <= t_idx)
            madd_ref[pl.ds(s0, ts), :] = jnp.where(sel, 0.0, NEG_BIG)
            return 0

        lax.fori_loop(0, n_sel, body, 0)

    m_ref[...] = jnp.full(m_ref.shape, NEG_BIG, F32)
    l_ref[...] = jnp.zeros(l_ref.shape, F32)
    acc_ref[...] = jnp.zeros(acc_ref.shape, F32)

    def attend(j):
        s0 = block_start(j)
        madd = madd_ref[pl.ds(s0, ts), :]
        near = jnp.minimum(q_blk - j, 2)
        for h in range(n_heads):
            lg = jnp.dot(ka_ref[0, pl.ds(s0, ts), _pair(h)], qam_ref[0, h],
                         preferred_element_type=F32)
            lg = lg + tab_ref[near, h] + madd
            lg_ref[h] = lg
            bmax_ref[h] = jnp.max(lg, axis=0, keepdims=True)
        for h in range(n_heads):
            rows = slice(h * HEAD_DIM, (h + 1) * HEAD_DIM)
            m_old = m_ref[h]
            m_new = jnp.maximum(m_old, bmax_ref[h])
            alpha = jnp.exp2(m_old - m_new)
            p = jnp.exp2(lg_ref[h] - m_new)
            l_ref[h] = alpha * l_ref[h] + jnp.sum(p, axis=0, keepdims=True)
            pv = jnp.dot(vat_ref[0, j, rows, :], p.astype(BF16), preferred_element_type=F32)
            acc_ref[rows, :] = alpha * acc_ref[rows, :] + pv
            m_ref[h] = m_new

    def index_next(j, carry, diagonal):
        vmin8, vmax8 = carry
        s0 = block_start(j)
        kblk = ki_ref[0, pl.ds(s0, ts), :]
        score = jnp.zeros((ts, tq), F32)
        for h in range(IDX_HEADS):
            dots = jnp.dot(kblk, qin_ref[0, h], preferred_element_type=F32)
            score = score + jnp.maximum(dots, 0.0) * witn_ref[0, h:h + 1, :]
        if diagonal:
            future = s_rel > t_rel
            low = jnp.where(future, jnp.inf, score)
            score = jnp.where(future, -jnp.inf, score)
        else:
            low = score
        sc_ref[nxt, pl.ds(s0, ts), :] = score
        groups = (ts // SUBLANES, SUBLANES, tq)
        return (jnp.minimum(vmin8, low.reshape(groups).min(axis=0)),
                jnp.maximum(vmax8, score.reshape(groups).max(axis=0)))

    def write_output():
        for h in range(n_heads):
            rows = slice(h * HEAD_DIM, (h + 1) * HEAD_DIM)
            acc_ref[rows, :] = acc_ref[rows, :] * (1.0 / l_ref[h])
        o_ref[0] = (acc_ref[...].T * ga_ref[0]).astype(BF16)

    has_next = q_blk + 1 < pl.num_programs(1)

    @pl.when(has_next)
    def _():
        def pair(i, carry):
            for j in (2 * i, 2 * i + 1):
                attend(j)
                carry = index_next(j, carry, diagonal=False)
            return carry

        def single(j, carry):
            attend(j)
            return index_next(j, carry, diagonal=False)

        carry = lax.fori_loop(0, nblk // 2, pair, (jnp.full((SUBLANES, tq), jnp.inf, F32),
                                                   jnp.full((SUBLANES, tq), -jnp.inf, F32)))
        carry = lax.fori_loop(2 * (nblk // 2), nblk, single, carry)
        vmin8, vmax8 = index_next(nblk, carry, diagonal=True)
        mm_ref[nxt, 0] = vmin8
        mm_ref[nxt, 1] = vmax8
        write_output()

    @pl.when(jnp.logical_not(has_next))
    def _():
        def body(j, _):
            attend(j)
            return 0

        lax.fori_loop(0, nblk, body, 0)
        write_output()


def _dsa(ki, qim, wit, ka, vat, qam, ga, tab, topk):
    b, l, d_a = ka.shape
    n_heads = d_a // HEAD_DIM
    tq = ATT_TILE
    nq = l // tq
    assert tq <= topk, "the first query tile must not need indexer scores"
    next_tile = lambda j: jnp.minimum(j + 1, nq - 1)
    kernel = functools.partial(_dsa_kernel, topk=topk, n_heads=n_heads)
    return pl.pallas_call(
        kernel,
        grid=(b, nq),
        in_specs=[
            pl.BlockSpec((1, l, LANES), lambda i, j: (i, 0, 0)),
            pl.BlockSpec((1, IDX_HEADS, LANES, tq), lambda i, j: (i, 0, 0, next_tile(j))),
            pl.BlockSpec((1, IDX_HEADS, tq), lambda i, j: (i, 0, next_tile(j))),
            pl.BlockSpec((1, l, d_a), lambda i, j: (i, 0, 0)),
            pl.BlockSpec((1, nq, d_a, tq), lambda i, j: (i, 0, 0, 0)),
            pl.BlockSpec((1, n_heads, LANES, tq), lambda i, j: (i, 0, 0, j)),
            pl.BlockSpec((1, tq, d_a), lambda i, j: (i, j, 0)),
            pl.BlockSpec(tab.shape, lambda i, j: (0, 0, 0, 0), pipeline_mode=pl.Buffered(1)),
        ],
        out_specs=pl.BlockSpec((1, tq, d_a), lambda i, j: (i, j, 0)),
        out_shape=jax.ShapeDtypeStruct((b, l, d_a), BF16),
        scratch_shapes=[
            pltpu.VMEM((2, l, tq), F32),
            pltpu.VMEM((2, 2, SUBLANES, tq), F32),
            pltpu.VMEM((l, tq), F32),
            pltpu.VMEM((n_heads, 1, tq), F32),
            pltpu.VMEM((n_heads, 1, tq), F32),
            pltpu.VMEM((d_a, tq), F32),
            pltpu.VMEM((n_heads, tq, tq), F32),
            pltpu.VMEM((n_heads, 1, tq), F32),
        ],
        compiler_params=_params("arbitrary", "arbitrary"),
        name="dsa",
    )(ki, qim, wit, ka, vat, qam, ga, tab)


def _stick_kernel(kb_ref, vbt_ref, qbm_ref, gb_ref, u_ref, x_ref, oa_ref, wa_ref, wb_ref, y_ref,
                  below_ref, acc_ref, z_ref, lb_ref, top_ref, *, n_heads):
    tq = y_ref.shape[1]
    ts = tq
    q_blk = pl.program_id(1)
    s_rel = lax.broadcasted_iota(I32, (ts, tq), 0)
    t_rel = lax.broadcasted_iota(I32, (ts, tq), 1)

    below_ref[...] = jnp.zeros(below_ref.shape, F32)
    acc_ref[...] = jnp.zeros(acc_ref.shape, F32)

    n_chunks = 4
    chunk = y_ref.shape[2] // n_chunks

    def project_a(c):
        cols = slice(c * chunk, (c + 1) * chunk)
        y_ref[0, :, cols] = x_ref[0, :, cols] + jnp.dot(oa_ref[0], wa_ref[:, cols],
                                                        preferred_element_type=F32)

    def block(j, diagonal, fill=()):
        fill_after = {(k + 1) * n_heads // len(fill) - 1: c for k, c in enumerate(fill)}
        s0 = pl.multiple_of(j * ts, ts)
        strict = s_rel < t_rel

        def z_dot(h):
            return jnp.dot(kb_ref[0, pl.ds(s0, ts), _pair(h)], qbm_ref[0, h],
                           preferred_element_type=F32)

        def terms(z2):
            log_beta = jnp.minimum(z2, 0.0) - jnp.log2(1.0 + jnp.exp2(-jnp.abs(z2)))
            log_om = log_beta - z2
            if diagonal:
                log_om = jnp.where(strict, log_om, 0.0)
            suffix = jnp.dot(u_ref[...], log_om.astype(BF16), preferred_element_type=F32)
            return log_beta, log_om[0:1, :], suffix

        def finish(h, log_a, below):
            rows = slice(h * HEAD_DIM, (h + 1) * HEAD_DIM)
            a = jnp.exp2(log_a)
            if diagonal:
                a = jnp.where(strict, a, 0.0)
            pv = jnp.dot(vbt_ref[0, j, rows, :], a.astype(BF16), preferred_element_type=F32)
            acc_ref[rows, :] += pv * jnp.exp2(below)

        for h in range(n_heads):
            z_ref[h] = z_dot(h)
        below, worst = [], None
        for h in range(n_heads):
            log_beta, om_row, suffix = terms(z_ref[h])
            lb_ref[h] = log_beta + suffix
            below.append(below_ref[h])
            after = below[h] + om_row + suffix[0:1, :]
            below_ref[h] = after
            worst = after if worst is None else jnp.maximum(worst, after)
            if h in fill_after:
                project_a(fill_after[h])
        top = jnp.max(worst)
        for h in range(n_heads):
            finish(h, lb_ref[h], below[h])
        return top

    @pl.when(q_blk == 0)
    def _():
        top_ref[0] = block(q_blk, diagonal=True, fill=(0, 1, 2, 3))

    @pl.when(q_blk > 0)
    def _():
        block(q_blk, diagonal=True, fill=(0, 1))
        top_ref[0] = block(q_blk - 1, diagonal=False, fill=(2, 3))

    def cond(carry):
        i, top = carry
        return (i <= q_blk) & (top >= EXP2_UNDERFLOW)

    def body(carry):
        i, _ = carry
        return i + 1, block(q_blk - i, diagonal=False)

    lax.while_loop(cond, body, (jnp.int32(2), top_ref[0]))

    ob = (acc_ref[...].T * gb_ref[0]).astype(BF16)
    y_ref[0] += jnp.dot(ob, wb_ref[...], preferred_element_type=F32)


def _stick(kb, vbt, qbm, gb, u2, x, oa, wa, wb):
    b, l, d_b = kb.shape
    d = x.shape[2]
    n_heads = d_b // HEAD_DIM
    tq = ATT_TILE
    nq = l // tq
    return pl.pallas_call(
        functools.partial(_stick_kernel, n_heads=n_heads),
        grid=(b, nq),
        in_specs=[
            pl.BlockSpec((1, l, d_b), lambda i, j: (i, 0, 0)),
            pl.BlockSpec((1, nq, d_b, tq), lambda i, j: (i, 0, 0, 0)),
            pl.BlockSpec((1, n_heads, LANES, tq), lambda i, j: (i, 0, 0, j)),
            pl.BlockSpec((1, tq, d_b), lambda i, j: (i, j, 0)),
            pl.BlockSpec(u2.shape, lambda i, j: (0, 0)),
            pl.BlockSpec((1, tq, d), lambda i, j: (i, j, 0)),
            pl.BlockSpec((1, tq, oa.shape[2]), lambda i, j: (i, j, 0)),
            pl.BlockSpec(wa.shape, lambda i, j: (0, 0)),
            pl.BlockSpec(wb.shape, lambda i, j: (0, 0)),
        ],
        out_specs=pl.BlockSpec((1, tq, d), lambda i, j: (i, j, 0)),
        out_shape=jax.ShapeDtypeStruct((b, l, d), F32),
        scratch_shapes=[
            pltpu.VMEM((n_heads, 1, tq), F32),
            pltpu.VMEM((d_b, tq), F32),
            pltpu.VMEM((n_heads, tq, tq), F32),
            pltpu.VMEM((n_heads, tq, tq), F32),
            pltpu.SMEM((1,), F32),
        ],
        compiler_params=_params("arbitrary", "arbitrary"),
        name="stick",
    )(kb, vbt, qbm, gb, u2, x, oa, wa, wb)


def kernel(x, norm_gain, w_in, q_norm_gain, k_norm_gain, rel_bias, w_out):
    b, l, d = x.shape
    depth = w_in.shape[0]
    d_a = d // 2
    d_b = d - d_a
    h_a = d_a // HEAD_DIM
    d_qi = IDX_HEADS * IDX_DIM
    topk = min(TOPK_MAX, l // 4)
    ts = ATT_TILE
    scale = HEAD_DIM ** -0.5 * LOG2E
    idx_scale = (IDX_HEADS * IDX_DIM) ** -0.5

    lane = jnp.arange(LANES)
    gsum = (lane[:, None] // HEAD_DIM == lane[None, :] // HEAD_DIM).astype(BF16)
    u2 = (jnp.arange(ts)[None, :] > jnp.arange(ts)[:, None]).astype(BF16)
    tab = _t5_table(rel_bias.astype(F32), ts)

    for layer in range(depth):
        w_all = _pack_w_in(jnp.swapaxes(w_in[layer], 0, 1), d_a, d_b, d_qi, scale, idx_scale)
        qg = jnp.tile(q_norm_gain[layer] * scale, h_a)[None, :]
        kg = jnp.tile(k_norm_gain[layer], h_a)[None, :]

        (qam, ka, vat, ga, qim, ki, wit, qbm, kb, vbt, gb) = _inproj(
            x, norm_gain[layer][None, :], w_all, qg, kg, gsum, d_a, d_b, d_qi)
        oa = _dsa(ki, qim, wit, ka, vat, qam, ga, tab, topk)
        w_o = w_out[layer].astype(BF16)
        x = _stick(kb, vbt, qbm, gb, u2, x, oa, w_o[:d_a], w_o[d_a:])
    return x
```

```python
import functools
import math

import jax
import jax.numpy as jnp
import numpy as np
from jax import lax
from jax.experimental import pallas as pl
from jax.experimental.pallas import tpu as pltpu

F32 = jnp.float32
BF16 = jnp.bfloat16
I32 = jnp.int32

HEAD_DIM = 64
IDX_HEADS = 16
IDX_DIM = 64
TOPK_MAX = 256
NUM_BUCKETS = 32
MAX_DISTANCE = 128
RMS_EPS = 1e-6

LANES = 128
SUBLANES = 8
NEG_BIG = -1e30
EXP2_UNDERFLOW = -150.0
LOG2E = math.log2(math.e)
BISECT_CAP = 300
BISECT_BLIND = 16
BISECT_GROUP = 2
BISECT_INTERP = 8
INTERP_MARGIN = 0.05

VMEM_LIMIT_BYTES = 56 * 1024 * 1024
TOKEN_TILE = 512
ATT_TILE = 256


def _params(*sem):
    return pltpu.CompilerParams(dimension_semantics=sem, vmem_limit_bytes=VMEM_LIMIT_BYTES)


def _pair(h):
    return slice((h // 2) * LANES, (h // 2 + 1) * LANES)


def _store_head_masked(dst_ref, src):
    rows = src.shape[0]
    low = lax.broadcasted_iota(I32, (rows, LANES), 1) < HEAD_DIM
    for h in range(dst_ref.shape[1]):
        keep = low if h % 2 == 0 else jnp.logical_not(low)
        dst_ref[0, h] = jnp.where(keep, src[:, _pair(h)], 0.0).T.astype(BF16)


def _pack_w_in_kernel(wt_ref, o_ref, *, d_a, d_b, d_qi, scale, idx_scale):
    o_ki = 4 * d_a + d_qi
    o_wi = o_ki + IDX_DIM
    o_b = o_wi + IDX_HEADS
    d = wt_ref.shape[1]

    def put(dst, rows):
        o_ref[:, dst:dst + LANES] = rows.T.astype(BF16)

    for c in range(0, o_ki, LANES):
        put(c, wt_ref[c:c + LANES, :])
    k_idx = wt_ref[o_ki:o_wi, :]
    put(o_ki, jnp.concatenate([k_idx, k_idx], axis=0))
    w_idx = wt_ref[o_wi:o_b, :] * idx_scale
    put(o_ki + LANES, jnp.concatenate([w_idx, jnp.zeros((LANES - IDX_HEADS, d), F32)], axis=0))
    for c in range(0, 4 * d_b, LANES):
        rows = wt_ref[o_b + c:o_b + c + LANES, :]
        put(o_ki + 2 * LANES + c, rows * scale if c < d_b else rows)


def _pack_w_in(w_t, d_a, d_b, d_qi, scale, idx_scale):
    p, d = w_t.shape
    width = p + 2 * LANES - IDX_DIM - IDX_HEADS
    return pl.pallas_call(
        functools.partial(_pack_w_in_kernel, d_a=d_a, d_b=d_b, d_qi=d_qi, scale=scale,
                          idx_scale=idx_scale),
        out_shape=jax.ShapeDtypeStruct((d, width), BF16),
        compiler_params=pltpu.CompilerParams(vmem_limit_bytes=VMEM_LIMIT_BYTES),
        name="pack_w_in",
    )(w_t)


def _inproj_kernel(x_ref, gain_ref, w_ref, qg_ref, kg_ref, gsum_ref,
                   qa_ref, ka_ref, vat_ref, ga_ref, qi_ref, ki_ref, wit_ref,
                   qb_ref, kb_ref, vbt_ref, gb_ref, *, d_a, d_b, d_qi):
    x = x_ref[0]
    ms = jnp.mean(x * x, axis=-1, keepdims=True)
    h = (x * lax.rsqrt(ms + RMS_EPS) * gain_ref[...]).astype(BF16)
    ts = vat_ref.shape[3]

    def proj(c0, width):
        return jnp.dot(h, w_ref[:, c0:c0 + width], preferred_element_type=F32)

    def head_norm(y, g):
        sq = (y * y).astype(BF16)
        ones = gsum_ref[...]
        ssum = jnp.concatenate(
            [jnp.dot(sq[:, c:c + LANES], ones, preferred_element_type=F32)
             for c in range(0, y.shape[1], LANES)], axis=1)
        return y * lax.rsqrt(ssum * (1.0 / HEAD_DIM) + RMS_EPS) * g

    def silu(g):
        return g * (1.0 / (1.0 + jnp.exp(-g)))

    def store_key_blocks_t(dst_ref, v):
        vt = v.T.astype(BF16)
        for c in range(dst_ref.shape[1]):
            dst_ref[0, c] = vt[:, c * ts:(c + 1) * ts]

    c = 0
    _store_head_masked(qa_ref, head_norm(proj(c, d_a), qg_ref[...])); c += d_a
    ka_ref[0] = head_norm(proj(c, d_a), kg_ref[...]).astype(BF16); c += d_a
    store_key_blocks_t(vat_ref, proj(c, d_a)); c += d_a
    ga_ref[0] = silu(proj(c, d_a)); c += d_a
    _store_head_masked(qi_ref, proj(c, d_qi)); c += d_qi
    ki_ref[0] = proj(c, LANES).astype(BF16); c += LANES
    wit_ref[0] = proj(c, LANES).T[:IDX_HEADS, :]; c += LANES
    _store_head_masked(qb_ref, proj(c, d_b)); c += d_b
    kb_ref[0] = proj(c, d_b).astype(BF16); c += d_b
    store_key_blocks_t(vbt_ref, proj(c, d_b)); c += d_b
    gb_ref[0] = silu(proj(c, d_b))


def _inproj(x, gain, w_all, qg, kg, gsum, d_a, d_b, d_qi):
    b, l, d = x.shape
    tm = TOKEN_TILE
    ts = ATT_TILE
    nb = l // ts
    row = lambda width: pl.BlockSpec((1, tm, width), lambda i, j: (i, j, 0))
    const = lambda shape: pl.BlockSpec(shape, lambda i, j: (0, 0))
    vt_spec = lambda ch: pl.BlockSpec((1, tm // ts, ch, ts), lambda i, j: (i, j, 0, 0))
    tok = lambda width, dt: jax.ShapeDtypeStruct((b, l, width), dt)
    heads = lambda width: jax.ShapeDtypeStruct((b, width // HEAD_DIM, LANES, l), BF16)
    heads_spec = lambda width: pl.BlockSpec((1, width // HEAD_DIM, LANES, tm), lambda i, j: (i, 0, 0, j))
    out_shapes = [
        heads(d_a), tok(d_a, BF16), jax.ShapeDtypeStruct((b, nb, d_a, ts), BF16), tok(d_a, F32),
        heads(d_qi), tok(LANES, BF16), jax.ShapeDtypeStruct((b, IDX_HEADS, l), F32),
        heads(d_b), tok(d_b, BF16), jax.ShapeDtypeStruct((b, nb, d_b, ts), BF16), tok(d_b, F32),
    ]
    out_specs = [heads_spec(d_a), row(d_a), vt_spec(d_a), row(d_a), heads_spec(d_qi), row(LANES),
                 pl.BlockSpec((1, IDX_HEADS, tm), lambda i, j: (i, 0, j)),
                 heads_spec(d_b), row(d_b), vt_spec(d_b), row(d_b)]
    return pl.pallas_call(
        functools.partial(_inproj_kernel, d_a=d_a, d_b=d_b, d_qi=d_qi),
        grid=(b, l // tm),
        in_specs=[row(d), const(gain.shape),
                  pl.BlockSpec(w_all.shape, lambda i, j: (0, 0), pipeline_mode=pl.Buffered(1)),
                  const(qg.shape), const(kg.shape), const(gsum.shape)],
        out_specs=out_specs,
        out_shape=out_shapes,
        compiler_params=_params("arbitrary", "arbitrary"),
        name="inproj",
    )(x, gain, w_all, qg, kg, gsum)


def _t5_large_thresholds():
    max_exact = NUM_BUCKETS // 2
    d = np.arange(max_exact, 2 * MAX_DISTANCE + 1)
    large = max_exact + (np.log(d.astype(np.float32) / np.float32(max_exact))
                         / np.float32(math.log(MAX_DISTANCE / max_exact))
                         * np.float32(NUM_BUCKETS - max_exact)).astype(np.int32)
    large = np.minimum(large, NUM_BUCKETS - 1)
    return [int(d[np.argmax(large >= k)]) for k in range(max_exact + 1, NUM_BUCKETS)]


def _t5_bucket(dist):
    max_exact = NUM_BUCKETS // 2
    d = jnp.maximum(dist, 0)
    large = jnp.full(d.shape, max_exact, I32)
    for first in _t5_large_thresholds():
        large = large + (d >= first).astype(I32)
    return jnp.where(d < max_exact, d, large)


def _t5_table_kernel(rb_ref, tab_ref, *, n_heads):
    ts = tab_ref.shape[2]
    rows = 16
    far = _t5_large_thresholds()[-1]
    t_rel = lax.broadcasted_iota(I32, (rows, ts), 1)
    s_rel = lax.broadcasted_iota(I32, (rows, ts), 0)

    for i in range(3):
        n_far = max(0, min(ts, i * ts - far + 1)) // rows

        def fill(c, _, i=i):
            r0 = pl.multiple_of(c * rows, rows)
            for h in range(n_heads):
                tab_ref[i, h, pl.ds(r0, rows), :] = jnp.full((rows, ts), rb_ref[NUM_BUCKETS - 1, h] * LOG2E, F32)
            return 0

        def compute(c, _, i=i):
            r0 = pl.multiple_of(c * rows, rows)
            bucket = _t5_bucket(t_rel - (r0 + s_rel) + i * ts)
            acc = [jnp.zeros((rows, ts), F32) for _ in range(n_heads)]
            for j in range(NUM_BUCKETS):
                hit = bucket == j
                acc = [jnp.where(hit, rb_ref[j, h] * LOG2E, acc[h]) for h in range(n_heads)]
            for h in range(n_heads):
                tab_ref[i, h, pl.ds(r0, rows), :] = acc[h]
            return 0

        lax.fori_loop(0, n_far, fill, 0)
        lax.fori_loop(n_far, ts // rows, compute, 0)


def _t5_table(rel_bias, ts):
    n_heads = rel_bias.shape[1]
    return pl.pallas_call(
        functools.partial(_t5_table_kernel, n_heads=n_heads),
        in_specs=[pl.BlockSpec(memory_space=pltpu.SMEM)],
        out_specs=pl.BlockSpec(memory_space=pltpu.VMEM),
        out_shape=jax.ShapeDtypeStruct((3, n_heads, ts, ts), F32),
        compiler_params=pltpu.CompilerParams(vmem_limit_bytes=VMEM_LIMIT_BYTES),
        name="t5_table",
    )(rel_bias)


def _dsa_kernel(ki_ref, qin_ref, witn_ref, ka_ref, vat_ref, qam_ref, ga_ref, tab_ref, o_ref,
                sc_ref, mm_ref, madd_ref, m_ref, l_ref, acc_ref, lg_ref, bmax_ref, *, topk, n_heads):
    tq = o_ref.shape[1]
    ts = tq
    q_blk = pl.program_id(1)
    nblk = q_blk + 1
    cur = q_blk % 2
    nxt = 1 - cur
    s_rel = lax.broadcasted_iota(I32, (ts, tq), 0)
    t_rel = lax.broadcasted_iota(I32, (ts, tq), 1)
    t_idx = q_blk * tq + t_rel

    def block_start(j):
        return pl.multiple_of(j * ts, ts)

    all_selected = nblk * tq <= topk
    n_sel = jnp.where(all_selected, 0, nblk)

    @pl.when(all_selected)
    def _():
        mm_ref[cur] = jnp.zeros(mm_ref.shape[1:], F32)

        def body(j, _):
            s0 = block_start(j)
            madd_ref[pl.ds(s0, ts), :] = jnp.where(s0 + s_rel > t_idx, NEG_BIG, 0.0)
            return 0

        lax.fori_loop(0, nblk, body, 0)

    vmin = jnp.min(mm_ref[cur, 0], axis=0, keepdims=True)
    vmax = jnp.max(mm_ref[cur, 1], axis=0, keepdims=True)

    def count(pred_fn):
        def body(j, c):
            s0 = block_start(j)
            hit = pred_fn(sc_ref[cur, pl.ds(s0, ts), :], s0).reshape(ts // SUBLANES, SUBLANES, tq)
            c = list(c)
            for r in range(ts // SUBLANES):
                c[r % 4] = jnp.where(hit[r], c[r % 4] + 1, c[r % 4])
            return tuple(c)
        z = jnp.zeros((SUBLANES, tq), I32)
        c8 = lax.fori_loop(0, n_sel, body, (z, z, z, z))
        return jnp.sum(c8[0] + c8[1] + c8[2] + c8[3], axis=0, keepdims=True)

    def midpoint(lo, hi):
        return 0.5 * lo + 0.5 * hi

    def searching(lo, hi, cnt_lo):
        mid = midpoint(lo, hi)
        return (cnt_lo > topk) & (mid > lo) & (mid < hi)

    def status(lo, hi, cnt_lo):
        flags = 2.0 * searching(lo, hi, cnt_lo).astype(F32) + (cnt_lo > topk).astype(F32)
        return jnp.max(flags)

    n_causal = t_idx[0:1, :] + 1
    lo0 = jnp.where(n_causal > topk, vmin, -jnp.inf)
    hi0 = vmax + (jnp.abs(vmax) * 2.0 ** -20 + 1e-30)

    def bisect_cond(carry):
        it, _, _, _, state = carry
        return (state >= 2) & (it < BISECT_CAP)

    def bisect_step(_, carry):
        lo, hi, cnt_lo = carry
        go = searching(lo, hi, cnt_lo)
        mid = midpoint(lo, hi)
        cnt = count(lambda blk, s0: blk >= mid)
        take = go & (cnt >= topk)
        return (jnp.where(take, mid, lo), jnp.where(go & (cnt < topk), mid, hi),
                jnp.where(take, cnt, cnt_lo))

    def bisect_body(carry):
        it, lo, hi, cnt_lo, _ = carry
        lo, hi, cnt_lo = lax.fori_loop(0, BISECT_GROUP, bisect_step, (lo, hi, cnt_lo))
        return it + BISECT_GROUP, lo, hi, cnt_lo, status(lo, hi, cnt_lo)

    def interp_step(_, carry):
        lo, hi, cnt_lo, cnt_hi = carry
        frac = (cnt_lo - topk).astype(F32) + 0.5
        frac = frac / jnp.maximum(cnt_lo - cnt_hi, 1).astype(F32)
        frac = jnp.clip(frac, INTERP_MARGIN, 1.0 - INTERP_MARGIN)
        base = jnp.maximum(lo, vmin)
        probe = base + (hi - base) * frac
        go = (cnt_lo > topk) & (probe > lo) & (probe < hi)
        cnt = count(lambda blk, s0: blk >= probe)
        take = go & (cnt >= topk)
        drop = go & (cnt < topk)
        return (jnp.where(take, probe, lo), jnp.where(drop, probe, hi),
                jnp.where(take, cnt, cnt_lo), jnp.where(drop, cnt, cnt_hi))

    n_interp = jnp.where(all_selected, 0, BISECT_INTERP)
    n_blind = jnp.where(all_selected, 0, BISECT_BLIND - BISECT_INTERP)
    lo, hi, cnt_lo, _ = lax.fori_loop(0, n_interp, interp_step,
                                      (lo0, hi0, n_causal, jnp.zeros_like(n_causal)))
    lo, hi, cnt_lo = lax.fori_loop(0, n_blind, bisect_step, (lo, hi, cnt_lo))
    _, thr, hi, cnt_thr, state = lax.while_loop(
        bisect_cond, bisect_body,
        (jnp.int32(BISECT_BLIND), lo, hi, cnt_lo,
         jnp.where(all_selected, 0, status(lo, hi, cnt_lo))))

    tie = state >= 1

    @pl.when(jnp.logical_not(tie))
    def _():
        def body(j, _):
            s0 = block_start(j)
            sel = (sc_ref[cur, pl.ds(s0, ts), :] >= thr) & (s0 + s_rel <= t_idx)
            madd_ref[pl.ds(s0, ts), :] = jnp.where(sel, 0.0, NEG_BIG)
            return 0

        lax.fori_loop(0, n_sel, body, 0)

    @pl.when(tie)
    def _():
        need = topk - count(lambda blk, s0: blk > thr)
        n_bits = sc_ref.shape[1].bit_length()

        def jbody(i, bound):
            cand = bound + jnp.left_shift(jnp.int32(1), n_bits - 1 - i)
            cnt = count(lambda blk, s0: (blk == thr) & (s0 + s_rel < cand))
            return jnp.where(cnt <= need, cand, bound)

        bound = lax.fori_loop(0, n_bits, jbody, jnp.zeros((1, tq), I32))

        def body(j, _):
            s0 = block_start(j)
            blk = sc_ref[cur, pl.ds(s0, ts), :]
            sel = ((blk > thr) | ((blk == thr) & (s0 + s_rel < bound))) & (s0 + s_rel <= t_idx)
            madd_ref[pl.ds(s0, ts), :] = jnp.where(sel, 0.0, NEG_BIG)
            return 0

        lax.fori_loop(0, n_sel, body, 0)

    m_ref[...] = jnp.full(m_ref.shape, NEG_BIG, F32)
    l_ref[...] = jnp.zeros(l_ref.shape, F32)
    acc_ref[...] = jnp.zeros(acc_ref.shape, F32)

    def attend(j):
        s0 = block_start(j)
        madd = madd_ref[pl.ds(s0, ts), :]
        near = jnp.minimum(q_blk - j, 2)
        for h in range(n_heads):
            lg = jnp.dot(ka_ref[0, pl.ds(s0, ts), _pair(h)], qam_ref[0, h],
                         preferred_element_type=F32)
            lg = lg + tab_ref[near, h] + madd
            lg_ref[h] = lg
            bmax_ref[h] = jnp.max(lg, axis=0, keepdims=True)
        for h in range(n_heads):
            rows = slice(h * HEAD_DIM, (h + 1) * HEAD_DIM)
            m_old = m_ref[h]
            m_new = jnp.maximum(m_old, bmax_ref[h])
            alpha = jnp.exp2(m_old - m_new)
            p = jnp.exp2(lg_ref[h] - m_new)
            l_ref[h] = alpha * l_ref[h] + jnp.sum(p, axis=0, keepdims=True)
            pv = jnp.dot(vat_ref[0, j, rows, :], p.astype(BF16), preferred_element_type=F32)
            acc_ref[rows, :] = alpha * acc_ref[rows, :] + pv
            m_ref[h] = m_new

    def index_next(j, carry, diagonal):
        vmin8, vmax8 = carry
        s0 = block_start(j)
        kblk = ki_ref[0, pl.ds(s0, ts), :]
        score = jnp.zeros((ts, tq), F32)
        for h in range(IDX_HEADS):
            dots = jnp.dot(kblk, qin_ref[0, h], preferred_element_type=F32)
            score = score + jnp.maximum(dots, 0.0) * witn_ref[0, h:h + 1, :]
        if diagonal:
            future = s_rel > t_rel
            low = jnp.where(future, jnp.inf, score)
            score = jnp.where(future, -jnp.inf, score)
        else:
            low = score
        sc_ref[nxt, pl.ds(s0, ts), :] = score
        groups = (ts // SUBLANES, SUBLANES, tq)
        return (jnp.minimum(vmin8, low.reshape(groups).min(axis=0)),
                jnp.maximum(vmax8, score.reshape(groups).max(axis=0)))

    def write_output():
        for h in range(n_heads):
            rows = slice(h * HEAD_DIM, (h + 1) * HEAD_DIM)
            acc_ref[rows, :] = acc_ref[rows, :] * (1.0 / l_ref[h])
        o_ref[0] = (acc_ref[...].T * ga_ref[0]).astype(BF16)

    has_next = q_blk + 1 < pl.num_programs(1)

    @pl.when(has_next)
    def _():
        def pair(i, carry):
            for j in (2 * i, 2 * i + 1):
                attend(j)
                carry = index_next(j, carry, diagonal=False)
            return carry

        def single(j, carry):
            attend(j)
            return index_next(j, carry, diagonal=False)

        carry = lax.fori_loop(0, nblk // 2, pair, (jnp.full((SUBLANES, tq), jnp.inf, F32),
                                                   jnp.full((SUBLANES, tq), -jnp.inf, F32)))
        carry = lax.fori_loop(2 * (nblk // 2), nblk, single, carry)
        vmin8, vmax8 = index_next(nblk, carry, diagonal=True)
        mm_ref[nxt, 0] = vmin8
        mm_ref[nxt, 1] = vmax8
        write_output()

    @pl.when(jnp.logical_not(has_next))
    def _():
        def body(j, _):
            attend(j)
            return 0

        lax.fori_loop(0, nblk, body, 0)
        write_output()


def _dsa(ki, qim, wit, ka, vat, qam, ga, tab, topk):
    b, l, d_a = ka.shape
    n_heads = d_a // HEAD_DIM
    tq = ATT_TILE
    nq = l // tq
    assert tq <= topk, "the first query tile must not need indexer scores"
    next_tile = lambda j: jnp.minimum(j + 1, nq - 1)
    kernel = functools.partial(_dsa_kernel, topk=topk, n_heads=n_heads)
    return pl.pallas_call(
        kernel,
        grid=(b, nq),
        in_specs=[
            pl.BlockSpec((1, l, LANES), lambda i, j: (i, 0, 0)),
            pl.BlockSpec((1, IDX_HEADS, LANES, tq), lambda i, j: (i, 0, 0, next_tile(j))),
            pl.BlockSpec((1, IDX_HEADS, tq), lambda i, j: (i, 0, next_tile(j))),
            pl.BlockSpec((1, l, d_a), lambda i, j: (i, 0, 0)),
            pl.BlockSpec((1, nq, d_a, tq), lambda i, j: (i, 0, 0, 0)),
            pl.BlockSpec((1, n_heads, LANES, tq), lambda i, j: (i, 0, 0, j)),
            pl.BlockSpec((1, tq, d_a), lambda i, j: (i, j, 0)),
            pl.BlockSpec(tab.shape, lambda i, j: (0, 0, 0, 0), pipeline_mode=pl.Buffered(1)),
        ],
        out_specs=pl.BlockSpec((1, tq, d_a), lambda i, j: (i, j, 0)),
        out_shape=jax.ShapeDtypeStruct((b, l, d_a), BF16),
        scratch_shapes=[
            pltpu.VMEM((2, l, tq), F32),
            pltpu.VMEM((2, 2, SUBLANES, tq), F32),
            pltpu.VMEM((l, tq), F32),
            pltpu.VMEM((n_heads, 1, tq), F32),
            pltpu.VMEM((n_heads, 1, tq), F32),
            pltpu.VMEM((d_a, tq), F32),
            pltpu.VMEM((n_heads, tq, tq), F32),
            pltpu.VMEM((n_heads, 1, tq), F32),
        ],
        compiler_params=_params("arbitrary", "arbitrary"),
        name="dsa",
    )(ki, qim, wit, ka, vat, qam, ga, tab)


def _stick_kernel(kb_ref, vbt_ref, qbm_ref, gb_ref, u_ref, x_ref, oa_ref, wa_ref, wb_ref, y_ref,
                  below_ref, acc_ref, z_ref, lb_ref, top_ref, *, n_heads):
    tq = y_ref.shape[1]
    ts = tq
    q_blk = pl.program_id(1)
    s_rel = lax.broadcasted_iota(I32, (ts, tq), 0)
    t_rel = lax.broadcasted_iota(I32, (ts, tq), 1)

    below_ref[...] = jnp.zeros(below_ref.shape, F32)
    acc_ref[...] = jnp.zeros(acc_ref.shape, F32)

    def block(j, diagonal):
        s0 = pl.multiple_of(j * ts, ts)
        strict = s_rel < t_rel

        def z_dot(h):
            return jnp.dot(kb_ref[0, pl.ds(s0, ts), _pair(h)], qbm_ref[0, h],
                           preferred_element_type=F32)

        def terms(z2):
            log_beta = jnp.minimum(z2, 0.0) - jnp.log2(1.0 + jnp.exp2(-jnp.abs(z2)))
            log_om = log_beta - z2
            if diagonal:
                log_om = jnp.where(strict, log_om, 0.0)
            suffix = jnp.dot(u_ref[...], log_om.astype(BF16), preferred_element_type=F32)
            return log_beta, log_om[0:1, :], suffix

        def finish(h, log_a, below):
            rows = slice(h * HEAD_DIM, (h + 1) * HEAD_DIM)
            a = jnp.exp2(log_a)
            if diagonal:
                a = jnp.where(strict, a, 0.0)
            pv = jnp.dot(vbt_ref[0, j, rows, :], a.astype(BF16), preferred_element_type=F32)
            acc_ref[rows, :] += pv * jnp.exp2(below)

        for h in range(n_heads):
            z_ref[h] = z_dot(h)
        below, worst = [], None
        for h in range(n_heads):
            log_beta, om_row, suffix = terms(z_ref[h])
            lb_ref[h] = log_beta + suffix
            below.append(below_ref[h])
            after = below[h] + om_row + suffix[0:1, :]
            below_ref[h] = after
            worst = after if worst is None else jnp.maximum(worst, after)
        top = jnp.max(worst)
        for h in range(n_heads):
            finish(h, lb_ref[h], below[h])
        return top

    @pl.when(q_blk == 0)
    def _():
        top_ref[0] = block(q_blk, diagonal=True)

    @pl.when(q_blk > 0)
    def _():
        block(q_blk, diagonal=True)
        top_ref[0] = block(q_blk - 1, diagonal=False)

    def cond(carry):
        i, top = carry
        return (i <= q_blk) & (top >= EXP2_UNDERFLOW)

    def body(carry):
        i, _ = carry
        return i + 1, block(q_blk - i, diagonal=False)

    lax.while_loop(cond, body, (jnp.int32(2), top_ref[0]))

    ob = (acc_ref[...].T * gb_ref[0]).astype(BF16)
    y_ref[0] = (x_ref[0]
                + jnp.dot(oa_ref[0], wa_ref[...], preferred_element_type=F32)
                + jnp.dot(ob, wb_ref[...], preferred_element_type=F32))


def _stick(kb, vbt, qbm, gb, u2, x, oa, wa, wb):
    b, l, d_b = kb.shape
    d = x.shape[2]
    n_heads = d_b // HEAD_DIM
    tq = ATT_TILE
    nq = l // tq
    return pl.pallas_call(
        functools.partial(_stick_kernel, n_heads=n_heads),
        grid=(b, nq),
        in_specs=[
            pl.BlockSpec((1, l, d_b), lambda i, j: (i, 0, 0)),
            pl.BlockSpec((1, nq, d_b, tq), lambda i, j: (i, 0, 0, 0)),
            pl.BlockSpec((1, n_heads, LANES, tq), lambda i, j: (i, 0, 0, j)),
            pl.BlockSpec((1, tq, d_b), lambda i, j: (i, j, 0)),
            pl.BlockSpec(u2.shape, lambda i, j: (0, 0)),
            pl.BlockSpec((1, tq, d), lambda i, j: (i, j, 0)),
            pl.BlockSpec((1, tq, oa.shape[2]), lambda i, j: (i, j, 0)),
            pl.BlockSpec(wa.shape, lambda i, j: (0, 0)),
            pl.BlockSpec(wb.shape, lambda i, j: (0, 0)),
        ],
        out_specs=pl.BlockSpec((1, tq, d), lambda i, j: (i, j, 0)),
        out_shape=jax.ShapeDtypeStruct((b, l, d), F32),
        scratch_shapes=[
            pltpu.VMEM((n_heads, 1, tq), F32),
            pltpu.VMEM((d_b, tq), F32),
            pltpu.VMEM((n_heads, tq, tq), F32),
            pltpu.VMEM((n_heads, tq, tq), F32),
            pltpu.SMEM((1,), F32),
        ],
        compiler_params=_params("arbitrary", "arbitrary"),
        name="stick",
    )(kb, vbt, qbm, gb, u2, x, oa, wa, wb)


def kernel(x, norm_gain, w_in, q_norm_gain, k_norm_gain, rel_bias, w_out):
    b, l, d = x.shape
    depth = w_in.shape[0]
    d_a = d // 2
    d_b = d - d_a
    h_a = d_a // HEAD_DIM
    d_qi = IDX_HEADS * IDX_DIM
    topk = min(TOPK_MAX, l // 4)
    ts = ATT_TILE
    scale = HEAD_DIM ** -0.5 * LOG2E
    idx_scale = (IDX_HEADS * IDX_DIM) ** -0.5

    lane = jnp.arange(LANES)
    gsum = (lane[:, None] // HEAD_DIM == lane[None, :] // HEAD_DIM).astype(BF16)
    u2 = (jnp.arange(ts)[None, :] > jnp.arange(ts)[:, None]).astype(BF16)
    tab = _t5_table(rel_bias.astype(F32), ts)

    for layer in range(depth):
        w_all = _pack_w_in(jnp.swapaxes(w_in[layer], 0, 1), d_a, d_b, d_qi, scale, idx_scale)
        qg = jnp.tile(q_norm_gain[layer] * scale, h_a)[None, :]
        kg = jnp.tile(k_norm_gain[layer], h_a)[None, :]

        (qam, ka, vat, ga, qim, ki, wit, qbm, kb, vbt, gb) = _inproj(
            x, norm_gain[layer][None, :], w_all, qg, kg, gsum, d_a, d_b, d_qi)
        oa = _dsa(ki, qim, wit, ka, vat, qam, ga, tab, topk)
        w_o = w_out[layer].astype(BF16)
        x = _stick(kb, vbt, qbm, gb, u2, x, oa, w_o[:d_a], w_o[d_a:])
    return x
```

```python
import functools
import math

import jax
import jax.numpy as jnp
import numpy as np
from jax import lax
from jax.experimental import pallas as pl
from jax.experimental.pallas import tpu as pltpu

F32 = jnp.float32
BF16 = jnp.bfloat16
I32 = jnp.int32

HEAD_DIM = 64
IDX_HEADS = 16
IDX_DIM = 64
TOPK_MAX = 256
NUM_BUCKETS = 32
MAX_DISTANCE = 128
RMS_EPS = 1e-6

LANES = 128
SUBLANES = 8
NEG_BIG = -1e30
EXP2_UNDERFLOW = -150.0
LOG2E = math.log2(math.e)
BISECT_CAP = 300
BISECT_BLIND = 16
BISECT_GROUP = 2
BISECT_INTERP = 8
INTERP_MARGIN = 0.05

VMEM_LIMIT_BYTES = 56 * 1024 * 1024
TOKEN_TILE = 512
ATT_TILE = 256


def _params(*sem):
    return pltpu.CompilerParams(dimension_semantics=sem, vmem_limit_bytes=VMEM_LIMIT_BYTES)


def _pair(h):
    return slice((h // 2) * LANES, (h // 2 + 1) * LANES)


def _store_head_masked(dst_ref, src):
    rows = src.shape[0]
    low = lax.broadcasted_iota(I32, (rows, LANES), 1) < HEAD_DIM
    for h in range(dst_ref.shape[1]):
        keep = low if h % 2 == 0 else jnp.logical_not(low)
        dst_ref[0, h] = jnp.where(keep, src[:, _pair(h)], 0.0).T.astype(BF16)


def _pack_w_in_kernel(wt_ref, o_ref, *, d_a, d_b, d_qi, scale, idx_scale):
    o_ki = 4 * d_a + d_qi
    o_wi = o_ki + IDX_DIM
    o_b = o_wi + IDX_HEADS
    d = wt_ref.shape[1]

    def put(dst, rows):
        o_ref[:, dst:dst + LANES] = rows.T.astype(BF16)

    for c in range(0, o_ki, LANES):
        put(c, wt_ref[c:c + LANES, :])
    k_idx = wt_ref[o_ki:o_wi, :]
    put(o_ki, jnp.concatenate([k_idx, k_idx], axis=0))
    w_idx = wt_ref[o_wi:o_b, :] * idx_scale
    put(o_ki + LANES, jnp.concatenate([w_idx, jnp.zeros((LANES - IDX_HEADS, d), F32)], axis=0))
    for c in range(0, 4 * d_b, LANES):
        rows = wt_ref[o_b + c:o_b + c + LANES, :]
        put(o_ki + 2 * LANES + c, rows * scale if c < d_b else rows)


def _pack_w_in(w_t, d_a, d_b, d_qi, scale, idx_scale):
    p, d = w_t.shape
    width = p + 2 * LANES - IDX_DIM - IDX_HEADS
    return pl.pallas_call(
        functools.partial(_pack_w_in_kernel, d_a=d_a, d_b=d_b, d_qi=d_qi, scale=scale,
                          idx_scale=idx_scale),
        out_shape=jax.ShapeDtypeStruct((d, width), BF16),
        compiler_params=pltpu.CompilerParams(vmem_limit_bytes=VMEM_LIMIT_BYTES),
        name="pack_w_in",
    )(w_t)


def _inproj_kernel(x_ref, gain_ref, w_ref, qg_ref, kg_ref, gsum_ref,
                   qa_ref, ka_ref, vat_ref, ga_ref, qi_ref, ki_ref, wit_ref,
                   qb_ref, kb_ref, vbt_ref, gb_ref, *, d_a, d_b, d_qi):
    x = x_ref[0]
    ms = jnp.mean(x * x, axis=-1, keepdims=True)
    h = (x * lax.rsqrt(ms + RMS_EPS) * gain_ref[...]).astype(BF16)
    ts = vat_ref.shape[3]

    def proj(c0, width):
        return jnp.dot(h, w_ref[:, c0:c0 + width], preferred_element_type=F32)

    def head_norm(y, g):
        sq = (y * y).astype(BF16)
        ones = gsum_ref[...]
        ssum = jnp.concatenate(
            [jnp.dot(sq[:, c:c + LANES], ones, preferred_element_type=F32)
             for c in range(0, y.shape[1], LANES)], axis=1)
        return y * lax.rsqrt(ssum * (1.0 / HEAD_DIM) + RMS_EPS) * g

    def silu(g):
        return g * (1.0 / (1.0 + jnp.exp(-g)))

    def store_key_blocks_t(dst_ref, v):
        vt = v.T.astype(BF16)
        for c in range(dst_ref.shape[1]):
            dst_ref[0, c] = vt[:, c * ts:(c + 1) * ts]

    c = 0
    _store_head_masked(qa_ref, head_norm(proj(c, d_a), qg_ref[...])); c += d_a
    ka_ref[0] = head_norm(proj(c, d_a), kg_ref[...]).astype(BF16); c += d_a
    store_key_blocks_t(vat_ref, proj(c, d_a)); c += d_a
    ga_ref[0] = silu(proj(c, d_a)); c += d_a
    _store_head_masked(qi_ref, proj(c, d_qi)); c += d_qi
    ki_ref[0] = proj(c, LANES).astype(BF16); c += LANES
    wit_ref[0] = proj(c, LANES).T[:IDX_HEADS, :]; c += LANES
    _store_head_masked(qb_ref, proj(c, d_b))
    store_key_blocks_t(vbt_ref, proj(c + 2 * d_b, d_b))
    gb_ref[0] = silu(proj(c + 3 * d_b, d_b))
    kb_ref[0] = proj(c + d_b, d_b).astype(BF16)


def _inproj(x, gain, w_all, qg, kg, gsum, d_a, d_b, d_qi):
    b, l, d = x.shape
    tm = TOKEN_TILE
    ts = ATT_TILE
    nb = l // ts
    row = lambda width: pl.BlockSpec((1, tm, width), lambda i, j: (i, j, 0))
    const = lambda shape: pl.BlockSpec(shape, lambda i, j: (0, 0))
    vt_spec = lambda ch: pl.BlockSpec((1, tm // ts, ch, ts), lambda i, j: (i, j, 0, 0))
    tok = lambda width, dt: jax.ShapeDtypeStruct((b, l, width), dt)
    heads = lambda width: jax.ShapeDtypeStruct((b, width // HEAD_DIM, LANES, l), BF16)
    heads_spec = lambda width: pl.BlockSpec((1, width // HEAD_DIM, LANES, tm), lambda i, j: (i, 0, 0, j))
    out_shapes = [
        heads(d_a), tok(d_a, BF16), jax.ShapeDtypeStruct((b, nb, d_a, ts), BF16), tok(d_a, F32),
        heads(d_qi), tok(LANES, BF16), jax.ShapeDtypeStruct((b, IDX_HEADS, l), F32),
        heads(d_b), tok(d_b, BF16), jax.ShapeDtypeStruct((b, nb, d_b, ts), BF16), tok(d_b, F32),
    ]
    out_specs = [heads_spec(d_a), row(d_a), vt_spec(d_a), row(d_a), heads_spec(d_qi), row(LANES),
                 pl.BlockSpec((1, IDX_HEADS, tm), lambda i, j: (i, 0, j)),
                 heads_spec(d_b), row(d_b), vt_spec(d_b), row(d_b)]
    return pl.pallas_call(
        functools.partial(_inproj_kernel, d_a=d_a, d_b=d_b, d_qi=d_qi),
        grid=(b, l // tm),
        in_specs=[row(d), const(gain.shape),
                  pl.BlockSpec(w_all.shape, lambda i, j: (0, 0), pipeline_mode=pl.Buffered(1)),
                  const(qg.shape), const(kg.shape), const(gsum.shape)],
        out_specs=out_specs,
        out_shape=out_shapes,
        compiler_params=_params("arbitrary", "arbitrary"),
        name="inproj",
    )(x, gain, w_all, qg, kg, gsum)


def _t5_large_thresholds():
    max_exact = NUM_BUCKETS // 2
    d = np.arange(max_exact, 2 * MAX_DISTANCE + 1)
    large = max_exact + (np.log(d.astype(np.float32) / np.float32(max_exact))
                         / np.float32(math.log(MAX_DISTANCE / max_exact))
                         * np.float32(NUM_BUCKETS - max_exact)).astype(np.int32)
    large = np.minimum(large, NUM_BUCKETS - 1)
    return [int(d[np.argmax(large >= k)]) for k in range(max_exact + 1, NUM_BUCKETS)]


def _t5_bucket(dist):
    max_exact = NUM_BUCKETS // 2
    d = jnp.maximum(dist, 0)
    large = jnp.full(d.shape, max_exact, I32)
    for first in _t5_large_thresholds():
        large = large + (d >= first).astype(I32)
    return jnp.where(d < max_exact, d, large)


def _t5_table_kernel(rb_ref, tab_ref, *, n_heads):
    ts = tab_ref.shape[2]
    rows = 16
    far = _t5_large_thresholds()[-1]
    t_rel = lax.broadcasted_iota(I32, (rows, ts), 1)
    s_rel = lax.broadcasted_iota(I32, (rows, ts), 0)

    for i in range(3):
        n_far = max(0, min(ts, i * ts - far + 1)) // rows

        def fill(c, _, i=i):
            r0 = pl.multiple_of(c * rows, rows)
            for h in range(n_heads):
                tab_ref[i, h, pl.ds(r0, rows), :] = jnp.full((rows, ts), rb_ref[NUM_BUCKETS - 1, h] * LOG2E, F32)
            return 0

        def compute(c, _, i=i):
            r0 = pl.multiple_of(c * rows, rows)
            bucket = _t5_bucket(t_rel - (r0 + s_rel) + i * ts)
            acc = [jnp.zeros((rows, ts), F32) for _ in range(n_heads)]
            for j in range(NUM_BUCKETS):
                hit = bucket == j
                acc = [jnp.where(hit, rb_ref[j, h] * LOG2E, acc[h]) for h in range(n_heads)]
            for h in range(n_heads):
                tab_ref[i, h, pl.ds(r0, rows), :] = acc[h]
            return 0

        lax.fori_loop(0, n_far, fill, 0)
        lax.fori_loop(n_far, ts // rows, compute, 0)


def _t5_table(rel_bias, ts):
    n_heads = rel_bias.shape[1]
    return pl.pallas_call(
        functools.partial(_t5_table_kernel, n_heads=n_heads),
        in_specs=[pl.BlockSpec(memory_space=pltpu.SMEM)],
        out_specs=pl.BlockSpec(memory_space=pltpu.VMEM),
        out_shape=jax.ShapeDtypeStruct((3, n_heads, ts, ts), F32),
        compiler_params=pltpu.CompilerParams(vmem_limit_bytes=VMEM_LIMIT_BYTES),
        name="t5_table",
    )(rel_bias)


def _dsa_kernel(ki_ref, qin_ref, witn_ref, ka_ref, vat_ref, qam_ref, ga_ref, tab_ref, o_ref,
                sc_ref, mm_ref, madd_ref, m_ref, l_ref, acc_ref, lg_ref, bmax_ref, *, topk, n_heads):
    tq = o_ref.shape[1]
    ts = tq
    q_blk = pl.program_id(1)
    nblk = q_blk + 1
    cur = q_blk % 2
    nxt = 1 - cur
    s_rel = lax.broadcasted_iota(I32, (ts, tq), 0)
    t_rel = lax.broadcasted_iota(I32, (ts, tq), 1)
    t_idx = q_blk * tq + t_rel

    def block_start(j):
        return pl.multiple_of(j * ts, ts)

    all_selected = nblk * tq <= topk
    n_sel = jnp.where(all_selected, 0, nblk)

    @pl.when(all_selected)
    def _():
        mm_ref[cur] = jnp.zeros(mm_ref.shape[1:], F32)

        def body(j, _):
            s0 = block_start(j)
            madd_ref[pl.ds(s0, ts), :] = jnp.where(s0 + s_rel > t_idx, NEG_BIG, 0.0)
            return 0

        lax.fori_loop(0, nblk, body, 0)

    vmin = jnp.min(mm_ref[cur, 0], axis=0, keepdims=True)
    vmax = jnp.max(mm_ref[cur, 1], axis=0, keepdims=True)

    def count(pred_fn):
        def body(j, c):
            s0 = block_start(j)
            hit = pred_fn(sc_ref[cur, pl.ds(s0, ts), :], s0).reshape(ts // SUBLANES, SUBLANES, tq)
            c = list(c)
            for r in range(ts // SUBLANES):
                c[r % 4] = jnp.where(hit[r], c[r % 4] + 1, c[r % 4])
            return tuple(c)
        z = jnp.zeros((SUBLANES, tq), I32)
        c8 = lax.fori_loop(0, n_sel, body, (z, z, z, z))
        return jnp.sum(c8[0] + c8[1] + c8[2] + c8[3], axis=0, keepdims=True)

    def midpoint(lo, hi):
        return 0.5 * lo + 0.5 * hi

    def searching(lo, hi, cnt_lo):
        mid = midpoint(lo, hi)
        return (cnt_lo > topk) & (mid > lo) & (mid < hi)

    def status(lo, hi, cnt_lo):
        flags = 2.0 * searching(lo, hi, cnt_lo).astype(F32) + (cnt_lo > topk).astype(F32)
        return jnp.max(flags)

    n_causal = t_idx[0:1, :] + 1
    lo0 = jnp.where(n_causal > topk, vmin, -jnp.inf)
    hi0 = vmax + (jnp.abs(vmax) * 2.0 ** -20 + 1e-30)

    def bisect_cond(carry):
        it, _, _, _, state = carry
        return (state >= 2) & (it < BISECT_CAP)

    def bisect_step(_, carry):
        lo, hi, cnt_lo = carry
        go = searching(lo, hi, cnt_lo)
        mid = midpoint(lo, hi)
        cnt = count(lambda blk, s0: blk >= mid)
        take = go & (cnt >= topk)
        return (jnp.where(take, mid, lo), jnp.where(go & (cnt < topk), mid, hi),
                jnp.where(take, cnt, cnt_lo))

    def bisect_body(carry):
        it, lo, hi, cnt_lo, _ = carry
        lo, hi, cnt_lo = lax.fori_loop(0, BISECT_GROUP, bisect_step, (lo, hi, cnt_lo))
        return it + BISECT_GROUP, lo, hi, cnt_lo, status(lo, hi, cnt_lo)

    def interp_step(_, carry):
        lo, hi, cnt_lo, cnt_hi = carry
        frac = (cnt_lo - topk).astype(F32) + 0.5
        frac = frac / jnp.maximum(cnt_lo - cnt_hi, 1).astype(F32)
        frac = jnp.clip(frac, INTERP_MARGIN, 1.0 - INTERP_MARGIN)
        base = jnp.maximum(lo, vmin)
        probe = base + (hi - base) * frac
        go = (cnt_lo > topk) & (probe > lo) & (probe < hi)
        cnt = count(lambda blk, s0: blk >= probe)
        take = go & (cnt >= topk)
        drop = go & (cnt < topk)
        return (jnp.where(take, probe, lo), jnp.where(drop, probe, hi),
                jnp.where(take, cnt, cnt_lo), jnp.where(drop, cnt, cnt_hi))

    n_interp = jnp.where(all_selected, 0, BISECT_INTERP)
    n_blind = jnp.where(all_selected, 0, BISECT_BLIND - BISECT_INTERP)
    lo, hi, cnt_lo, _ = lax.fori_loop(0, n_interp, interp_step,
                                      (lo0, hi0, n_causal, jnp.zeros_like(n_causal)))
    lo, hi, cnt_lo = lax.fori_loop(0, n_blind, bisect_step, (lo, hi, cnt_lo))
    _, thr, hi, cnt_thr, state = lax.while_loop(
        bisect_cond, bisect_body,
        (jnp.int32(BISECT_BLIND), lo, hi, cnt_lo,
         jnp.where(all_selected, 0, status(lo, hi, cnt_lo))))

    tie = state >= 1

    @pl.when(jnp.logical_not(tie))
    def _():
        def body(j, _):
            s0 = block_start(j)
            sel = (sc_ref[cur, pl.ds(s0, ts), :] >= thr) & (s0 + s_rel <= t_idx)
            madd_ref[pl.ds(s0, ts), :] = jnp.where(sel, 0.0, NEG_BIG)
            return 0

        lax.fori_loop(0, n_sel, body, 0)

    @pl.when(tie)
    def _():
        need = topk - count(lambda blk, s0: blk > thr)
        n_bits = sc_ref.shape[1].bit_length()

        def jbody(i, bound):
            cand = bound + jnp.left_shift(jnp.int32(1), n_bits - 1 - i)
            cnt = count(lambda blk, s0: (blk == thr) & (s0 + s_rel < cand))
            return jnp.where(cnt <= need, cand, bound)

        bound = lax.fori_loop(0, n_bits, jbody, jnp.zeros((1, tq), I32))

        def body(j, _):
            s0 = block_start(j)
            blk = sc_ref[cur, pl.ds(s0, ts), :]
            sel = ((blk > thr) | ((blk == thr) & (s0 + s_rel < bound))) & (s0 + s_rel <= t_idx)
            madd_ref[pl.ds(s0, ts), :] = jnp.where(sel, 0.0, NEG_BIG)
            return 0

        lax.fori_loop(0, n_sel, body, 0)

    m_ref[...] = jnp.full(m_ref.shape, NEG_BIG, F32)
    l_ref[...] = jnp.zeros(l_ref.shape, F32)
    acc_ref[...] = jnp.zeros(acc_ref.shape, F32)

    def attend(j):
        s0 = block_start(j)
        madd = madd_ref[pl.ds(s0, ts), :]
        near = jnp.minimum(q_blk - j, 2)
        for h in range(n_heads):
            lg = jnp.dot(ka_ref[0, pl.ds(s0, ts), _pair(h)], qam_ref[0, h],
                         preferred_element_type=F32)
            lg = lg + tab_ref[near, h] + madd
            lg_ref[h] = lg
            bmax_ref[h] = jnp.max(lg, axis=0, keepdims=True)
        for h in range(n_heads):
            rows = slice(h * HEAD_DIM, (h + 1) * HEAD_DIM)
            m_old = m_ref[h]
            m_new = jnp.maximum(m_old, bmax_ref[h])
            alpha = jnp.exp2(m_old - m_new)
            p = jnp.exp2(lg_ref[h] - m_new)
            l_ref[h] = alpha * l_ref[h] + jnp.sum(p, axis=0, keepdims=True)
            pv = jnp.dot(vat_ref[0, j, rows, :], p.astype(BF16), preferred_element_type=F32)
            acc_ref[rows, :] = alpha * acc_ref[rows, :] + pv
            m_ref[h] = m_new

    def index_next(j, carry, diagonal):
        vmin8, vmax8 = carry
        s0 = block_start(j)
        kblk = ki_ref[0, pl.ds(s0, ts), :]
        score = jnp.zeros((ts, tq), F32)
        for h in range(IDX_HEADS):
            dots = jnp.dot(kblk, qin_ref[0, h], preferred_element_type=F32)
            score = score + jnp.maximum(dots, 0.0) * witn_ref[0, h:h + 1, :]
        if diagonal:
            future = s_rel > t_rel
            low = jnp.where(future, jnp.inf, score)
            score = jnp.where(future, -jnp.inf, score)
        else:
            low = score
        sc_ref[nxt, pl.ds(s0, ts), :] = score
        groups = (ts // SUBLANES, SUBLANES, tq)
        return (jnp.minimum(vmin8, low.reshape(groups).min(axis=0)),
                jnp.maximum(vmax8, score.reshape(groups).max(axis=0)))

    def write_output():
        for h in range(n_heads):
            rows = slice(h * HEAD_DIM, (h + 1) * HEAD_DIM)
            acc_ref[rows, :] = acc_ref[rows, :] * (1.0 / l_ref[h])
        o_ref[0] = (acc_ref[...].T * ga_ref[0]).astype(BF16)

    has_next = q_blk + 1 < pl.num_programs(1)

    @pl.when(has_next)
    def _():
        def pair(i, carry):
            for j in (2 * i, 2 * i + 1):
                attend(j)
                carry = index_next(j, carry, diagonal=False)
            return carry

        def single(j, carry):
            attend(j)
            return index_next(j, carry, diagonal=False)

        carry = lax.fori_loop(0, nblk // 2, pair, (jnp.full((SUBLANES, tq), jnp.inf, F32),
                                                   jnp.full((SUBLANES, tq), -jnp.inf, F32)))
        carry = lax.fori_loop(2 * (nblk // 2), nblk, single, carry)
        vmin8, vmax8 = index_next(nblk, carry, diagonal=True)
        mm_ref[nxt, 0] = vmin8
        mm_ref[nxt, 1] = vmax8
        write_output()

    @pl.when(jnp.logical_not(has_next))
    def _():
        def body(j, _):
            attend(j)
            return 0

        lax.fori_loop(0, nblk, body, 0)
        write_output()


def _dsa(ki, qim, wit, ka, vat, qam, ga, tab, topk):
    b, l, d_a = ka.shape
    n_heads = d_a // HEAD_DIM
    tq = ATT_TILE
    nq = l // tq
    assert tq <= topk, "the first query tile must not need indexer scores"
    next_tile = lambda j: jnp.minimum(j + 1, nq - 1)
    kernel = functools.partial(_dsa_kernel, topk=topk, n_heads=n_heads)
    return pl.pallas_call(
        kernel,
        grid=(b, nq),
        in_specs=[
            pl.BlockSpec((1, l, LANES), lambda i, j: (i, 0, 0)),
            pl.BlockSpec((1, IDX_HEADS, LANES, tq), lambda i, j: (i, 0, 0, next_tile(j))),
            pl.BlockSpec((1, IDX_HEADS, tq), lambda i, j: (i, 0, next_tile(j))),
            pl.BlockSpec((1, l, d_a), lambda i, j: (i, 0, 0)),
            pl.BlockSpec((1, nq, d_a, tq), lambda i, j: (i, 0, 0, 0)),
            pl.BlockSpec((1, n_heads, LANES, tq), lambda i, j: (i, 0, 0, j)),
            pl.BlockSpec((1, tq, d_a), lambda i, j: (i, j, 0)),
            pl.BlockSpec(tab.shape, lambda i, j: (0, 0, 0, 0), pipeline_mode=pl.Buffered(1)),
        ],
        out_specs=pl.BlockSpec((1, tq, d_a), lambda i, j: (i, j, 0)),
        out_shape=jax.ShapeDtypeStruct((b, l, d_a), BF16),
        scratch_shapes=[
            pltpu.VMEM((2, l, tq), F32),
            pltpu.VMEM((2, 2, SUBLANES, tq), F32),
            pltpu.VMEM((l, tq), F32),
            pltpu.VMEM((n_heads, 1, tq), F32),
            pltpu.VMEM((n_heads, 1, tq), F32),
            pltpu.VMEM((d_a, tq), F32),
            pltpu.VMEM((n_heads, tq, tq), F32),
            pltpu.VMEM((n_heads, 1, tq), F32),
        ],
        compiler_params=_params("arbitrary", "arbitrary"),
        name="dsa",
    )(ki, qim, wit, ka, vat, qam, ga, tab)


def _stick_kernel(kb_ref, vbt_ref, qbm_ref, gb_ref, u_ref, x_ref, oa_ref, wa_ref, wb_ref, y_ref,
                  below_ref, acc_ref, z_ref, lb_ref, top_ref, *, n_heads):
    tq = y_ref.shape[1]
    ts = tq
    q_blk = pl.program_id(1)
    s_rel = lax.broadcasted_iota(I32, (ts, tq), 0)
    t_rel = lax.broadcasted_iota(I32, (ts, tq), 1)

    below_ref[...] = jnp.zeros(below_ref.shape, F32)
    acc_ref[...] = jnp.zeros(acc_ref.shape, F32)

    def block(j, diagonal):
        s0 = pl.multiple_of(j * ts, ts)
        strict = s_rel < t_rel

        def z_dot(h):
            return jnp.dot(kb_ref[0, pl.ds(s0, ts), _pair(h)], qbm_ref[0, h],
                           preferred_element_type=F32)

        def terms(z2):
            log_beta = jnp.minimum(z2, 0.0) - jnp.log2(1.0 + jnp.exp2(-jnp.abs(z2)))
            log_om = log_beta - z2
            if diagonal:
                log_om = jnp.where(strict, log_om, 0.0)
            suffix = jnp.dot(u_ref[...], log_om.astype(BF16), preferred_element_type=F32)
            return log_beta, log_om[0:1, :], suffix

        def finish(h, log_a, below):
            rows = slice(h * HEAD_DIM, (h + 1) * HEAD_DIM)
            a = jnp.exp2(log_a)
            if diagonal:
                a = jnp.where(strict, a, 0.0)
            pv = jnp.dot(vbt_ref[0, j, rows, :], a.astype(BF16), preferred_element_type=F32)
            acc_ref[rows, :] += pv * jnp.exp2(below)

        for h in range(n_heads):
            z_ref[h] = z_dot(h)
        below, worst = [], None
        for h in range(n_heads):
            log_beta, om_row, suffix = terms(z_ref[h])
            lb_ref[h] = log_beta + suffix
            below.append(below_ref[h])
            after = below[h] + om_row + suffix[0:1, :]
            below_ref[h] = after
            worst = after if worst is None else jnp.maximum(worst, after)
        top = jnp.max(worst)
        for h in range(n_heads):
            finish(h, lb_ref[h], below[h])
        return top

    @pl.when(q_blk == 0)
    def _():
        top_ref[0] = block(q_blk, diagonal=True)

    @pl.when(q_blk > 0)
    def _():
        block(q_blk, diagonal=True)
        top_ref[0] = block(q_blk - 1, diagonal=False)

    def cond(carry):
        i, top = carry
        return (i <= q_blk) & (top >= EXP2_UNDERFLOW)

    def body(carry):
        i, _ = carry
        return i + 1, block(q_blk - i, diagonal=False)

    lax.while_loop(cond, body, (jnp.int32(2), top_ref[0]))

    ob = (acc_ref[...].T * gb_ref[0]).astype(BF16)
    y_ref[0] = (x_ref[0]
                + jnp.dot(oa_ref[0], wa_ref[...], preferred_element_type=F32)
                + jnp.dot(ob, wb_ref[...], preferred_element_type=F32))


def _stick(kb, vbt, qbm, gb, u2, x, oa, wa, wb):
    b, l, d_b = kb.shape
    d = x.shape[2]
    n_heads = d_b // HEAD_DIM
    tq = ATT_TILE
    nq = l // tq
    return pl.pallas_call(
        functools.partial(_stick_kernel, n_heads=n_heads),
        grid=(b, nq),
        in_specs=[
            pl.BlockSpec((1, l, d_b), lambda i, j: (i, 0, 0)),
            pl.BlockSpec((1, nq, d_b, tq), lambda i, j: (i, 0, 0, 0)),
            pl.BlockSpec((1, n_heads, LANES, tq), lambda i, j: (i, 0, 0, j)),
            pl.BlockSpec((1, tq, d_b), lambda i, j: (i, j, 0)),
            pl.BlockSpec(u2.shape, lambda i, j: (0, 0)),
            pl.BlockSpec((1, tq, d), lambda i, j: (i, j, 0)),
            pl.BlockSpec((1, tq, oa.shape[2]), lambda i, j: (i, j, 0)),
            pl.BlockSpec(wa.shape, lambda i, j: (0, 0)),
            pl.BlockSpec(wb.shape, lambda i, j: (0, 0)),
        ],
        out_specs=pl.BlockSpec((1, tq, d), lambda i, j: (i, j, 0)),
        out_shape=jax.ShapeDtypeStruct((b, l, d), F32),
        scratch_shapes=[
            pltpu.VMEM((n_heads, 1, tq), F32),
            pltpu.VMEM((d_b, tq), F32),
            pltpu.VMEM((n_heads, tq, tq), F32),
            pltpu.VMEM((n_heads, tq, tq), F32),
            pltpu.SMEM((1,), F32),
        ],
        compiler_params=_params("arbitrary", "arbitrary"),
        name="stick",
    )(kb, vbt, qbm, gb, u2, x, oa, wa, wb)


def kernel(x, norm_gain, w_in, q_norm_gain, k_norm_gain, rel_bias, w_out):
    b, l, d = x.shape
    depth = w_in.shape[0]
    d_a = d // 2
    d_b = d - d_a
    h_a = d_a // HEAD_DIM
    d_qi = IDX_HEADS * IDX_DIM
    topk = min(TOPK_MAX, l // 4)
    ts = ATT_TILE
    scale = HEAD_DIM ** -0.5 * LOG2E
    idx_scale = (IDX_HEADS * IDX_DIM) ** -0.5

    lane = jnp.arange(LANES)
    gsum = (lane[:, None] // HEAD_DIM == lane[None, :] // HEAD_DIM).astype(BF16)
    u2 = (jnp.arange(ts)[None, :] > jnp.arange(ts)[:, None]).astype(BF16)
    tab = _t5_table(rel_bias.astype(F32), ts)

    for layer in range(depth):
        w_all = _pack_w_in(jnp.swapaxes(w_in[layer], 0, 1), d_a, d_b, d_qi, scale, idx_scale)
        qg = jnp.tile(q_norm_gain[layer] * scale, h_a)[None, :]
        kg = jnp.tile(k_norm_gain[layer], h_a)[None, :]

        (qam, ka, vat, ga, qim, ki, wit, qbm, kb, vbt, gb) = _inproj(
            x, norm_gain[layer][None, :], w_all, qg, kg, gsum, d_a, d_b, d_qi)
        oa = _dsa(ki, qim, wit, ka, vat, qam, ga, tab, topk)
        w_o = w_out[layer].astype(BF16)
        x = _stick(kb, vbt, qbm, gb, u2, x, oa, w_o[:d_a], w_o[d_a:])
    return x
```

```python
import functools
import math

import jax
import jax.numpy as jnp
import numpy as np
from jax import lax
from jax.experimental import pallas as pl
from jax.experimental.pallas import tpu as pltpu

F32 = jnp.float32
BF16 = jnp.bfloat16
I32 = jnp.int32

HEAD_DIM = 64
IDX_HEADS = 16
IDX_DIM = 64
TOPK_MAX = 256
NUM_BUCKETS = 32
MAX_DISTANCE = 128
RMS_EPS = 1e-6

LANES = 128
SUBLANES = 8
NEG_BIG = -1e30
EXP2_UNDERFLOW = -150.0
LOG2E = math.log2(math.e)
BISECT_CAP = 300
BISECT_BLIND = 16
BISECT_GROUP = 2
BISECT_INTERP = 8
INTERP_MARGIN = 0.05

VMEM_LIMIT_BYTES = 56 * 1024 * 1024
TOKEN_TILE = 512
ATT_TILE = 256


def _params(*sem):
    return pltpu.CompilerParams(dimension_semantics=sem, vmem_limit_bytes=VMEM_LIMIT_BYTES)


def _pair(h):
    return slice((h // 2) * LANES, (h // 2 + 1) * LANES)


def _store_head_masked(dst_ref, src):
    rows = src.shape[0]
    low = lax.broadcasted_iota(I32, (rows, LANES), 1) < HEAD_DIM
    for h in range(dst_ref.shape[1]):
        keep = low if h % 2 == 0 else jnp.logical_not(low)
        dst_ref[0, h] = jnp.where(keep, src[:, _pair(h)], 0.0).T.astype(BF16)


def _pack_w_in_kernel(wt_ref, o_ref, *, d_a, d_b, d_qi, scale, idx_scale):
    o_ki = 4 * d_a + d_qi
    o_wi = o_ki + IDX_DIM
    o_b = o_wi + IDX_HEADS
    d = wt_ref.shape[1]

    def put(dst, rows):
        o_ref[:, dst:dst + LANES] = rows.T.astype(BF16)

    for c in range(0, o_ki, LANES):
        put(c, wt_ref[c:c + LANES, :])
    k_idx = wt_ref[o_ki:o_wi, :]
    put(o_ki, jnp.concatenate([k_idx, k_idx], axis=0))
    w_idx = wt_ref[o_wi:o_b, :] * idx_scale
    put(o_ki + LANES, jnp.concatenate([w_idx, jnp.zeros((LANES - IDX_HEADS, d), F32)], axis=0))
    for c in range(0, 4 * d_b, LANES):
        rows = wt_ref[o_b + c:o_b + c + LANES, :]
        put(o_ki + 2 * LANES + c, rows * scale if c < d_b else rows)


def _pack_w_in(w_t, d_a, d_b, d_qi, scale, idx_scale):
    p, d = w_t.shape
    width = p + 2 * LANES - IDX_DIM - IDX_HEADS
    return pl.pallas_call(
        functools.partial(_pack_w_in_kernel, d_a=d_a, d_b=d_b, d_qi=d_qi, scale=scale,
                          idx_scale=idx_scale),
        out_shape=jax.ShapeDtypeStruct((d, width), BF16),
        compiler_params=pltpu.CompilerParams(vmem_limit_bytes=VMEM_LIMIT_BYTES),
        name="pack_w_in",
    )(w_t)


def _inproj_kernel(x_ref, gain_ref, w_ref, qg_ref, kg_ref, gsum_ref,
                   qa_ref, ka_ref, vat_ref, ga_ref, qi_ref, ki_ref, wit_ref,
                   qb_ref, kb_ref, vbt_ref, gb_ref, *, d_a, d_b, d_qi):
    x = x_ref[0]
    ms = jnp.mean(x * x, axis=-1, keepdims=True)
    h = (x * lax.rsqrt(ms + RMS_EPS) * gain_ref[...]).astype(BF16)
    ts = vat_ref.shape[3]

    def proj(c0, width):
        return jnp.dot(h, w_ref[:, c0:c0 + width], preferred_element_type=F32)

    def head_norm(y, g):
        sq = (y * y).astype(BF16)
        ones = gsum_ref[...]
        ssum = jnp.concatenate(
            [jnp.dot(sq[:, c:c + LANES], ones, preferred_element_type=F32)
             for c in range(0, y.shape[1], LANES)], axis=1)
        return y * lax.rsqrt(ssum * (1.0 / HEAD_DIM) + RMS_EPS) * g

    def silu(g):
        return g * (1.0 / (1.0 + jnp.exp(-g)))

    def store_key_blocks_t(dst_ref, v):
        vt = v.T.astype(BF16)
        for c in range(dst_ref.shape[1]):
            dst_ref[0, c] = vt[:, c * ts:(c + 1) * ts]

    c = 0
    _store_head_masked(qa_ref, head_norm(proj(c, d_a), qg_ref[...])); c += d_a
    ka_ref[0] = head_norm(proj(c, d_a), kg_ref[...]).astype(BF16); c += d_a
    store_key_blocks_t(vat_ref, proj(c, d_a)); c += d_a
    ga_ref[0] = silu(proj(c, d_a)); c += d_a
    _store_head_masked(qi_ref, proj(c, d_qi)); c += d_qi
    ki_ref[0] = proj(c, LANES).astype(BF16); c += LANES
    wit_ref[0] = proj(c, LANES).T[:IDX_HEADS, :]; c += LANES
    _store_head_masked(qb_ref, proj(c, d_b))
    store_key_blocks_t(vbt_ref, proj(c + 2 * d_b, d_b))
    gb_ref[0] = silu(proj(c + 3 * d_b, d_b))
    kb_ref[0] = proj(c + d_b, d_b).astype(BF16)


def _inproj(x, gain, w_all, qg, kg, gsum, d_a, d_b, d_qi):
    b, l, d = x.shape
    tm = TOKEN_TILE
    ts = ATT_TILE
    nb = l // ts
    row = lambda width: pl.BlockSpec((1, tm, width), lambda i, j: (i, j, 0))
    const = lambda shape: pl.BlockSpec(shape, lambda i, j: (0, 0))
    vt_spec = lambda ch: pl.BlockSpec((1, tm // ts, ch, ts), lambda i, j: (i, j, 0, 0))
    tok = lambda width, dt: jax.ShapeDtypeStruct((b, l, width), dt)
    heads = lambda width: jax.ShapeDtypeStruct((b, width // HEAD_DIM, LANES, l), BF16)
    heads_spec = lambda width: pl.BlockSpec((1, width // HEAD_DIM, LANES, tm), lambda i, j: (i, 0, 0, j))
    out_shapes = [
        heads(d_a), tok(d_a, BF16), jax.ShapeDtypeStruct((b, nb, d_a, ts), BF16), tok(d_a, F32),
        heads(d_qi), tok(LANES, BF16), jax.ShapeDtypeStruct((b, IDX_HEADS, l), F32),
        heads(d_b), tok(d_b, BF16), jax.ShapeDtypeStruct((b, nb, d_b, ts), BF16), tok(d_b, F32),
    ]
    out_specs = [heads_spec(d_a), row(d_a), vt_spec(d_a), row(d_a), heads_spec(d_qi), row(LANES),
                 pl.BlockSpec((1, IDX_HEADS, tm), lambda i, j: (i, 0, j)),
                 heads_spec(d_b), row(d_b), vt_spec(d_b), row(d_b)]
    return pl.pallas_call(
        functools.partial(_inproj_kernel, d_a=d_a, d_b=d_b, d_qi=d_qi),
        grid=(b, l // tm),
        in_specs=[row(d), const(gain.shape),
                  pl.BlockSpec(w_all.shape, lambda i, j: (0, 0), pipeline_mode=pl.Buffered(1)),
                  const(qg.shape), const(kg.shape), const(gsum.shape)],
        out_specs=out_specs,
        out_shape=out_shapes,
        compiler_params=_params("arbitrary", "arbitrary"),
        name="inproj",
    )(x, gain, w_all, qg, kg, gsum)


def _t5_large_thresholds():
    max_exact = NUM_BUCKETS // 2
    d = np.arange(max_exact, 2 * MAX_DISTANCE + 1)
    large = max_exact + (np.log(d.astype(np.float32) / np.float32(max_exact))
                         / np.float32(math.log(MAX_DISTANCE / max_exact))
                         * np.float32(NUM_BUCKETS - max_exact)).astype(np.int32)
    large = np.minimum(large, NUM_BUCKETS - 1)
    return [int(d[np.argmax(large >= k)]) for k in range(max_exact + 1, NUM_BUCKETS)]


def _t5_bucket(dist):
    max_exact = NUM_BUCKETS // 2
    d = jnp.maximum(dist, 0)
    large = jnp.full(d.shape, max_exact, I32)
    for first in _t5_large_thresholds():
        large = large + (d >= first).astype(I32)
    return jnp.where(d < max_exact, d, large)


def _t5_table_kernel(rb_ref, tab_ref, *, n_heads):
    ts = tab_ref.shape[2]
    rows = 16
    far = _t5_large_thresholds()[-1]
    t_rel = lax.broadcasted_iota(I32, (rows, ts), 1)
    s_rel = lax.broadcasted_iota(I32, (rows, ts), 0)

    for i in range(3):
        n_far = max(0, min(ts, i * ts - far + 1)) // rows

        def fill(c, _, i=i):
            r0 = pl.multiple_of(c * rows, rows)
            for h in range(n_heads):
                tab_ref[i, h, pl.ds(r0, rows), :] = jnp.full((rows, ts), rb_ref[NUM_BUCKETS - 1, h] * LOG2E, F32)
            return 0

        def compute(c, _, i=i):
            r0 = pl.multiple_of(c * rows, rows)
            bucket = _t5_bucket(t_rel - (r0 + s_rel) + i * ts)
            acc = [jnp.zeros((rows, ts), F32) for _ in range(n_heads)]
            for j in range(NUM_BUCKETS):
                hit = bucket == j
                acc = [jnp.where(hit, rb_ref[j, h] * LOG2E, acc[h]) for h in range(n_heads)]
            for h in range(n_heads):
                tab_ref[i, h, pl.ds(r0, rows), :] = acc[h]
            return 0

        lax.fori_loop(0, n_far, fill, 0)
        lax.fori_loop(n_far, ts // rows, compute, 0)


def _t5_table(rel_bias, ts):
    n_heads = rel_bias.shape[1]
    return pl.pallas_call(
        functools.partial(_t5_table_kernel, n_heads=n_heads),
        in_specs=[pl.BlockSpec(memory_space=pltpu.SMEM)],
        out_specs=pl.BlockSpec(memory_space=pltpu.VMEM),
        out_shape=jax.ShapeDtypeStruct((3, n_heads, ts, ts), F32),
        compiler_params=pltpu.CompilerParams(vmem_limit_bytes=VMEM_LIMIT_BYTES),
        name="t5_table",
    )(rel_bias)


def _dsa_kernel(ki_ref, qin_ref, witn_ref, ka_ref, vat_ref, qam_ref, ga_ref, tab_ref, o_ref,
                sc_ref, mm_ref, madd_ref, m_ref, l_ref, acc_ref, lg_ref, bmax_ref, *, topk, n_heads):
    tq = o_ref.shape[1]
    ts = tq
    q_blk = pl.program_id(1)
    nblk = q_blk + 1
    cur = q_blk % 2
    nxt = 1 - cur
    s_rel = lax.broadcasted_iota(I32, (ts, tq), 0)
    t_rel = lax.broadcasted_iota(I32, (ts, tq), 1)
    t_idx = q_blk * tq + t_rel

    def block_start(j):
        return pl.multiple_of(j * ts, ts)

    all_selected = nblk * tq <= topk
    n_sel = jnp.where(all_selected, 0, nblk)

    @pl.when(all_selected)
    def _():
        mm_ref[cur] = jnp.zeros(mm_ref.shape[1:], F32)

        def body(j, _):
            s0 = block_start(j)
            madd_ref[pl.ds(s0, ts), :] = jnp.where(s0 + s_rel > t_idx, NEG_BIG, 0.0)
            return 0

        lax.fori_loop(0, nblk, body, 0)

    vmin = jnp.min(mm_ref[cur, 0], axis=0, keepdims=True)
    vmax = jnp.max(mm_ref[cur, 1], axis=0, keepdims=True)

    def count(pred_fn):
        def body(j, c):
            s0 = block_start(j)
            hit = pred_fn(sc_ref[cur, pl.ds(s0, ts), :], s0).reshape(ts // SUBLANES, SUBLANES, tq)
            c = list(c)
            for r in range(ts // SUBLANES):
                c[r % 4] = jnp.where(hit[r], c[r % 4] + 1, c[r % 4])
            return tuple(c)
        z = jnp.zeros((SUBLANES, tq), I32)
        c8 = lax.fori_loop(0, n_sel, body, (z, z, z, z))
        return jnp.sum(c8[0] + c8[1] + c8[2] + c8[3], axis=0, keepdims=True)

    def midpoint(lo, hi):
        return 0.5 * lo + 0.5 * hi

    def searching(lo, hi, cnt_lo):
        mid = midpoint(lo, hi)
        return (cnt_lo > topk) & (mid > lo) & (mid < hi)

    def status(lo, hi, cnt_lo):
        flags = 2.0 * searching(lo, hi, cnt_lo).astype(F32) + (cnt_lo > topk).astype(F32)
        return jnp.max(flags)

    n_causal = t_idx[0:1, :] + 1
    lo0 = jnp.where(n_causal > topk, vmin, -jnp.inf)
    hi0 = vmax + (jnp.abs(vmax) * 2.0 ** -20 + 1e-30)

    def bisect_cond(carry):
        it, _, _, _, state = carry
        return (state >= 2) & (it < BISECT_CAP)

    def bisect_step(_, carry):
        lo, hi, cnt_lo = carry
        go = searching(lo, hi, cnt_lo)
        mid = midpoint(lo, hi)
        cnt = count(lambda blk, s0: blk >= mid)
        take = go & (cnt >= topk)
        return (jnp.where(take, mid, lo), jnp.where(go & (cnt < topk), mid, hi),
                jnp.where(take, cnt, cnt_lo))

    def bisect_body(carry):
        it, lo, hi, cnt_lo, _ = carry
        lo, hi, cnt_lo = lax.fori_loop(0, BISECT_GROUP, bisect_step, (lo, hi, cnt_lo))
        return it + BISECT_GROUP, lo, hi, cnt_lo, status(lo, hi, cnt_lo)

    def interp_step(_, carry):
        lo, hi, cnt_lo, cnt_hi = carry
        frac = (cnt_lo - topk).astype(F32) + 0.5
        frac = frac / jnp.maximum(cnt_lo - cnt_hi, 1).astype(F32)
        frac = jnp.clip(frac, INTERP_MARGIN, 1.0 - INTERP_MARGIN)
        base = jnp.maximum(lo, vmin)
        probe = base + (hi - base) * frac
        go = (cnt_lo > topk) & (probe > lo) & (probe < hi)
        cnt = count(lambda blk, s0: blk >= probe)
        take = go & (cnt >= topk)
        drop = go & (cnt < topk)
        return (jnp.where(take, probe, lo), jnp.where(drop, probe, hi),
                jnp.where(take, cnt, cnt_lo), jnp.where(drop, cnt, cnt_hi))

    n_interp = jnp.where(all_selected, 0, BISECT_INTERP)
    n_blind = jnp.where(all_selected, 0, BISECT_BLIND - BISECT_INTERP)
    lo, hi, cnt_lo, _ = lax.fori_loop(0, n_interp, interp_step,
                                      (lo0, hi0, n_causal, jnp.zeros_like(n_causal)))
    lo, hi, cnt_lo = lax.fori_loop(0, n_blind, bisect_step, (lo, hi, cnt_lo))
    _, thr, hi, cnt_thr, state = lax.while_loop(
        bisect_cond, bisect_body,
        (jnp.int32(BISECT_BLIND), lo, hi, cnt_lo,
         jnp.where(all_selected, 0, status(lo, hi, cnt_lo))))

    tie = state >= 1

    @pl.when(jnp.logical_not(tie))
    def _():
        def body(j, _):
            s0 = block_start(j)
            sel = (sc_ref[cur, pl.ds(s0, ts), :] >= thr) & (s0 + s_rel <= t_idx)
            madd_ref[pl.ds(s0, ts), :] = jnp.where(sel, 0.0, NEG_BIG)
            return 0

        lax.fori_loop(0, n_sel, body, 0)

    @pl.when(tie)
    def _():
        need = topk - count(lambda blk, s0: blk > thr)
        n_bits = sc_ref.shape[1].bit_length()

        def jbody(i, bound):
            cand = bound + jnp.left_shift(jnp.int32(1), n_bits - 1 - i)
            cnt = count(lambda blk, s0: (blk == thr) & (s0 + s_rel < cand))
            return jnp.where(cnt <= need, cand, bound)

        bound = lax.fori_loop(0, n_bits, jbody, jnp.zeros((1, tq), I32))

        def body(j, _):
            s0 = block_start(j)
            blk = sc_ref[cur, pl.ds(s0, ts), :]
            sel = ((blk > thr) | ((blk == thr) & (s0 + s_rel < bound))) & (s0 + s_rel <= t_idx)
            madd_ref[pl.ds(s0, ts), :] = jnp.where(sel, 0.0, NEG_BIG)
            return 0

        lax.fori_loop(0, n_sel, body, 0)

    def reset_accumulators():
        m_ref[...] = jnp.full(m_ref.shape, NEG_BIG, F32)
        l_ref[...] = jnp.zeros(l_ref.shape, F32)
        acc_ref[...] = jnp.zeros(acc_ref.shape, F32)

    @pl.when((pl.program_id(0) == 0) & (q_blk == 0))
    def _():
        reset_accumulators()

    def attend(j):
        s0 = block_start(j)
        madd = madd_ref[pl.ds(s0, ts), :]
        near = jnp.minimum(q_blk - j, 2)
        for h in range(n_heads):
            lg = jnp.dot(ka_ref[0, pl.ds(s0, ts), _pair(h)], qam_ref[0, h],
                         preferred_element_type=F32)
            lg = lg + tab_ref[near, h] + madd
            lg_ref[h] = lg
            bmax_ref[h] = jnp.max(lg, axis=0, keepdims=True)
        for h in range(n_heads):
            rows = slice(h * HEAD_DIM, (h + 1) * HEAD_DIM)
            m_old = m_ref[h]
            m_new = jnp.maximum(m_old, bmax_ref[h])
            alpha = jnp.exp2(m_old - m_new)
            p = jnp.exp2(lg_ref[h] - m_new)
            l_ref[h] = alpha * l_ref[h] + jnp.sum(p, axis=0, keepdims=True)
            pv = jnp.dot(vat_ref[0, j, rows, :], p.astype(BF16), preferred_element_type=F32)
            acc_ref[rows, :] = alpha * acc_ref[rows, :] + pv
            m_ref[h] = m_new

    def index_next(j, carry, diagonal):
        vmin8, vmax8 = carry
        s0 = block_start(j)
        kblk = ki_ref[0, pl.ds(s0, ts), :]
        score = jnp.zeros((ts, tq), F32)
        for h in range(IDX_HEADS):
            dots = jnp.dot(kblk, qin_ref[0, h], preferred_element_type=F32)
            score = score + jnp.maximum(dots, 0.0) * witn_ref[0, h:h + 1, :]
        if diagonal:
            future = s_rel > t_rel
            low = jnp.where(future, jnp.inf, score)
            score = jnp.where(future, -jnp.inf, score)
        else:
            low = score
        sc_ref[nxt, pl.ds(s0, ts), :] = score
        groups = (ts // SUBLANES, SUBLANES, tq)
        return (jnp.minimum(vmin8, low.reshape(groups).min(axis=0)),
                jnp.maximum(vmax8, score.reshape(groups).max(axis=0)))

    def write_output():
        for h in range(n_heads):
            rows = slice(h * HEAD_DIM, (h + 1) * HEAD_DIM)
            acc_ref[rows, :] = acc_ref[rows, :] * (1.0 / l_ref[h])
        o_ref[0] = (acc_ref[...].T * ga_ref[0]).astype(BF16)
        reset_accumulators()

    has_next = q_blk + 1 < pl.num_programs(1)

    @pl.when(has_next)
    def _():
        def pair(i, carry):
            for j in (2 * i, 2 * i + 1):
                attend(j)
                carry = index_next(j, carry, diagonal=False)
            return carry

        def single(j, carry):
            attend(j)
            return index_next(j, carry, diagonal=False)

        def quad(i, carry):
            for j in (4 * i, 4 * i + 1, 4 * i + 2, 4 * i + 3):
                attend(j)
                carry = index_next(j, carry, diagonal=False)
            return carry

        carry = lax.fori_loop(0, nblk // 4, quad, (jnp.full((SUBLANES, tq), jnp.inf, F32),
                                                   jnp.full((SUBLANES, tq), -jnp.inf, F32)))
        carry = lax.fori_loop(2 * (nblk // 4), nblk // 2, pair, carry)
        carry = lax.fori_loop(2 * (nblk // 2), nblk, single, carry)
        vmin8, vmax8 = index_next(nblk, carry, diagonal=True)
        mm_ref[nxt, 0] = vmin8
        mm_ref[nxt, 1] = vmax8
        write_output()

    @pl.when(jnp.logical_not(has_next))
    def _():
        def body(j, _):
            attend(j)
            return 0

        lax.fori_loop(0, nblk, body, 0)
        write_output()


def _dsa(ki, qim, wit, ka, vat, qam, ga, tab, topk):
    b, l, d_a = ka.shape
    n_heads = d_a // HEAD_DIM
    tq = ATT_TILE
    nq = l // tq
    assert tq <= topk, "the first query tile must not need indexer scores"
    next_tile = lambda j: jnp.minimum(j + 1, nq - 1)
    kernel = functools.partial(_dsa_kernel, topk=topk, n_heads=n_heads)
    return pl.pallas_call(
        kernel,
        grid=(b, nq),
        in_specs=[
            pl.BlockSpec((1, l, LANES), lambda i, j: (i, 0, 0)),
            pl.BlockSpec((1, IDX_HEADS, LANES, tq), lambda i, j: (i, 0, 0, next_tile(j))),
            pl.BlockSpec((1, IDX_HEADS, tq), lambda i, j: (i, 0, next_tile(j))),
            pl.BlockSpec((1, l, d_a), lambda i, j: (i, 0, 0)),
            pl.BlockSpec((1, nq, d_a, tq), lambda i, j: (i, 0, 0, 0)),
            pl.BlockSpec((1, n_heads, LANES, tq), lambda i, j: (i, 0, 0, j)),
            pl.BlockSpec((1, tq, d_a), lambda i, j: (i, j, 0)),
            pl.BlockSpec(tab.shape, lambda i, j: (0, 0, 0, 0), pipeline_mode=pl.Buffered(1)),
        ],
        out_specs=pl.BlockSpec((1, tq, d_a), lambda i, j: (i, j, 0)),
        out_shape=jax.ShapeDtypeStruct((b, l, d_a), BF16),
        scratch_shapes=[
            pltpu.VMEM((2, l, tq), F32),
            pltpu.VMEM((2, 2, SUBLANES, tq), F32),
            pltpu.VMEM((l, tq), F32),
            pltpu.VMEM((n_heads, 1, tq), F32),
            pltpu.VMEM((n_heads, 1, tq), F32),
            pltpu.VMEM((d_a, tq), F32),
            pltpu.VMEM((n_heads, tq, tq), F32),
            pltpu.VMEM((n_heads, 1, tq), F32),
        ],
        compiler_params=_params("arbitrary", "arbitrary"),
        name="dsa",
    )(ki, qim, wit, ka, vat, qam, ga, tab)


def _stick_kernel(kb_ref, vbt_ref, qbm_ref, gb_ref, u_ref, x_ref, oa_ref, wa_ref, wb_ref, y_ref,
                  below_ref, acc_ref, z_ref, lb_ref, top_ref, *, n_heads):
    tq = y_ref.shape[1]
    ts = tq
    q_blk = pl.program_id(1)
    s_rel = lax.broadcasted_iota(I32, (ts, tq), 0)
    t_rel = lax.broadcasted_iota(I32, (ts, tq), 1)

    def reset_accumulators():
        below_ref[...] = jnp.zeros(below_ref.shape, F32)
        acc_ref[...] = jnp.zeros(acc_ref.shape, F32)

    @pl.when((pl.program_id(0) == 0) & (q_blk == 0))
    def _():
        reset_accumulators()

    def block(j, diagonal):
        s0 = pl.multiple_of(j * ts, ts)
        strict = s_rel < t_rel

        def z_dot(h):
            return jnp.dot(kb_ref[0, pl.ds(s0, ts), _pair(h)], qbm_ref[0, h],
                           preferred_element_type=F32)

        def terms(z2):
            log_beta = jnp.minimum(z2, 0.0) - jnp.log2(1.0 + jnp.exp2(-jnp.abs(z2)))
            log_om = log_beta - z2
            if diagonal:
                log_om = jnp.where(strict, log_om, 0.0)
            suffix = jnp.dot(u_ref[...], log_om.astype(BF16), preferred_element_type=F32)
            return log_beta, log_om[0:1, :], suffix

        def finish(h, log_a, below):
            rows = slice(h * HEAD_DIM, (h + 1) * HEAD_DIM)
            a = jnp.exp2(log_a)
            if diagonal:
                a = jnp.where(strict, a, 0.0)
            pv = jnp.dot(vbt_ref[0, j, rows, :], a.astype(BF16), preferred_element_type=F32)
            acc_ref[rows, :] += pv * jnp.exp2(below)

        for h in range(n_heads):
            z_ref[h] = z_dot(h)
        below, worst = [], None
        for h in range(n_heads):
            log_beta, om_row, suffix = terms(z_ref[h])
            lb_ref[h] = log_beta + suffix
            below.append(below_ref[h])
            after = below[h] + om_row + suffix[0:1, :]
            below_ref[h] = after
            worst = after if worst is None else jnp.maximum(worst, after)
        top = jnp.max(worst)
        for h in range(n_heads):
            finish(h, lb_ref[h], below[h])
        return top

    @pl.when(q_blk == 0)
    def _():
        top_ref[0] = block(q_blk, diagonal=True)

    @pl.when(q_blk > 0)
    def _():
        block(q_blk, diagonal=True)
        top_ref[0] = block(q_blk - 1, diagonal=False)

    def cond(carry):
        i, top = carry
        return (i <= q_blk) & (top >= EXP2_UNDERFLOW)

    def body(carry):
        i, _ = carry
        return i + 1, block(q_blk - i, diagonal=False)

    lax.while_loop(cond, body, (jnp.int32(2), top_ref[0]))

    ob = (acc_ref[...].T * gb_ref[0]).astype(BF16)
    reset_accumulators()
    y_ref[0] = (x_ref[0]
                + jnp.dot(oa_ref[0], wa_ref[...], preferred_element_type=F32)
                + jnp.dot(ob, wb_ref[...], preferred_element_type=F32))


def _stick(kb, vbt, qbm, gb, u2, x, oa, wa, wb):
    b, l, d_b = kb.shape
    d = x.shape[2]
    n_heads = d_b // HEAD_DIM
    tq = ATT_TILE
    nq = l // tq
    return pl.pallas_call(
        functools.partial(_stick_kernel, n_heads=n_heads),
        grid=(b, nq),
        in_specs=[
            pl.BlockSpec((1, l, d_b), lambda i, j: (i, 0, 0)),
            pl.BlockSpec((1, nq, d_b, tq), lambda i, j: (i, 0, 0, 0)),
            pl.BlockSpec((1, n_heads, LANES, tq), lambda i, j: (i, 0, 0, j)),
            pl.BlockSpec((1, tq, d_b), lambda i, j: (i, j, 0)),
            pl.BlockSpec(u2.shape, lambda i, j: (0, 0)),
            pl.BlockSpec((1, tq, d), lambda i, j: (i, j, 0)),
            pl.BlockSpec((1, tq, oa.shape[2]), lambda i, j: (i, j, 0)),
            pl.BlockSpec(wa.shape, lambda i, j: (0, 0)),
            pl.BlockSpec(wb.shape, lambda i, j: (0, 0)),
        ],
        out_specs=pl.BlockSpec((1, tq, d), lambda i, j: (i, j, 0)),
        out_shape=jax.ShapeDtypeStruct((b, l, d), F32),
        scratch_shapes=[
            pltpu.VMEM((n_heads, 1, tq), F32),
            pltpu.VMEM((d_b, tq), F32),
            pltpu.VMEM((n_heads, tq, tq), F32),
            pltpu.VMEM((n_heads, tq, tq), F32),
            pltpu.SMEM((1,), F32),
        ],
        compiler_params=_params("arbitrary", "arbitrary"),
        name="stick",
    )(kb, vbt, qbm, gb, u2, x, oa, wa, wb)


def kernel(x, norm_gain, w_in, q_norm_gain, k_norm_gain, rel_bias, w_out):
    b, l, d = x.shape
    depth = w_in.shape[0]
    d_a = d // 2
    d_b = d - d_a
    h_a = d_a // HEAD_DIM
    d_qi = IDX_HEADS * IDX_DIM
    topk = min(TOPK_MAX, l // 4)
    ts = ATT_TILE
    scale = HEAD_DIM ** -0.5 * LOG2E
    idx_scale = (IDX_HEADS * IDX_DIM) ** -0.5

    lane = jnp.arange(LANES)
    gsum = (lane[:, None] // HEAD_DIM == lane[None, :] // HEAD_DIM).astype(BF16)
    u2 = (jnp.arange(ts)[None, :] > jnp.arange(ts)[:, None]).astype(BF16)
    tab = _t5_table(rel_bias.astype(F32), ts)

    for layer in range(depth):
        w_all = _pack_w_in(jnp.swapaxes(w_in[layer], 0, 1), d_a, d_b, d_qi, scale, idx_scale)
        qg = jnp.tile(q_norm_gain[layer] * scale, h_a)[None, :]
        kg = jnp.tile(k_norm_gain[layer], h_a)[None, :]

        (qam, ka, vat, ga, qim, ki, wit, qbm, kb, vbt, gb) = _inproj(
            x, norm_gain[layer][None, :], w_all, qg, kg, gsum, d_a, d_b, d_qi)
        oa = _dsa(ki, qim, wit, ka, vat, qam, ga, tab, topk)
        w_o = w_out[layer].astype(BF16)
        x = _stick(kb, vbt, qbm, gb, u2, x, oa, w_o[:d_a], w_o[d_a:])
    return x
```

```python
import functools
import math

import jax
import jax.numpy as jnp
import numpy as np
from jax import lax
from jax.experimental import pallas as pl
from jax.experimental.pallas import tpu as pltpu

F32 = jnp.float32
BF16 = jnp.bfloat16
I32 = jnp.int32

HEAD_DIM = 64
IDX_HEADS = 16
IDX_DIM = 64
TOPK_MAX = 256
NUM_BUCKETS = 32
MAX_DISTANCE = 128
RMS_EPS = 1e-6

LANES = 128
SUBLANES = 8
NEG_BIG = -1e30
EXP2_UNDERFLOW = -150.0
LOG2E = math.log2(math.e)
BISECT_CAP = 300
BISECT_BLIND = 16
BISECT_GROUP = 2
BISECT_INTERP = 8
INTERP_MARGIN = 0.05

VMEM_LIMIT_BYTES = 56 * 1024 * 1024
TOKEN_TILE = 512
ATT_TILE = 256


def _params(*sem):
    return pltpu.CompilerParams(dimension_semantics=sem, vmem_limit_bytes=VMEM_LIMIT_BYTES)


def _pair(h):
    return slice((h // 2) * LANES, (h // 2 + 1) * LANES)


def _store_head_masked(dst_ref, src):
    rows = src.shape[0]
    low = lax.broadcasted_iota(I32, (rows, LANES), 1) < HEAD_DIM
    for h in range(dst_ref.shape[1]):
        keep = low if h % 2 == 0 else jnp.logical_not(low)
        dst_ref[0, h] = jnp.where(keep, src[:, _pair(h)], 0.0).T.astype(BF16)


def _pack_w_in_kernel(wt_ref, o_ref, *, d_a, d_b, d_qi, scale, idx_scale):
    o_ki = 4 * d_a + d_qi
    o_wi = o_ki + IDX_DIM
    o_b = o_wi + IDX_HEADS
    d = wt_ref.shape[1]

    def put(dst, rows):
        o_ref[:, dst:dst + LANES] = rows.T.astype(BF16)

    for c in range(0, o_ki, LANES):
        put(c, wt_ref[c:c + LANES, :])
    k_idx = wt_ref[o_ki:o_wi, :]
    put(o_ki, jnp.concatenate([k_idx, k_idx], axis=0))
    w_idx = wt_ref[o_wi:o_b, :] * idx_scale
    put(o_ki + LANES, jnp.concatenate([w_idx, jnp.zeros((LANES - IDX_HEADS, d), F32)], axis=0))
    for c in range(0, 4 * d_b, LANES):
        rows = wt_ref[o_b + c:o_b + c + LANES, :]
        put(o_ki + 2 * LANES + c, rows * scale if c < d_b else rows)


def _pack_w_in(w_t, d_a, d_b, d_qi, scale, idx_scale):
    p, d = w_t.shape
    width = p + 2 * LANES - IDX_DIM - IDX_HEADS
    return pl.pallas_call(
        functools.partial(_pack_w_in_kernel, d_a=d_a, d_b=d_b, d_qi=d_qi, scale=scale,
                          idx_scale=idx_scale),
        out_shape=jax.ShapeDtypeStruct((d, width), BF16),
        compiler_params=pltpu.CompilerParams(vmem_limit_bytes=VMEM_LIMIT_BYTES),
        name="pack_w_in",
    )(w_t)


def _inproj_kernel(x_ref, gain_ref, w_ref, qg_ref, kg_ref, gsum_ref,
                   qa_ref, ka_ref, vat_ref, ga_ref, qi_ref, ki_ref, wit_ref,
                   qb_ref, kb_ref, vbt_ref, gb_ref, *, d_a, d_b, d_qi):
    x = x_ref[0]
    ms = jnp.mean(x * x, axis=-1, keepdims=True)
    h = (x * lax.rsqrt(ms + RMS_EPS) * gain_ref[...]).astype(BF16)
    ts = vat_ref.shape[3]

    def proj(c0, width):
        return jnp.dot(h, w_ref[:, c0:c0 + width], preferred_element_type=F32)

    def head_norm(y, g):
        sq = (y * y).astype(BF16)
        ones = gsum_ref[...]
        ssum = jnp.concatenate(
            [jnp.dot(sq[:, c:c + LANES], ones, preferred_element_type=F32)
             for c in range(0, y.shape[1], LANES)], axis=1)
        return y * lax.rsqrt(ssum * (1.0 / HEAD_DIM) + RMS_EPS) * g

    def silu(g):
        return g * (1.0 / (1.0 + jnp.exp(-g)))

    def store_key_blocks_t(dst_ref, v):
        vt = v.T.astype(BF16)
        for c in range(dst_ref.shape[1]):
            dst_ref[0, c] = vt[:, c * ts:(c + 1) * ts]

    c = 0
    _store_head_masked(qa_ref, head_norm(proj(c, d_a), qg_ref[...])); c += d_a
    ka_ref[0] = head_norm(proj(c, d_a), kg_ref[...]).astype(BF16); c += d_a
    store_key_blocks_t(vat_ref, proj(c, d_a)); c += d_a
    ga_ref[0] = silu(proj(c, d_a)); c += d_a
    _store_head_masked(qi_ref, proj(c, d_qi)); c += d_qi
    ki_ref[0] = proj(c, LANES).astype(BF16); c += LANES
    wit_ref[0] = proj(c, LANES).T[:IDX_HEADS, :]; c += LANES
    _store_head_masked(qb_ref, proj(c, d_b))
    store_key_blocks_t(vbt_ref, proj(c + 2 * d_b, d_b))
    gb_ref[0] = silu(proj(c + 3 * d_b, d_b))
    kb_ref[0] = proj(c + d_b, d_b).astype(BF16)


def _inproj(x, gain, w_all, qg, kg, gsum, d_a, d_b, d_qi):
    b, l, d = x.shape
    tm = TOKEN_TILE
    ts = ATT_TILE
    nb = l // ts
    row = lambda width: pl.BlockSpec((1, tm, width), lambda i, j: (i, j, 0))
    const = lambda shape: pl.BlockSpec(shape, lambda i, j: (0, 0))
    vt_spec = lambda ch: pl.BlockSpec((1, tm // ts, ch, ts), lambda i, j: (i, j, 0, 0))
    tok = lambda width, dt: jax.ShapeDtypeStruct((b, l, width), dt)
    heads = lambda width: jax.ShapeDtypeStruct((b, width // HEAD_DIM, LANES, l), BF16)
    heads_spec = lambda width: pl.BlockSpec((1, width // HEAD_DIM, LANES, tm), lambda i, j: (i, 0, 0, j))
    out_shapes = [
        heads(d_a), tok(d_a, BF16), jax.ShapeDtypeStruct((b, nb, d_a, ts), BF16), tok(d_a, F32),
        heads(d_qi), tok(LANES, BF16), jax.ShapeDtypeStruct((b, IDX_HEADS, l), F32),
        heads(d_b), tok(d_b, BF16), jax.ShapeDtypeStruct((b, nb, d_b, ts), BF16), tok(d_b, F32),
    ]
    out_specs = [heads_spec(d_a), row(d_a), vt_spec(d_a), row(d_a), heads_spec(d_qi), row(LANES),
                 pl.BlockSpec((1, IDX_HEADS, tm), lambda i, j: (i, 0, j)),
                 heads_spec(d_b), row(d_b), vt_spec(d_b), row(d_b)]
    return pl.pallas_call(
        functools.partial(_inproj_kernel, d_a=d_a, d_b=d_b, d_qi=d_qi),
        grid=(b, l // tm),
        in_specs=[row(d), const(gain.shape),
                  pl.BlockSpec(w_all.shape, lambda i, j: (0, 0), pipeline_mode=pl.Buffered(1)),
                  const(qg.shape), const(kg.shape), const(gsum.shape)],
        out_specs=out_specs,
        out_shape=out_shapes,
        compiler_params=_params("arbitrary", "arbitrary"),
        name="inproj",
    )(x, gain, w_all, qg, kg, gsum)


def _t5_large_thresholds():
    max_exact = NUM_BUCKETS // 2
    d = np.arange(max_exact, 2 * MAX_DISTANCE + 1)
    large = max_exact + (np.log(d.astype(np.float32) / np.float32(max_exact))
                         / np.float32(math.log(MAX_DISTANCE / max_exact))
                         * np.float32(NUM_BUCKETS - max_exact)).astype(np.int32)
    large = np.minimum(large, NUM_BUCKETS - 1)
    return [int(d[np.argmax(large >= k)]) for k in range(max_exact + 1, NUM_BUCKETS)]


def _t5_bucket(dist):
    max_exact = NUM_BUCKETS // 2
    d = jnp.maximum(dist, 0)
    large = jnp.full(d.shape, max_exact, I32)
    for first in _t5_large_thresholds():
        large = large + (d >= first).astype(I32)
    return jnp.where(d < max_exact, d, large)


def _t5_table_kernel(rb_ref, tab_ref, *, n_heads):
    ts = tab_ref.shape[2]
    rows = 16
    far = _t5_large_thresholds()[-1]
    t_rel = lax.broadcasted_iota(I32, (rows, ts), 1)
    s_rel = lax.broadcasted_iota(I32, (rows, ts), 0)

    for i in range(3):
        n_far = max(0, min(ts, i * ts - far + 1)) // rows

        def fill(c, _, i=i):
            r0 = pl.multiple_of(c * rows, rows)
            for h in range(n_heads):
                tab_ref[i, h, pl.ds(r0, rows), :] = jnp.full((rows, ts), rb_ref[NUM_BUCKETS - 1, h] * LOG2E, F32)
            return 0

        def compute(c, _, i=i):
            r0 = pl.multiple_of(c * rows, rows)
            bucket = _t5_bucket(t_rel - (r0 + s_rel) + i * ts)
            acc = [jnp.zeros((rows, ts), F32) for _ in range(n_heads)]
            for j in range(NUM_BUCKETS):
                hit = bucket == j
                acc = [jnp.where(hit, rb_ref[j, h] * LOG2E, acc[h]) for h in range(n_heads)]
            for h in range(n_heads):
                tab_ref[i, h, pl.ds(r0, rows), :] = acc[h]
            return 0

        lax.fori_loop(0, n_far, fill, 0)
        lax.fori_loop(n_far, ts // rows, compute, 0)


def _t5_table(rel_bias, ts):
    n_heads = rel_bias.shape[1]
    return pl.pallas_call(
        functools.partial(_t5_table_kernel, n_heads=n_heads),
        in_specs=[pl.BlockSpec(memory_space=pltpu.SMEM)],
        out_specs=pl.BlockSpec(memory_space=pltpu.VMEM),
        out_shape=jax.ShapeDtypeStruct((3, n_heads, ts, ts), F32),
        compiler_params=pltpu.CompilerParams(vmem_limit_bytes=VMEM_LIMIT_BYTES),
        name="t5_table",
    )(rel_bias)


def _dsa_kernel(ki_ref, qin_ref, witn_ref, ka_ref, vat_ref, qam_ref, ga_ref, tab_ref, o_ref,
                sc_ref, mm_ref, madd_ref, m_ref, l_ref, acc_ref, lg_ref, bmax_ref, *, topk, n_heads):
    tq = o_ref.shape[1]
    ts = tq
    q_blk = pl.program_id(1)
    nblk = q_blk + 1
    cur = q_blk % 2
    nxt = 1 - cur
    s_rel = lax.broadcasted_iota(I32, (ts, tq), 0)
    t_rel = lax.broadcasted_iota(I32, (ts, tq), 1)
    t_idx = q_blk * tq + t_rel

    def block_start(j):
        return pl.multiple_of(j * ts, ts)

    all_selected = nblk * tq <= topk
    n_sel = jnp.where(all_selected, 0, nblk)

    @pl.when(all_selected)
    def _():
        mm_ref[cur] = jnp.zeros(mm_ref.shape[1:], F32)

        def body(j, _):
            s0 = block_start(j)
            madd_ref[pl.ds(s0, ts), :] = jnp.where(s0 + s_rel > t_idx, NEG_BIG, 0.0)
            return 0

        lax.fori_loop(0, nblk, body, 0)

    vmin = jnp.min(mm_ref[cur, 0], axis=0, keepdims=True)
    vmax = jnp.max(mm_ref[cur, 1], axis=0, keepdims=True)

    def count(pred_fn):
        def body(j, c):
            s0 = block_start(j)
            hit = pred_fn(sc_ref[cur, pl.ds(s0, ts), :], s0).reshape(ts // SUBLANES, SUBLANES, tq)
            c = list(c)
            for r in range(ts // SUBLANES):
                c[r % 4] = jnp.where(hit[r], c[r % 4] + 1, c[r % 4])
            return tuple(c)
        z = jnp.zeros((SUBLANES, tq), I32)
        c8 = lax.fori_loop(0, n_sel, body, (z, z, z, z))
        return jnp.sum(c8[0] + c8[1] + c8[2] + c8[3], axis=0, keepdims=True)

    def midpoint(lo, hi):
        return 0.5 * lo + 0.5 * hi

    def searching(lo, hi, cnt_lo):
        mid = midpoint(lo, hi)
        return (cnt_lo > topk) & (mid > lo) & (mid < hi)

    def status(lo, hi, cnt_lo):
        flags = 2.0 * searching(lo, hi, cnt_lo).astype(F32) + (cnt_lo > topk).astype(F32)
        return jnp.max(flags)

    n_causal = t_idx[0:1, :] + 1
    lo0 = jnp.where(n_causal > topk, vmin, -jnp.inf)
    hi0 = vmax + (jnp.abs(vmax) * 2.0 ** -20 + 1e-30)

    def bisect_cond(carry):
        it, _, _, _, state = carry
        return (state >= 2) & (it < BISECT_CAP)

    def bisect_step(_, carry):
        lo, hi, cnt_lo = carry
        go = searching(lo, hi, cnt_lo)
        mid = midpoint(lo, hi)
        cnt = count(lambda blk, s0: blk >= mid)
        take = go & (cnt >= topk)
        return (jnp.where(take, mid, lo), jnp.where(go & (cnt < topk), mid, hi),
                jnp.where(take, cnt, cnt_lo))

    def bisect_body(carry):
        it, lo, hi, cnt_lo, _ = carry
        lo, hi, cnt_lo = lax.fori_loop(0, BISECT_GROUP, bisect_step, (lo, hi, cnt_lo))
        return it + BISECT_GROUP, lo, hi, cnt_lo, status(lo, hi, cnt_lo)

    def interp_step(_, carry):
        lo, hi, cnt_lo, cnt_hi = carry
        frac = (cnt_lo - topk).astype(F32) + 0.5
        frac = frac / jnp.maximum(cnt_lo - cnt_hi, 1).astype(F32)
        frac = jnp.clip(frac, INTERP_MARGIN, 1.0 - INTERP_MARGIN)
        base = jnp.maximum(lo, vmin)
        probe = base + (hi - base) * frac
        go = (cnt_lo > topk) & (probe > lo) & (probe < hi)
        cnt = count(lambda blk, s0: blk >= probe)
        take = go & (cnt >= topk)
        drop = go & (cnt < topk)
        return (jnp.where(take, probe, lo), jnp.where(drop, probe, hi),
                jnp.where(take, cnt, cnt_lo), jnp.where(drop, cnt, cnt_hi))

    n_interp = jnp.where(all_selected, 0, BISECT_INTERP)
    n_blind = jnp.where(all_selected, 0, BISECT_BLIND - BISECT_INTERP)
    lo, hi, cnt_lo, _ = lax.fori_loop(0, n_interp, interp_step,
                                      (lo0, hi0, n_causal, jnp.zeros_like(n_causal)))
    lo, hi, cnt_lo = lax.fori_loop(0, n_blind, bisect_step, (lo, hi, cnt_lo))
    _, thr, hi, cnt_thr, state = lax.while_loop(
        bisect_cond, bisect_body,
        (jnp.int32(BISECT_BLIND), lo, hi, cnt_lo,
         jnp.where(all_selected, 0, status(lo, hi, cnt_lo))))

    tie = state >= 1

    @pl.when(jnp.logical_not(tie))
    def _():
        def body(j, _):
            s0 = block_start(j)
            sel = (sc_ref[cur, pl.ds(s0, ts), :] >= thr) & (s0 + s_rel <= t_idx)
            madd_ref[pl.ds(s0, ts), :] = jnp.where(sel, 0.0, NEG_BIG)
            return 0

        lax.fori_loop(0, n_sel, body, 0)

    @pl.when(tie)
    def _():
        need = topk - count(lambda blk, s0: blk > thr)
        n_bits = sc_ref.shape[1].bit_length()

        def jbody(i, bound):
            cand = bound + jnp.left_shift(jnp.int32(1), n_bits - 1 - i)
            cnt = count(lambda blk, s0: (blk == thr) & (s0 + s_rel < cand))
            return jnp.where(cnt <= need, cand, bound)

        bound = lax.fori_loop(0, n_bits, jbody, jnp.zeros((1, tq), I32))

        def body(j, _):
            s0 = block_start(j)
            blk = sc_ref[cur, pl.ds(s0, ts), :]
            sel = ((blk > thr) | ((blk == thr) & (s0 + s_rel < bound))) & (s0 + s_rel <= t_idx)
            madd_ref[pl.ds(s0, ts), :] = jnp.where(sel, 0.0, NEG_BIG)
            return 0

        lax.fori_loop(0, n_sel, body, 0)

    def reset_accumulators():
        m_ref[...] = jnp.full(m_ref.shape, NEG_BIG, F32)
        l_ref[...] = jnp.zeros(l_ref.shape, F32)
        acc_ref[...] = jnp.zeros(acc_ref.shape, F32)

    @pl.when((pl.program_id(0) == 0) & (q_blk == 0))
    def _():
        reset_accumulators()

    def attend(j):
        s0 = block_start(j)
        madd = madd_ref[pl.ds(s0, ts), :]
        near = jnp.minimum(q_blk - j, 2)
        for h in range(n_heads):
            lg = jnp.dot(ka_ref[0, pl.ds(s0, ts), _pair(h)], qam_ref[0, h],
                         preferred_element_type=F32)
            lg = lg + tab_ref[near, h] + madd
            lg_ref[h] = lg
            bmax_ref[h] = jnp.max(lg, axis=0, keepdims=True)
        for h in range(n_heads):
            rows = slice(h * HEAD_DIM, (h + 1) * HEAD_DIM)
            m_old = m_ref[h]
            m_new = jnp.maximum(m_old, bmax_ref[h])
            alpha = jnp.exp2(m_old - m_new)
            p = jnp.exp2(lg_ref[h] - m_new)
            l_ref[h] = alpha * l_ref[h] + jnp.sum(p, axis=0, keepdims=True)
            pv = jnp.dot(vat_ref[0, j, rows, :], p.astype(BF16), preferred_element_type=F32)
            acc_ref[rows, :] = alpha * acc_ref[rows, :] + pv
            m_ref[h] = m_new

    def index_next(j, carry, diagonal):
        vmin8, vmax8 = carry
        s0 = block_start(j)
        kblk = ki_ref[0, pl.ds(s0, ts), :]
        score = jnp.zeros((ts, tq), F32)
        for h in range(IDX_HEADS):
            dots = jnp.dot(kblk, qin_ref[0, h], preferred_element_type=F32)
            score = score + jnp.maximum(dots, 0.0) * witn_ref[0, h:h + 1, :]
        if diagonal:
            future = s_rel > t_rel
            low = jnp.where(future, jnp.inf, score)
            score = jnp.where(future, -jnp.inf, score)
        else:
            low = score
        sc_ref[nxt, pl.ds(s0, ts), :] = score
        groups = (ts // SUBLANES, SUBLANES, tq)
        return (jnp.minimum(vmin8, low.reshape(groups).min(axis=0)),
                jnp.maximum(vmax8, score.reshape(groups).max(axis=0)))

    def write_output():
        for h in range(n_heads):
            rows = slice(h * HEAD_DIM, (h + 1) * HEAD_DIM)
            acc_ref[rows, :] = acc_ref[rows, :] * (1.0 / l_ref[h])
        o_ref[0] = (acc_ref[...].T * ga_ref[0]).astype(BF16)
        reset_accumulators()

    has_next = q_blk + 1 < pl.num_programs(1)

    @pl.when(has_next)
    def _():
        def pair(i, carry):
            for j in (2 * i, 2 * i + 1):
                attend(j)
                carry = index_next(j, carry, diagonal=False)
            return carry

        def single(j, carry):
            attend(j)
            return index_next(j, carry, diagonal=False)

        def quad(i, carry):
            for j in (4 * i, 4 * i + 1, 4 * i + 2, 4 * i + 3):
                attend(j)
                carry = index_next(j, carry, diagonal=False)
            return carry

        carry = lax.fori_loop(0, nblk // 4, quad, (jnp.full((SUBLANES, tq), jnp.inf, F32),
                                                   jnp.full((SUBLANES, tq), -jnp.inf, F32)))
        carry = lax.fori_loop(2 * (nblk // 4), nblk // 2, pair, carry)
        carry = lax.fori_loop(2 * (nblk // 2), nblk, single, carry)
        vmin8, vmax8 = index_next(nblk, carry, diagonal=True)
        mm_ref[nxt, 0] = vmin8
        mm_ref[nxt, 1] = vmax8
        write_output()

    @pl.when(jnp.logical_not(has_next))
    def _():
        def body(j, _):
            attend(j)
            return 0

        lax.fori_loop(0, nblk, body, 0)
        write_output()


def _dsa(ki, qim, wit, ka, vat, qam, ga, tab, topk):
    b, l, d_a = ka.shape
    n_heads = d_a // HEAD_DIM
    tq = ATT_TILE
    nq = l // tq
    assert tq <= topk, "the first query tile must not need indexer scores"
    next_tile = lambda j: jnp.minimum(j + 1, nq - 1)
    kernel = functools.partial(_dsa_kernel, topk=topk, n_heads=n_heads)
    return pl.pallas_call(
        kernel,
        grid=(b, nq),
        in_specs=[
            pl.BlockSpec((1, l, LANES), lambda i, j: (i, 0, 0)),
            pl.BlockSpec((1, IDX_HEADS, LANES, tq), lambda i, j: (i, 0, 0, next_tile(j))),
            pl.BlockSpec((1, IDX_HEADS, tq), lambda i, j: (i, 0, next_tile(j))),
            pl.BlockSpec((1, l, d_a), lambda i, j: (i, 0, 0)),
            pl.BlockSpec((1, nq, d_a, tq), lambda i, j: (i, 0, 0, 0)),
            pl.BlockSpec((1, n_heads, LANES, tq), lambda i, j: (i, 0, 0, j)),
            pl.BlockSpec((1, tq, d_a), lambda i, j: (i, j, 0)),
            pl.BlockSpec(tab.shape, lambda i, j: (0, 0, 0, 0), pipeline_mode=pl.Buffered(1)),
        ],
        out_specs=pl.BlockSpec((1, tq, d_a), lambda i, j: (i, j, 0)),
        out_shape=jax.ShapeDtypeStruct((b, l, d_a), BF16),
        scratch_shapes=[
            pltpu.VMEM((2, l, tq), F32),
            pltpu.VMEM((2, 2, SUBLANES, tq), F32),
            pltpu.VMEM((l, tq), F32),
            pltpu.VMEM((n_heads, 1, tq), F32),
            pltpu.VMEM((n_heads, 1, tq), F32),
            pltpu.VMEM((d_a, tq), F32),
            pltpu.VMEM((n_heads, tq, tq), F32),
            pltpu.VMEM((n_heads, 1, tq), F32),
        ],
        compiler_params=_params("arbitrary", "arbitrary"),
        name="dsa",
    )(ki, qim, wit, ka, vat, qam, ga, tab)


def _stick_kernel(kb_ref, vbt_ref, qbm_ref, gb_ref, u_ref, x_ref, oa_ref, wa_ref, wb_ref, y_ref,
                  below_ref, acc_ref, z_ref, lb_ref, top_ref, *, n_heads):
    tq = y_ref.shape[1]
    ts = tq
    q_blk = pl.program_id(1)
    s_rel = lax.broadcasted_iota(I32, (ts, tq), 0)
    t_rel = lax.broadcasted_iota(I32, (ts, tq), 1)

    below_ref[...] = jnp.zeros(below_ref.shape, F32)
    acc_ref[...] = jnp.zeros(acc_ref.shape, F32)

    def block(j, diagonal):
        s0 = pl.multiple_of(j * ts, ts)
        strict = s_rel < t_rel

        def z_dot(h):
            return jnp.dot(kb_ref[0, pl.ds(s0, ts), _pair(h)], qbm_ref[0, h],
                           preferred_element_type=F32)

        def terms(z2):
            log_beta = jnp.minimum(z2, 0.0) - jnp.log2(1.0 + jnp.exp2(-jnp.abs(z2)))
            log_om = log_beta - z2
            if diagonal:
                log_om = jnp.where(strict, log_om, 0.0)
            suffix = jnp.dot(u_ref[...], log_om.astype(BF16), preferred_element_type=F32)
            return log_beta, log_om[0:1, :], suffix

        def finish(h, log_a, below):
            rows = slice(h * HEAD_DIM, (h + 1) * HEAD_DIM)
            a = jnp.exp2(log_a)
            if diagonal:
                a = jnp.where(strict, a, 0.0)
            pv = jnp.dot(vbt_ref[0, j, rows, :], a.astype(BF16), preferred_element_type=F32)
            acc_ref[rows, :] += pv * jnp.exp2(below)

        for h in range(n_heads):
            z_ref[h] = z_dot(h)
        below, worst = [], None
        for h in range(n_heads):
            log_beta, om_row, suffix = terms(z_ref[h])
            lb_ref[h] = log_beta + suffix
            below.append(below_ref[h])
            after = below[h] + om_row + suffix[0:1, :]
            below_ref[h] = after
            worst = after if worst is None else jnp.maximum(worst, after)
        top = jnp.max(worst)
        for h in range(n_heads):
            finish(h, lb_ref[h], below[h])
        return top

    @pl.when(q_blk == 0)
    def _():
        top_ref[0] = block(q_blk, diagonal=True)

    @pl.when(q_blk > 0)
    def _():
        block(q_blk, diagonal=True)
        top_ref[0] = block(q_blk - 1, diagonal=False)

    def cond(carry):
        i, top = carry
        return (i <= q_blk) & (top >= EXP2_UNDERFLOW)

    def body(carry):
        i, _ = carry
        return i + 1, block(q_blk - i, diagonal=False)

    lax.while_loop(cond, body, (jnp.int32(2), top_ref[0]))

    ob = (acc_ref[...].T * gb_ref[0]).astype(BF16)
    y_ref[0] = (x_ref[0]
                + jnp.dot(oa_ref[0], wa_ref[...], preferred_element_type=F32)
                + jnp.dot(ob, wb_ref[...], preferred_element_type=F32))


def _stick(kb, vbt, qbm, gb, u2, x, oa, wa, wb):
    b, l, d_b = kb.shape
    d = x.shape[2]
    n_heads = d_b // HEAD_DIM
    tq = ATT_TILE
    nq = l // tq
    return pl.pallas_call(
        functools.partial(_stick_kernel, n_heads=n_heads),
        grid=(b, nq),
        in_specs=[
            pl.BlockSpec((1, l, d_b), lambda i, j: (i, 0, 0)),
            pl.BlockSpec((1, nq, d_b, tq), lambda i, j: (i, 0, 0, 0)),
            pl.BlockSpec((1, n_heads, LANES, tq), lambda i, j: (i, 0, 0, j)),
            pl.BlockSpec((1, tq, d_b), lambda i, j: (i, j, 0)),
            pl.BlockSpec(u2.shape, lambda i, j: (0, 0)),
            pl.BlockSpec((1, tq, d), lambda i, j: (i, j, 0)),
            pl.BlockSpec((1, tq, oa.shape[2]), lambda i, j: (i, j, 0)),
            pl.BlockSpec(wa.shape, lambda i, j: (0, 0)),
            pl.BlockSpec(wb.shape, lambda i, j: (0, 0)),
        ],
        out_specs=pl.BlockSpec((1, tq, d), lambda i, j: (i, j, 0)),
        out_shape=jax.ShapeDtypeStruct((b, l, d), F32),
        scratch_shapes=[
            pltpu.VMEM((n_heads, 1, tq), F32),
            pltpu.VMEM((d_b, tq), F32),
            pltpu.VMEM((n_heads, tq, tq), F32),
            pltpu.VMEM((n_heads, tq, tq), F32),
            pltpu.SMEM((1,), F32),
        ],
        compiler_params=_params("arbitrary", "arbitrary"),
        name="stick",
    )(kb, vbt, qbm, gb, u2, x, oa, wa, wb)


def kernel(x, norm_gain, w_in, q_norm_gain, k_norm_gain, rel_bias, w_out):
    b, l, d = x.shape
    depth = w_in.shape[0]
    d_a = d // 2
    d_b = d - d_a
    h_a = d_a // HEAD_DIM
    d_qi = IDX_HEADS * IDX_DIM
    topk = min(TOPK_MAX, l // 4)
    ts = ATT_TILE
    scale = HEAD_DIM ** -0.5 * LOG2E
    idx_scale = (IDX_HEADS * IDX_DIM) ** -0.5

    lane = jnp.arange(LANES)
    gsum = (lane[:, None] // HEAD_DIM == lane[None, :] // HEAD_DIM).astype(BF16)
    u2 = (jnp.arange(ts)[None, :] > jnp.arange(ts)[:, None]).astype(BF16)
    tab = _t5_table(rel_bias.astype(F32), ts)

    for layer in range(depth):
        w_all = _pack_w_in(jnp.swapaxes(w_in[layer], 0, 1), d_a, d_b, d_qi, scale, idx_scale)
        qg = jnp.tile(q_norm_gain[layer] * scale, h_a)[None, :]
        kg = jnp.tile(k_norm_gain[layer], h_a)[None, :]

        (qam, ka, vat, ga, qim, ki, wit, qbm, kb, vbt, gb) = _inproj(
            x, norm_gain[layer][None, :], w_all, qg, kg, gsum, d_a, d_b, d_qi)
        oa = _dsa(ki, qim, wit, ka, vat, qam, ga, tab, topk)
        w_o = w_out[layer].astype(BF16)
        x = _stick(kb, vbt, qbm, gb, u2, x, oa, w_o[:d_a], w_o[d_a:])
    return x
```
